```python
import math
import jax
import jax.numpy as jnp
from jax import lax
import numpy as np

D_MODEL = 4096
BATCH = 4
SEQ = 2048
DEPTH = 1
DEC_BATCH = 128
DEC_SEQ = 4
PAST_LEN = 16384
PAGE_SIZE = 128

SSM_WIDTH = D_MODEL // 2
SSM_GROUP = 16
SSM_GROUPS = SSM_WIDTH // SSM_GROUP
SSM_STATE = 64
DT_MIN = 1e-3
DT_MAX = 1e-1
CONV_WIDTH = D_MODEL // 2
CONV_K = 3
IN_WIDTHS = (SSM_WIDTH, SSM_WIDTH, CONV_WIDTH, CONV_WIDTH, CONV_WIDTH, CONV_WIDTH, D_MODEL, D_MODEL)
IN_COLS = 2 * SSM_WIDTH + 4 * CONV_WIDTH + 2 * D_MODEL
EPS = 1e-6

kernel_name = "hybrid_s5_shortconv_gated_decoder_step"


def _rmsnorm(x, g):
    xf = x.astype(jnp.float32)
    y = xf * lax.rsqrt(jnp.mean(xf * xf, axis=-1, keepdims=True) + EPS)
    return (y * g.astype(jnp.float32)).astype(x.dtype)


def _s5(u, h0, lam_re, lam_im, log_dt, b_re, b_im, c_re, c_im, d_skip):
    n, l, _ = u.shape
    f32 = jnp.float32
    uf = u.astype(f32).reshape(n, l, SSM_GROUPS, SSM_GROUP)
    lam = lax.complex(lam_re.astype(f32), lam_im.astype(f32))
    dt = jnp.exp(log_dt.astype(f32))[:, None]
    lam_bar = jnp.exp(lam * dt)
    b = lax.complex(b_re.astype(f32), b_im.astype(f32))
    b_bar = ((lam_bar - 1.0) / lam)[..., None] * b
    bu = jnp.einsum('gph,nlgh->nlgp', b_bar, uf.astype(jnp.complex64))
    a = jnp.broadcast_to(lam_bar, bu.shape)

    def combine(e1, e2):
        a1, b1 = e1
        a2, b2 = e2
        return a1 * a2, a2 * b1 + b2

    a_cum, h = lax.associative_scan(combine, (a, bu), axis=1)
    h0c = lax.complex(h0[..., 0].astype(f32), h0[..., 1].astype(f32))
    h = h + a_cum * h0c[:, None]
    c = lax.complex(c_re.astype(f32), c_im.astype(f32))
    y = jnp.einsum('ghp,nlgp->nlgh', c, h).real \
        + d_skip.astype(f32).reshape(SSM_GROUPS, SSM_GROUP) * uf
    h_last = h[:, -1]
    new_state = jnp.stack([h_last.real, h_last.imag], axis=-1)
    return y.reshape(n, l, SSM_WIDTH).astype(u.dtype), new_state


def _short_conv(v, buf, w):
    l = v.shape[1]
    full = jnp.concatenate([buf.astype(v.dtype), v], axis=1)
    out = full[:, 0:l] * w[0]
    for k in range(1, CONV_K):
        out = out + full[:, k:k + l] * w[k]
    return out, full[:, -(CONV_K - 1):]


def _layer(x, c, ssm_h0, conv_buf, norm_g, w_ada, b_ada, w_in, lam_re, lam_im, log_dt,
           b_re, b_im, c_re, c_im, d_skip, w_glu, b_glu, w_pa, conv_w, w_pb, w_o):
    mod = jax.nn.silu(c) @ w_ada + b_ada
    shift, scale, gate = jnp.split(mod, 3, axis=-1)
    xn = _rmsnorm(x, norm_g) * (1.0 + scale[:, None]) + shift[:, None]
    proj = xn @ w_in
    split_points = np.cumsum(IN_WIDTHS)[:-1].tolist()
    u_a, z_a, h_b, c_b, b_b, z_b, g_a, g_b = jnp.split(proj, split_points, axis=-1)
    y_a, ssm_new = _s5(u_a, ssm_h0, lam_re, lam_im, log_dt, b_re, b_im, c_re, c_im, d_skip)
    y_a = jax.nn.gelu(y_a)
    y_a = y_a * jax.nn.sigmoid(y_a @ w_glu + b_glu)
    branch_a = (y_a * jax.nn.silu(z_a)) @ w_pa
    conv_out, conv_new = _short_conv(c_b * h_b, conv_buf, conv_w)
    branch_b = (b_b * conv_out * jax.nn.silu(z_b)) @ w_pb
    merged = jax.nn.sigmoid(g_a) * branch_a + jax.nn.sigmoid(g_b) * branch_b
    out = merged @ w_o
    return x + gate[:, None] * out, ssm_new, conv_new


def setup_inputs(seed: int = 0) -> dict:
    key = jax.random.key(seed)
    ks = jax.random.split(key, 32)
    f32 = jnp.float32
    nrm = lambda k, shape, s: (jax.random.normal(k, shape, f32) * s)
    lam_im_base = math.pi * jnp.arange(SSM_STATE, dtype=f32)
    return {
        "x_prompt": nrm(ks[0], (BATCH, SEQ, D_MODEL), 1.0),
        "x_sample": nrm(ks[1], (DEC_BATCH, DEC_SEQ, D_MODEL), 1.0),
        "state_ssm": nrm(ks[2], (DEPTH, DEC_BATCH, SSM_GROUPS, SSM_STATE, 2), 0.1),
        "state_conv": nrm(ks[3], (DEPTH, DEC_BATCH, CONV_K - 1, CONV_WIDTH), 1.0),
        "c_prompt": nrm(ks[4], (BATCH, D_MODEL), 1.0),
        "c_sample": nrm(ks[5], (DEC_BATCH, D_MODEL), 1.0),
        "norm_g": 1.0 + nrm(ks[6], (DEPTH, D_MODEL), 0.02),
        "w_ada": nrm(ks[7], (DEPTH, D_MODEL, 3 * D_MODEL), 0.5 * D_MODEL ** -0.5),
        "b_ada": nrm(ks[8], (DEPTH, 3 * D_MODEL), 0.02),
        "w_in": nrm(ks[9], (DEPTH, D_MODEL, IN_COLS), D_MODEL ** -0.5),
        "lam_re": -0.5 + nrm(ks[10], (DEPTH, SSM_GROUPS, SSM_STATE), 0.01),
        "lam_im": lam_im_base + nrm(ks[11], (DEPTH, SSM_GROUPS, SSM_STATE), 0.01),
        "log_dt": jax.random.uniform(ks[12], (DEPTH, SSM_GROUPS), f32,
                                     minval=math.log(DT_MIN), maxval=math.log(DT_MAX)),
        "b_re": nrm(ks[13], (DEPTH, SSM_GROUPS, SSM_STATE, SSM_GROUP), (2 * SSM_GROUP) ** -0.5),
        "b_im": nrm(ks[14], (DEPTH, SSM_GROUPS, SSM_STATE, SSM_GROUP), (2 * SSM_GROUP) ** -0.5),
        "c_re": nrm(ks[15], (DEPTH, SSM_GROUPS, SSM_GROUP, SSM_STATE), 0.5),
        "c_im": nrm(ks[16], (DEPTH, SSM_GROUPS, SSM_GROUP, SSM_STATE), 0.5),
        "d_skip": nrm(ks[17], (DEPTH, SSM_WIDTH), 1.0),
        "w_glu": nrm(ks[18], (DEPTH, SSM_WIDTH, SSM_WIDTH), SSM_WIDTH ** -0.5),
        "b_glu": nrm(ks[19], (DEPTH, SSM_WIDTH), 0.02),
        "w_pa": nrm(ks[20], (DEPTH, SSM_WIDTH, D_MODEL), SSM_WIDTH ** -0.5),
        "conv_w": nrm(ks[21], (DEPTH, CONV_K, CONV_WIDTH), CONV_K ** -0.5),
        "w_pb": nrm(ks[22], (DEPTH, CONV_WIDTH, D_MODEL), CONV_WIDTH ** -0.5),
        "w_o": nrm(ks[23], (DEPTH, D_MODEL, D_MODEL), D_MODEL ** -0.5),
        "final_g": 1.0 + nrm(ks[24], (D_MODEL,), 0.02),
    }


def reference(x_prompt, x_sample, state_ssm, state_conv, c_prompt, c_sample, norm_g, w_ada, b_ada,
              w_in, lam_re, lam_im, log_dt, b_re, b_im, c_re, c_im, d_skip, w_glu, b_glu, w_pa,
              conv_w, w_pb, w_o, final_g):
    n_p = x_prompt.shape[0]
    ssm_p0 = jnp.zeros((n_p, SSM_GROUPS, SSM_STATE, 2), jnp.float32)
    conv_p0 = jnp.zeros((n_p, CONV_K - 1, CONV_WIDTH), x_prompt.dtype)
    hp, hs = x_prompt, x_sample
    ssm_p_all, conv_p_all, ssm_s_all, conv_s_all = [], [], [], []
    for l in range(DEPTH):
        params = (norm_g[l], w_ada[l], b_ada[l], w_in[l], lam_re[l], lam_im[l], log_dt[l],
                  b_re[l], b_im[l], c_re[l], c_im[l], d_skip[l], w_glu[l], b_glu[l], w_pa[l],
                  conv_w[l], w_pb[l], w_o[l])
        hp, sp, cp = _layer(hp, c_prompt, ssm_p0, conv_p0, *params)
        hs, ss, cs = _layer(hs, c_sample, state_ssm[l], state_conv[l], *params)
        ssm_p_all.append(sp)
        conv_p_all.append(cp)
        ssm_s_all.append(ss)
        conv_s_all.append(cs)
    y_prompt = _rmsnorm(hp, final_g)
    y_sample = _rmsnorm(hs, final_g)
    return (y_prompt, y_sample, jnp.stack(ssm_p_all), jnp.stack(conv_p_all),
            jnp.stack(ssm_s_all), jnp.stack(conv_s_all))
```

```python
import functools

import jax
import jax.numpy as jnp
from jax import lax
from jax.experimental import pallas as pl
from jax.experimental.pallas import tpu as pltpu

F32 = jnp.float32
BF16 = jnp.bfloat16

D_MODEL = 4096
WIDTH = 2048
N_GROUPS = 128
GROUP = 16
N_STATE = 64
GROUPS_PER_BLOCK = 8
N_BLOCKS = N_GROUPS // GROUPS_PER_BLOCK
STATE_LANES = GROUPS_PER_BLOCK * N_STATE
LANES = 128
SEG_ROWS = 128
IN_COLS = 10 * WIDTH
EPS = 1e-6
VMEM_LIMIT = 56 * 1024 * 1024


def _cparams(n_axes):
    return pltpu.CompilerParams(dimension_semantics=("arbitrary",) * n_axes,
                                vmem_limit_bytes=VMEM_LIMIT)


def _sigmoid(x):
    return jax.nn.sigmoid(x)


def _silu(x):
    return x * jax.nn.sigmoid(x)


def _ada_kernel(c_ref, w_ref, b_ref, o_ref):
    a = _silu(c_ref[...]).astype(BF16)
    o_ref[...] = jnp.dot(a, w_ref[...].astype(BF16), preferred_element_type=F32) + b_ref[...]


def _ada(c_all, w_ada, b_ada, tn=512):
    rows = c_all.shape[0]
    n_out = w_ada.shape[1]
    return pl.pallas_call(
        _ada_kernel,
        grid=(n_out // tn,),
        in_specs=[pl.BlockSpec((rows, D_MODEL), lambda j: (0, 0)),
                  pl.BlockSpec((D_MODEL, tn), lambda j: (0, j)),
                  pl.BlockSpec((1, tn), lambda j: (0, j))],
        out_specs=pl.BlockSpec((rows, tn), lambda j: (0, j)),
        out_shape=jax.ShapeDtypeStruct((rows, n_out), F32),
        compiler_params=_cparams(1),
        name="ada_mod",
    )(c_all, w_ada, b_ada.reshape(1, n_out))


def _mod_rows(ref, k, bcast):
    return ref[pl.ds(k, 1), :] if bcast else ref[...]


def _inproj_kernel(x_ref, g_ref, sc_ref, sh_ref, w_ref, o_ref, xn_ref, *, tm, tn, bcast):
    j = pl.program_id(1)

    @pl.when(j == 0)
    def _():
        for k in range(tm // SEG_ROWS):
            rows = slice(k * SEG_ROWS, (k + 1) * SEG_ROWS)
            x = x_ref[rows, :]
            y = x * lax.rsqrt(jnp.mean(x * x, axis=-1, keepdims=True) + EPS) * g_ref[...]
            y = y * (1.0 + _mod_rows(sc_ref, k, bcast)) + _mod_rows(sh_ref, k, bcast)
            xn_ref[rows, :] = y.astype(BF16)

    acc = jnp.dot(xn_ref[...], w_ref[...], preferred_element_type=F32)
    seg = j // (WIDTH // tn)
    is_silu = jnp.logical_or(seg == 1, seg == 5)
    is_sig = seg >= 6

    @pl.when(is_silu)
    def _():
        o_ref[...] = _silu(acc)

    @pl.when(is_sig)
    def _():
        o_ref[...] = _sigmoid(acc)

    @pl.when(jnp.logical_not(jnp.logical_or(is_silu, is_sig)))
    def _():
        o_ref[...] = acc


def _inproj(x2d, n_planes, tm, norm_g, scale, shift, w_in_bf, bcast, tn=512):
    q = scale.shape[0]
    kern = functools.partial(_inproj_kernel, tm=tm, tn=tn, bcast=bcast)
    return pl.pallas_call(
        kern,
        grid=(n_planes, IN_COLS // tn),
        in_specs=[pl.BlockSpec((tm, D_MODEL), lambda s, j: (0, s)),
                  pl.BlockSpec((1, D_MODEL), lambda s, j: (0, 0)),
                  pl.BlockSpec((q, D_MODEL), lambda s, j: (0, 0)),
                  pl.BlockSpec((q, D_MODEL), lambda s, j: (0, 0)),
                  pl.BlockSpec((D_MODEL, tn), lambda s, j: (0, j))],
        out_specs=pl.BlockSpec((tm, tn), lambda s, j: (s, j)),
        out_shape=jax.ShapeDtypeStruct((n_planes * tm, IN_COLS), F32),
        scratch_shapes=[pltpu.VMEM((tm, D_MODEL), BF16)],
        compiler_params=_cparams(2),
        name="inproj",
    )(x2d, norm_g.reshape(1, D_MODEL), scale, shift, w_in_bf)


def _s5_matrices(t_len, lam_re, lam_im, log_dt, b_re, b_im, c_re, c_im):
    hi = lax.Precision.HIGHEST
    lam = lax.complex(lam_re.astype(F32), lam_im.astype(F32))
    z = lam * jnp.exp(log_dt.astype(F32))[:, None]
    lam_bar = jnp.exp(z)
    b_bar = ((lam_bar - 1.0) / lam)[..., None] * lax.complex(b_re.astype(F32), b_im.astype(F32))
    c = lax.complex(c_re.astype(F32), c_im.astype(F32))
    js = jnp.arange(t_len + 1, dtype=F32)
    pw = jnp.exp(z[None] * js[:, None, None])
    cp = c[None] * pw[:, :, None, :]

    kj = jnp.einsum('jgap,gph->jgah', cp[:t_len], b_bar, precision=hi).real
    kr = kj.reshape(t_len, N_BLOCKS, GROUPS_PER_BLOCK, GROUP, GROUP).transpose(1, 0, 2, 4, 3)
    eye = jnp.eye(GROUPS_PER_BLOCK, dtype=F32)
    ktb = kr[:, :, :, :, None, :] * eye[None, None, :, None, :, None]
    ktb = ktb.reshape(N_BLOCKS, t_len, LANES, LANES)
    k_rev = ktb[:, ::-1].reshape(N_BLOCKS, t_len * LANES, LANES).astype(BF16)

    def pair_tiles(x):
        xr = x.reshape(N_BLOCKS, 4, 2, t_len, GROUP, N_STATE).transpose(0, 3, 1, 2, 4, 5)
        eye2 = jnp.eye(2, dtype=F32)
        t = xr[:, :, :, :, :, None, :] * eye2[None, None, None, :, None, :, None]
        return t.reshape(N_BLOCKS, t_len, 4, 2 * GROUP, 2 * N_STATE).astype(BF16)

    bf = pw[t_len - 1 - jnp.arange(t_len)][:, :, :, None] * b_bar[None]
    bf = bf.transpose(1, 0, 3, 2)
    cf = cp[1:t_len + 1].transpose(1, 0, 2, 3)
    a_t = pw[t_len]
    return dict(
        k_rev=k_rev,
        b_re=pair_tiles(bf.real), b_im=pair_tiles(bf.imag),
        c_re=pair_tiles(cf.real), c_im=pair_tiles(-cf.imag),
        a_re=a_t.real.reshape(N_BLOCKS, 1, STATE_LANES), a_im=a_t.imag.reshape(N_BLOCKS, 1, STATE_LANES))


def _s5_kernel(u_ref, h0r_ref, h0i_ref, kr_ref, bre_ref, bim_ref, cre_ref, cim_ref, ar_ref, ai_ref, d_ref,
               y_ref, hr_out, hi_out,
               ucat, bend_r, bend_i, cend_r, cend_i, hl_r, hl_i, hp_r, hp_i, *, t_len, m, n_seq):
    blk = pl.program_id(0)

    @pl.when(blk == 0)
    def _():
        for ref in (bend_r, bend_i, cend_r, cend_i):
            ref[...] = jnp.zeros(ref.shape, ref.dtype)

    for s in range(t_len):
        ucat[:, s * LANES:(s + 1) * LANES] = u_ref[s].astype(BF16)
        for q in range(4):
            r0 = s * LANES + q * 2 * GROUP
            rows = slice(r0, r0 + 2 * GROUP)
            cols = slice(q * LANES, (q + 1) * LANES)
            bend_r[rows, cols] = bre_ref[0, s, q]
            bend_i[rows, cols] = bim_ref[0, s, q]
            cend_r[rows, cols] = cre_ref[0, s, q]
            cend_i[rows, cols] = cim_ref[0, s, q]

    uc = ucat[...]
    hl_r[...] = jnp.dot(uc, bend_r[...], preferred_element_type=F32)
    hl_i[...] = jnp.dot(uc, bend_i[...], preferred_element_type=F32)

    ar = ar_ref[0]
    ai = ai_ref[0]
    cps = m // n_seq
    if cps == 1:
        h_r = h0r_ref[0]
        h_i = h0i_ref[0]
        hp_r[...] = h_r
        hp_i[...] = h_i
        hr_out[0] = ar * h_r - ai * h_i + hl_r[...]
        hi_out[0] = ar * h_i + ai * h_r + hl_i[...]
    else:
        def body(c, carry):
            new = []
            for n in range(n_seq):
                h_r, h_i = carry[n]
                row = pl.ds(n * cps + c, 1)
                hp_r[row, :] = h_r
                hp_i[row, :] = h_i
                new.append((ar * h_r - ai * h_i + hl_r[row, :], ar * h_i + ai * h_r + hl_i[row, :]))
            return tuple(new)

        init = tuple((h0r_ref[0, n:n + 1, :], h0i_ref[0, n:n + 1, :]) for n in range(n_seq))
        fin = lax.fori_loop(0, cps, body, init)
        for n in range(n_seq):
            hr_out[0, n:n + 1, :] = fin[n][0]
            hi_out[0, n:n + 1, :] = fin[n][1]

    nt = (((1,), (1,)), ((), ()))
    ycar = (lax.dot_general(hp_r[...].astype(BF16), cend_r[...], nt, preferred_element_type=F32)
            + lax.dot_general(hp_i[...].astype(BF16), cend_i[...], nt, preferred_element_type=F32))
    dvec = d_ref[0]
    for t in range(t_len):
        yt = jnp.dot(ucat[:, :(t + 1) * LANES], kr_ref[0, (t_len - 1 - t) * LANES:, :],
                     preferred_element_type=F32)
        y_ref[t] = yt + ycar[:, t * LANES:(t + 1) * LANES] + dvec * u_ref[t]


def _s5(proj3, h0_re, h0_im, mats, d_skip, t_len, m, n_seq):
    kern = functools.partial(_s5_kernel, t_len=t_len, m=m, n_seq=n_seq)
    tile = (1, t_len, 4, 2 * GROUP, 2 * N_STATE)
    blk3 = lambda b: (b, 0, 0)
    blk5 = lambda b: (b, 0, 0, 0, 0)
    return pl.pallas_call(
        kern,
        grid=(N_BLOCKS,),
        in_specs=[pl.BlockSpec((t_len, m, LANES), lambda b: (0, 0, b)),
                  pl.BlockSpec((1, n_seq, STATE_LANES), blk3),
                  pl.BlockSpec((1, n_seq, STATE_LANES), blk3),
                  pl.BlockSpec((1, t_len * LANES, LANES), blk3),
                  pl.BlockSpec(tile, blk5), pl.BlockSpec(tile, blk5),
                  pl.BlockSpec(tile, blk5), pl.BlockSpec(tile, blk5),
                  pl.BlockSpec((1, 1, STATE_LANES), blk3),
                  pl.BlockSpec((1, 1, STATE_LANES), blk3),
                  pl.BlockSpec((1, 1, LANES), blk3)],
        out_specs=[pl.BlockSpec((t_len, m, LANES), lambda b: (0, 0, b)),
                   pl.BlockSpec((1, n_seq, STATE_LANES), blk3),
                   pl.BlockSpec((1, n_seq, STATE_LANES), blk3)],
        out_shape=[jax.ShapeDtypeStruct((t_len, m, WIDTH), F32),
                   jax.ShapeDtypeStruct((N_BLOCKS, n_seq, STATE_LANES), F32),
                   jax.ShapeDtypeStruct((N_BLOCKS, n_seq, STATE_LANES), F32)],
        scratch_shapes=[pltpu.VMEM((m, t_len * LANES), BF16),
                        pltpu.VMEM((t_len * LANES, STATE_LANES), BF16),
                        pltpu.VMEM((t_len * LANES, STATE_LANES), BF16),
                        pltpu.VMEM((t_len * LANES, STATE_LANES), BF16),
                        pltpu.VMEM((t_len * LANES, STATE_LANES), BF16),
                        pltpu.VMEM((m, STATE_LANES), F32), pltpu.VMEM((m, STATE_LANES), F32),
                        pltpu.VMEM((m, STATE_LANES), F32), pltpu.VMEM((m, STATE_LANES), F32)],
        compiler_params=_cparams(1),
        name="s5_scan",
    )(proj3, h0_re, h0_im, mats["k_rev"], mats["b_re"], mats["b_im"], mats["c_re"], mats["c_im"],
      mats["a_re"], mats["a_im"], d_skip.reshape(N_BLOCKS, 1, LANES))


def _conv_kernel(hb0, cb0, hb1, cb1, hb2, cb2, bb, szb, buf0, buf1, w_ref, bin_ref, v_ref, *, m, n_seq):
    s = pl.program_id(0)
    cps = m // n_seq
    v0 = hb0[...] * cb0[...]
    v1 = hb1[...] * cb1[...]
    v2 = hb2[...] * cb2[...]

    def prev_chunk(v, first_rows):
        if cps == 1:
            return first_rows
        out = pltpu.roll(v, 1, 0)
        rid = lax.broadcasted_iota(jnp.int32, v.shape, 0)
        for n in range(n_seq):
            out = jnp.where(rid == n * cps, first_rows[n:n + 1, :], out)
        return out

    b0 = buf0[...]
    b1 = buf1[...]
    vm1 = jnp.where(s >= 1, v1, prev_chunk(v1, b1))
    vm2 = jnp.where(s >= 2, v2, prev_chunk(v2, jnp.where(s == 0, b0, b1)))
    conv = vm2 * w_ref[0:1, :] + vm1 * w_ref[1:2, :] + v0 * w_ref[2:3, :]
    bin_ref[...] = (bb[...] * conv * szb[...]).astype(BF16)
    v_ref[...] = v0


def _conv_branch(proj, buf0, buf1, conv_w, t_len, m, n_seq, tn=256):
    nb = WIDTH // tn
    kern = functools.partial(_conv_kernel, m=m, n_seq=n_seq)

    def plane(seg, back):
        return pl.BlockSpec((m, tn), lambda s, j: ((s + t_len - back) % t_len, seg * nb + j))

    return pl.pallas_call(
        kern,
        grid=(t_len, nb),
        in_specs=[plane(2, 0), plane(3, 0), plane(2, 1), plane(3, 1), plane(2, 2), plane(3, 2),
                  plane(4, 0), plane(5, 0),
                  pl.BlockSpec((n_seq, tn), lambda s, j: (0, j)),
                  pl.BlockSpec((n_seq, tn), lambda s, j: (0, j)),
                  pl.BlockSpec((3, tn), lambda s, j: (0, j))],
        out_specs=[pl.BlockSpec((m, tn), lambda s, j: (s, j)),
                   pl.BlockSpec((m, tn), lambda s, j: (s, j))],
        out_shape=[jax.ShapeDtypeStruct((t_len * m, WIDTH), BF16),
                   jax.ShapeDtypeStruct((t_len * m, WIDTH), F32)],
        compiler_params=_cparams(2),
        name="conv_branch",
    )(proj, proj, proj, proj, proj, proj, proj, proj, buf0, buf1, conv_w)


def _glu_kernel(y_ref, sza_ref, w_ref, b_ref, o_ref, ya32, ya16, *, tn):
    j = pl.program_id(1)

    @pl.when(j == 0)
    def _():
        ya = jax.nn.gelu(y_ref[...])
        ya32[...] = ya
        ya16[...] = ya.astype(BF16)

    glu = jnp.dot(ya16[...], w_ref[...], preferred_element_type=F32) + b_ref[...]
    ya = ya32[:, pl.ds(pl.multiple_of(j * tn, tn), tn)]
    o_ref[...] = (ya * _sigmoid(glu) * sza_ref[...]).astype(BF16)


def _glu(y, proj, w_glu_bf, b_glu, n_tiles, tm, tn=512):
    nb = WIDTH // tn
    return pl.pallas_call(
        functools.partial(_glu_kernel, tn=tn),
        grid=(n_tiles, nb),
        in_specs=[pl.BlockSpec((tm, WIDTH), lambda i, j: (i, 0)),
                  pl.BlockSpec((tm, tn), lambda i, j: (i, nb + j)),
                  pl.BlockSpec((WIDTH, tn), lambda i, j: (0, j)),
                  pl.BlockSpec((1, tn), lambda i, j: (0, j))],
        out_specs=pl.BlockSpec((tm, tn), lambda i, j: (i, j)),
        out_shape=jax.ShapeDtypeStruct((n_tiles * tm, WIDTH), BF16),
        scratch_shapes=[pltpu.VMEM((tm, WIDTH), F32), pltpu.VMEM((tm, WIDTH), BF16)],
        compiler_params=_cparams(2),
        name="glu_gate",
    )(y, proj, w_glu_bf, b_glu.reshape(1, WIDTH))


def _merge_kernel(a_ref, b_ref, wa_ref, wb_ref, sga_ref, sgb_ref, o_ref):
    pa = jnp.dot(a_ref[...], wa_ref[...], preferred_element_type=F32)
    pb = jnp.dot(b_ref[...], wb_ref[...], preferred_element_type=F32)
    o_ref[...] = (sga_ref[...] * pa + sgb_ref[...] * pb).astype(BF16)


def _merge(a_in, b_in, proj, w_pa_bf, w_pb_bf, n_tiles, tm, tn=512):
    nb = WIDTH // tn
    return pl.pallas_call(
        _merge_kernel,
        grid=(n_tiles, D_MODEL // tn),
        in_specs=[pl.BlockSpec((tm, WIDTH), lambda i, j: (i, 0)),
                  pl.BlockSpec((tm, WIDTH), lambda i, j: (i, 0)),
                  pl.BlockSpec((WIDTH, tn), lambda i, j: (0, j)),
                  pl.BlockSpec((WIDTH, tn), lambda i, j: (0, j)),
                  pl.BlockSpec((tm, tn), lambda i, j: (i, 6 * nb + j)),
                  pl.BlockSpec((tm, tn), lambda i, j: (i, 8 * nb + j))],
        out_specs=pl.BlockSpec((tm, tn), lambda i, j: (i, j)),
        out_shape=jax.ShapeDtypeStruct((n_tiles * tm, D_MODEL), BF16),
        compiler_params=_cparams(2),
        name="merge",
    )(a_in, b_in, w_pa_bf, w_pb_bf, proj, proj)


def _outproj_kernel(m_ref, w_ref, x_ref, gate_ref, fg_ref, o_ref, acc_ref, *, tm, tiles_per_plane, bcast):
    i = pl.program_id(1)
    k = pl.program_id(2)

    @pl.when(k == 0)
    def _():
        acc_ref[...] = jnp.zeros(acc_ref.shape, F32)

    acc_ref[...] += jnp.dot(m_ref[...], w_ref[...], preferred_element_type=F32)

    @pl.when(k == pl.num_programs(2) - 1)
    def _():
        for seg in range(tm // SEG_ROWS):
            rows = slice(seg * SEG_ROWS, (seg + 1) * SEG_ROWS)
            gate = _mod_rows(gate_ref, i * (tm // SEG_ROWS) + seg, bcast)
            h = x_ref[rows, :] + gate * acc_ref[rows, :]
            o_ref[rows, :] = h * lax.rsqrt(jnp.mean(h * h, axis=-1, keepdims=True) + EPS) * fg_ref[...]


def _outproj(merged, w_o_bf, x2d, gate, final_g, n_planes, m, bcast, tm=256, tk=512):
    tiles = m // tm
    q = gate.shape[0]
    kern = functools.partial(_outproj_kernel, tm=tm, tiles_per_plane=tiles, bcast=bcast)
    return pl.pallas_call(
        kern,
        grid=(n_planes, tiles, D_MODEL // tk),
        in_specs=[pl.BlockSpec((tm, tk), lambda s, i, k: (s * tiles + i, k)),
                  pl.BlockSpec((tk, D_MODEL), lambda s, i, k: (k, 0)),
                  pl.BlockSpec((tm, D_MODEL), lambda s, i, k: (i, s)),
                  pl.BlockSpec((q, D_MODEL), lambda s, i, k: (0, 0)),
                  pl.BlockSpec((1, D_MODEL), lambda s, i, k: (0, 0))],
        out_specs=pl.BlockSpec((tm, D_MODEL), lambda s, i, k: (i, s)),
        out_shape=jax.ShapeDtypeStruct(x2d.shape, F32),
        scratch_shapes=[pltpu.VMEM((tm, D_MODEL), F32)],
        compiler_params=_cparams(3),
        name="outproj",
    )(merged, w_o_bf, x2d, gate, final_g.reshape(1, D_MODEL))


def _state_to_blocks(state):
    n = state.shape[0]
    re = state[..., 0].reshape(n, N_BLOCKS, STATE_LANES).transpose(1, 0, 2)
    im = state[..., 1].reshape(n, N_BLOCKS, STATE_LANES).transpose(1, 0, 2)
    return re, im


def _blocks_to_state(re, im):
    n = re.shape[1]
    re = re.transpose(1, 0, 2).reshape(n, N_GROUPS, N_STATE)
    im = im.transpose(1, 0, 2).reshape(n, N_GROUPS, N_STATE)
    return jnp.stack([re, im], axis=-1)


def _group(x2d, n_planes, m, n_seq, t_len, tile_rows, bcast, mod, ssm0, conv0, p):
    shift, scale, gate = mod
    proj = _inproj(x2d, n_planes, tile_rows, p["norm_g"], scale, shift, p["w_in"], bcast)
    h0_re, h0_im = _state_to_blocks(ssm0)
    y3, hr, hi = _s5(proj.reshape(t_len, m, IN_COLS), h0_re, h0_im, p["mats"][t_len], p["d_skip"],
                     t_len, m, n_seq)
    b_in, v = _conv_branch(proj, conv0[:, 0], conv0[:, 1], p["conv_w"], t_len, m, n_seq)
    n_tiles = (t_len * m) // tile_rows
    a_in = _glu(y3.reshape(t_len * m, WIDTH), proj, p["w_glu"], p["b_glu"], n_tiles, tile_rows)
    merged = _merge(a_in, b_in, proj, p["w_pa"], p["w_pb"], n_tiles, tile_rows)
    out = _outproj(merged, p["w_o"], x2d, gate, p["final_g"], n_planes, (t_len * m) // n_planes, bcast)
    cps = m // n_seq
    v3 = v.reshape(t_len, m, WIDTH)
    conv_new = v3[t_len - 2:, cps - 1::cps, :].transpose(1, 0, 2)
    return out, _blocks_to_state(hr, hi), conv_new


def kernel(x_prompt, x_sample, state_ssm, state_conv, c_prompt, c_sample, norm_g, w_ada, b_ada, w_in, lam_re, lam_im, log_dt, b_re, b_im, c_re, c_im, d_skip, w_glu, b_glu, w_pa, conv_w, w_pb, w_o, final_g):
    depth = norm_g.shape[0]
    assert depth == 1
    n_p, seq, _ = x_prompt.shape
    n_s, dec, _ = x_sample.shape
    t_p = 16
    t_s = dec
    m_p = n_p * seq // t_p
    assert m_p % SEG_ROWS == 0 and (seq // t_p) == SEG_ROWS and n_s == SEG_ROWS

    l = 0
    s5_args = (lam_re[l], lam_im[l], log_dt[l], b_re[l], b_im[l], c_re[l], c_im[l])
    p = dict(norm_g=norm_g[l], w_in=w_in[l].astype(BF16), d_skip=d_skip[l], conv_w=conv_w[l],
             w_glu=w_glu[l].astype(BF16), b_glu=b_glu[l], w_pa=w_pa[l].astype(BF16),
             w_pb=w_pb[l].astype(BF16), w_o=w_o[l].astype(BF16), final_g=final_g,
             mats={t_p: _s5_matrices(t_p, *s5_args), t_s: _s5_matrices(t_s, *s5_args)})

    n_c = n_p + n_s
    pad = (-n_c) % 16
    c_all = jnp.concatenate([c_prompt, c_sample, jnp.zeros((pad, D_MODEL), F32)], axis=0)
    mod = _ada(c_all, w_ada[l], b_ada[l])
    mod_p = tuple(mod[:n_p, k * D_MODEL:(k + 1) * D_MODEL] for k in range(3))
    mod_s = tuple(mod[n_p:n_c, k * D_MODEL:(k + 1) * D_MODEL] for k in range(3))

    ssm_p0 = jnp.zeros((n_p, N_GROUPS, N_STATE, 2), F32)
    conv_p0 = jnp.zeros((n_p, 2, WIDTH), F32)
    y_p, ssm_p, conv_p = _group(x_prompt.reshape(m_p, t_p * D_MODEL), t_p, m_p, n_p, t_p, m_p, True,
                                mod_p, ssm_p0, conv_p0, p)
    y_prompt = y_p.reshape(n_p, seq, D_MODEL)

    x_s = x_sample.transpose(1, 0, 2).reshape(dec * n_s, D_MODEL)
    y_s, ssm_s, conv_s = _group(x_s, 1, n_s, n_s, t_s, dec * n_s, False,
                                mod_s, state_ssm[l], state_conv[l], p)
    y_sample = y_s.reshape(dec, n_s, D_MODEL).transpose(1, 0, 2)

    return (y_prompt, y_sample, ssm_p[None], conv_p[None], ssm_s[None], conv_s[None])
```

```python
import functools

import jax
import jax.numpy as jnp
from jax import lax
from jax.experimental import pallas as pl
from jax.experimental.pallas import tpu as pltpu

F32 = jnp.float32
BF16 = jnp.bfloat16

D_MODEL = 4096
WIDTH = 2048
N_GROUPS = 128
GROUP = 16
N_STATE = 64
GROUPS_PER_BLOCK = 8
N_BLOCKS = N_GROUPS // GROUPS_PER_BLOCK
STATE_LANES = GROUPS_PER_BLOCK * N_STATE
LANES = 128
SEG_ROWS = 128
IN_COLS = 10 * WIDTH
EPS = 1e-6
VMEM_LIMIT = 56 * 1024 * 1024
NT_DIMS = (((1,), (1,)), ((), ()))


def _cparams(n_axes):
    return pltpu.CompilerParams(dimension_semantics=("arbitrary",) * n_axes,
                                vmem_limit_bytes=VMEM_LIMIT)


def _sigmoid(x):
    return jax.nn.sigmoid(x)


def _silu(x):
    return x * jax.nn.sigmoid(x)


def _col_blocks(w, tn):
    k, n = w.shape
    return w.reshape(k, n // tn, tn).transpose(1, 0, 2).astype(BF16)


def _ada_kernel(c_ref, w_ref, b_ref, o_ref):
    a = _silu(c_ref[...]).astype(BF16)
    o_ref[...] = jnp.dot(a, w_ref[...].astype(BF16), preferred_element_type=F32) + b_ref[...]


def _ada(c_all, w_ada, b_ada, tn=512):
    rows = c_all.shape[0]
    n_out = w_ada.shape[1]
    return pl.pallas_call(
        _ada_kernel,
        grid=(n_out // tn,),
        in_specs=[pl.BlockSpec((rows, D_MODEL), lambda j: (0, 0)),
                  pl.BlockSpec((D_MODEL, tn), lambda j: (0, j)),
                  pl.BlockSpec((1, tn), lambda j: (0, j))],
        out_specs=pl.BlockSpec((rows, tn), lambda j: (0, j)),
        out_shape=jax.ShapeDtypeStruct((rows, n_out), F32),
        compiler_params=_cparams(1),
        name="ada_mod",
    )(c_all, w_ada, b_ada.reshape(1, n_out))


def _mod_rows(ref, k, bcast):
    return ref[pl.ds(k, 1), :] if bcast else ref[...]


def _inproj_kernel(x_ref, g_ref, sc_ref, sh_ref, w_ref, o_ref, xn_ref, *, tm, tn, bcast):
    j = pl.program_id(1)

    @pl.when(j == 0)
    def _():
        for k in range(tm // SEG_ROWS):
            rows = slice(k * SEG_ROWS, (k + 1) * SEG_ROWS)
            x = x_ref[rows, :]
            y = x * lax.rsqrt(jnp.mean(x * x, axis=-1, keepdims=True) + EPS) * g_ref[...]
            y = y * (1.0 + _mod_rows(sc_ref, k, bcast)) + _mod_rows(sh_ref, k, bcast)
            xn_ref[rows, :] = y.astype(BF16)

    acc = jnp.dot(xn_ref[...], w_ref[...], preferred_element_type=F32)
    seg = j // (WIDTH // tn)
    is_silu = jnp.logical_or(seg == 1, seg == 5)
    is_sig = seg >= 6

    @pl.when(is_silu)
    def _():
        o_ref[...] = _silu(acc)

    @pl.when(is_sig)
    def _():
        o_ref[...] = _sigmoid(acc)

    @pl.when(jnp.logical_not(jnp.logical_or(is_silu, is_sig)))
    def _():
        o_ref[...] = acc


def _inproj(x2d, n_planes, tm, norm_g, scale, shift, w_in_blk, bcast):
    q = scale.shape[0]
    nb, _, tn = w_in_blk.shape
    kern = functools.partial(_inproj_kernel, tm=tm, tn=tn, bcast=bcast)
    return pl.pallas_call(
        kern,
        grid=(n_planes, nb),
        in_specs=[pl.BlockSpec((tm, D_MODEL), lambda s, j: (0, s)),
                  pl.BlockSpec((1, D_MODEL), lambda s, j: (0, 0)),
                  pl.BlockSpec((q, D_MODEL), lambda s, j: (0, 0)),
                  pl.BlockSpec((q, D_MODEL), lambda s, j: (0, 0)),
                  pl.BlockSpec((None, D_MODEL, tn), lambda s, j: (j, 0, 0))],
        out_specs=pl.BlockSpec((tm, tn), lambda s, j: (s, j)),
        out_shape=jax.ShapeDtypeStruct((n_planes * tm, IN_COLS), F32),
        scratch_shapes=[pltpu.VMEM((tm, D_MODEL), BF16)],
        compiler_params=_cparams(2),
        name="inproj",
    )(x2d, norm_g.reshape(1, D_MODEL), scale, shift, w_in_blk)


def _cmul(ar, ai, br, bi):
    return ar * br - ai * bi, ar * bi + ai * br


def _s5_kernel(u_ref, h0r_ref, h0i_ref, lr_ref, li_ref, ldt_ref, btr_ref, bti_ref, ctr_ref, cti_ref, d_ref,
               y_ref, hr_out, hi_out,
               ucat, bend_r, bend_i, cend_r, cend_i, krev, hl_r, hl_i, hp_r, hp_i, *, t_len, m, n_seq):
    blk = pl.program_id(0)

    @pl.when(blk == 0)
    def _():
        for ref in (bend_r, bend_i, cend_r, cend_i):
            ref[...] = jnp.zeros(ref.shape, ref.dtype)

    lam_r = lr_ref[0]
    lam_i = li_ref[0]
    dt = jnp.exp(ldt_ref[0])
    mag = jnp.exp(lam_r * dt)
    lb_r = mag * jnp.cos(lam_i * dt)
    lb_i = mag * jnp.sin(lam_i * dt)
    den = lam_r * lam_r + lam_i * lam_i
    nr = lb_r - 1.0
    co_r = (nr * lam_r + lb_i * lam_i) / den
    co_i = (lb_i * lam_r - nr * lam_i) / den
    bb_r, bb_i = _cmul(co_r, co_i, btr_ref[0], bti_ref[0])
    c_r = ctr_ref[0]
    c_i = cti_ref[0]
    pw = [(jnp.ones_like(lb_r), jnp.zeros_like(lb_r))]
    for _ in range(t_len):
        pw.append(_cmul(pw[-1][0], pw[-1][1], lb_r, lb_i))
    ar, ai = pw[t_len]

    lane = lax.broadcasted_iota(jnp.int32, (GROUP, LANES), 1)
    lo = lane < N_STATE

    def pair_tile(x, q):
        slab = x[:, q * LANES:(q + 1) * LANES]
        return jnp.concatenate([jnp.where(lo, slab, 0.0), jnp.where(lo, 0.0, slab)], axis=0).astype(BF16)

    grp_row = lax.broadcasted_iota(jnp.int32, (LANES, STATE_LANES), 0) // GROUP
    grp_lane = lax.broadcasted_iota(jnp.int32, (LANES, STATE_LANES), 1) // N_STATE
    same_group = grp_row == grp_lane

    def expand(x):
        return jnp.where(same_group, jnp.concatenate([x] * GROUPS_PER_BLOCK, axis=0), 0.0).astype(BF16)

    bbx_r = expand(bb_r)
    bbx_i = expand(bb_i)
    for s in range(t_len):
        ucat[:, s * LANES:(s + 1) * LANES] = u_ref[s].astype(BF16)
        be_r, be_i = _cmul(pw[t_len - 1 - s][0], pw[t_len - 1 - s][1], bb_r, bb_i)
        ce_r, ce_i = _cmul(pw[s + 1][0], pw[s + 1][1], c_r, c_i)
        for q in range(4):
            r0 = s * LANES + q * 2 * GROUP
            rows = slice(r0, r0 + 2 * GROUP)
            cols = slice(q * LANES, (q + 1) * LANES)
            bend_r[rows, cols] = pair_tile(be_r, q)
            bend_i[rows, cols] = pair_tile(be_i, q)
            cend_r[rows, cols] = pair_tile(ce_r, q)
            cend_i[rows, cols] = pair_tile(-ce_i, q)
        cl_r, cl_i = _cmul(pw[s][0], pw[s][1], c_r, c_i)
        k_lag = (lax.dot_general(bbx_r, expand(cl_r), NT_DIMS, preferred_element_type=F32)
                 - lax.dot_general(bbx_i, expand(cl_i), NT_DIMS, preferred_element_type=F32))
        krev[(t_len - 1 - s) * LANES:(t_len - s) * LANES, :] = k_lag.astype(BF16)

    uc = ucat[...]
    hl_r[...] = jnp.dot(uc, bend_r[...], preferred_element_type=F32)
    hl_i[...] = jnp.dot(uc, bend_i[...], preferred_element_type=F32)

    cps = m // n_seq
    if cps == 1:
        h_r = h0r_ref[0]
        h_i = h0i_ref[0]
        hp_r[...] = h_r
        hp_i[...] = h_i
        hr_out[0] = ar * h_r - ai * h_i + hl_r[...]
        hi_out[0] = ar * h_i + ai * h_r + hl_i[...]
    else:
        def body(c, carry):
            new = []
            for n in range(n_seq):
                h_r, h_i = carry[n]
                row = pl.ds(n * cps + c, 1)
                hp_r[row, :] = h_r
                hp_i[row, :] = h_i
                new.append((ar * h_r - ai * h_i + hl_r[row, :], ar * h_i + ai * h_r + hl_i[row, :]))
            return tuple(new)

        init = tuple((h0r_ref[0, n:n + 1, :], h0i_ref[0, n:n + 1, :]) for n in range(n_seq))
        fin = lax.fori_loop(0, cps, body, init)
        for n in range(n_seq):
            hr_out[0, n:n + 1, :] = fin[n][0]
            hi_out[0, n:n + 1, :] = fin[n][1]

    ycar = (lax.dot_general(hp_r[...].astype(BF16), cend_r[...], NT_DIMS, preferred_element_type=F32)
            + lax.dot_general(hp_i[...].astype(BF16), cend_i[...], NT_DIMS, preferred_element_type=F32))
    dvec = d_ref[0]
    for t in range(t_len):
        yt = jnp.dot(ucat[:, :(t + 1) * LANES], krev[(t_len - 1 - t) * LANES:, :],
                     preferred_element_type=F32)
        y_ref[t] = yt + ycar[:, t * LANES:(t + 1) * LANES] + dvec * u_ref[t]


def _s5_params(lam_re, lam_im, log_dt, b_re, b_im, c_re, c_im, d_skip):
    row = lambda x: x.astype(F32).reshape(N_BLOCKS, 1, STATE_LANES)
    ldt = jnp.broadcast_to(log_dt.astype(F32)[:, None], (N_GROUPS, N_STATE))
    bt = lambda x: (x.astype(F32).reshape(N_BLOCKS, GROUPS_PER_BLOCK, N_STATE, GROUP)
                    .transpose(0, 3, 1, 2).reshape(N_BLOCKS, GROUP, STATE_LANES))
    ct = lambda x: (x.astype(F32).reshape(N_BLOCKS, GROUPS_PER_BLOCK, GROUP, N_STATE)
                    .transpose(0, 2, 1, 3).reshape(N_BLOCKS, GROUP, STATE_LANES))
    return (row(lam_re), row(lam_im), row(ldt), bt(b_re), bt(b_im), ct(c_re), ct(c_im),
            d_skip.astype(F32).reshape(N_BLOCKS, 1, LANES))


def _s5(proj3, h0_re, h0_im, s5p, t_len, m, n_seq):
    kern = functools.partial(_s5_kernel, t_len=t_len, m=m, n_seq=n_seq)
    blk3 = lambda b: (b, 0, 0)
    row_spec = pl.BlockSpec((1, 1, STATE_LANES), blk3)
    mat_spec = pl.BlockSpec((1, GROUP, STATE_LANES), blk3)
    state_spec = pl.BlockSpec((1, n_seq, STATE_LANES), blk3)
    end_mat = pltpu.VMEM((t_len * LANES, STATE_LANES), BF16)
    chunk_state = pltpu.VMEM((m, STATE_LANES), F32)
    return pl.pallas_call(
        kern,
        grid=(N_BLOCKS,),
        in_specs=[pl.BlockSpec((t_len, m, LANES), lambda b: (0, 0, b)),
                  state_spec, state_spec,
                  row_spec, row_spec, row_spec,
                  mat_spec, mat_spec, mat_spec, mat_spec,
                  pl.BlockSpec((1, 1, LANES), blk3)],
        out_specs=[pl.BlockSpec((t_len, m, LANES), lambda b: (0, 0, b)), state_spec, state_spec],
        out_shape=[jax.ShapeDtypeStruct((t_len, m, WIDTH), F32),
                   jax.ShapeDtypeStruct((N_BLOCKS, n_seq, STATE_LANES), F32),
                   jax.ShapeDtypeStruct((N_BLOCKS, n_seq, STATE_LANES), F32)],
        scratch_shapes=[pltpu.VMEM((m, t_len * LANES), BF16),
                        end_mat, end_mat, end_mat, end_mat,
                        pltpu.VMEM((t_len * LANES, LANES), BF16),
                        chunk_state, chunk_state, chunk_state, chunk_state],
        compiler_params=_cparams(1),
        name="s5_scan",
    )(proj3, h0_re, h0_im, *s5p)


def _conv_kernel(hb0, cb0, hb1, cb1, hb2, cb2, bb, szb, buf0, buf1, w_ref, bin_ref, v_ref, *, m, n_seq):
    s = pl.program_id(0)
    cps = m // n_seq
    v0 = hb0[...] * cb0[...]
    v1 = hb1[...] * cb1[...]
    v2 = hb2[...] * cb2[...]

    def prev_chunk(v, first_rows):
        if cps == 1:
            return first_rows
        out = pltpu.roll(v, 1, 0)
        rid = lax.broadcasted_iota(jnp.int32, v.shape, 0)
        for n in range(n_seq):
            out = jnp.where(rid == n * cps, first_rows[n:n + 1, :], out)
        return out

    b0 = buf0[...]
    b1 = buf1[...]
    vm1 = jnp.where(s >= 1, v1, prev_chunk(v1, b1))
    vm2 = jnp.where(s >= 2, v2, prev_chunk(v2, jnp.where(s == 0, b0, b1)))
    conv = vm2 * w_ref[0:1, :] + vm1 * w_ref[1:2, :] + v0 * w_ref[2:3, :]
    bin_ref[...] = (bb[...] * conv * szb[...]).astype(BF16)
    v_ref[...] = v0


def _conv_branch(proj, buf0, buf1, conv_w, t_len, m, n_seq, tn=256):
    nb = WIDTH // tn
    kern = functools.partial(_conv_kernel, m=m, n_seq=n_seq)

    def plane(seg, back):
        return pl.BlockSpec((m, tn), lambda s, j: ((s + t_len - back) % t_len, seg * nb + j))

    return pl.pallas_call(
        kern,
        grid=(t_len, nb),
        in_specs=[plane(2, 0), plane(3, 0), plane(2, 1), plane(3, 1), plane(2, 2), plane(3, 2),
                  plane(4, 0), plane(5, 0),
                  pl.BlockSpec((n_seq, tn), lambda s, j: (0, j)),
                  pl.BlockSpec((n_seq, tn), lambda s, j: (0, j)),
                  pl.BlockSpec((3, tn), lambda s, j: (0, j))],
        out_specs=[pl.BlockSpec((m, tn), lambda s, j: (s, j)),
                   pl.BlockSpec((m, tn), lambda s, j: (s, j))],
        out_shape=[jax.ShapeDtypeStruct((t_len * m, WIDTH), BF16),
                   jax.ShapeDtypeStruct((t_len * m, WIDTH), F32)],
        compiler_params=_cparams(2),
        name="conv_branch",
    )(proj, proj, proj, proj, proj, proj, proj, proj, buf0, buf1, conv_w)


def _glu_kernel(y_ref, sza_ref, w_ref, b_ref, o_ref, ya32, ya16, *, tn):
    j = pl.program_id(1)

    @pl.when(j == 0)
    def _():
        ya = jax.nn.gelu(y_ref[...])
        ya32[...] = ya
        ya16[...] = ya.astype(BF16)

    glu = jnp.dot(ya16[...], w_ref[...], preferred_element_type=F32) + b_ref[...]
    ya = ya32[:, pl.ds(pl.multiple_of(j * tn, tn), tn)]
    o_ref[...] = (ya * _sigmoid(glu) * sza_ref[...]).astype(BF16)


def _glu(y, proj, w_glu_blk, b_glu, n_tiles, tm):
    nb, _, tn = w_glu_blk.shape
    return pl.pallas_call(
        functools.partial(_glu_kernel, tn=tn),
        grid=(n_tiles, nb),
        in_specs=[pl.BlockSpec((tm, WIDTH), lambda i, j: (i, 0)),
                  pl.BlockSpec((tm, tn), lambda i, j: (i, nb + j)),
                  pl.BlockSpec((None, WIDTH, tn), lambda i, j: (j, 0, 0)),
                  pl.BlockSpec((1, tn), lambda i, j: (0, j))],
        out_specs=pl.BlockSpec((tm, tn), lambda i, j: (i, j)),
        out_shape=jax.ShapeDtypeStruct((n_tiles * tm, WIDTH), BF16),
        scratch_shapes=[pltpu.VMEM((tm, WIDTH), F32), pltpu.VMEM((tm, WIDTH), BF16)],
        compiler_params=_cparams(2),
        name="glu_gate",
    )(y, proj, w_glu_blk, b_glu.reshape(1, WIDTH))


def _merge_kernel(a_ref, b_ref, wa_ref, wb_ref, sga_ref, sgb_ref, o_ref):
    pa = jnp.dot(a_ref[...], wa_ref[...], preferred_element_type=F32)
    pb = jnp.dot(b_ref[...], wb_ref[...], preferred_element_type=F32)
    o_ref[...] = (sga_ref[...] * pa + sgb_ref[...] * pb).astype(BF16)


def _merge(a_in, b_in, proj, w_pa_blk, w_pb_blk, n_tiles, tm):
    nbo, _, tn = w_pa_blk.shape
    nb = WIDTH // tn
    w_spec = pl.BlockSpec((None, WIDTH, tn), lambda i, j: (j, 0, 0))
    return pl.pallas_call(
        _merge_kernel,
        grid=(n_tiles, nbo),
        in_specs=[pl.BlockSpec((tm, WIDTH), lambda i, j: (i, 0)),
                  pl.BlockSpec((tm, WIDTH), lambda i, j: (i, 0)),
                  w_spec, w_spec,
                  pl.BlockSpec((tm, tn), lambda i, j: (i, 6 * nb + j)),
                  pl.BlockSpec((tm, tn), lambda i, j: (i, 8 * nb + j))],
        out_specs=pl.BlockSpec((tm, tn), lambda i, j: (i, j)),
        out_shape=jax.ShapeDtypeStruct((n_tiles * tm, D_MODEL), BF16),
        compiler_params=_cparams(2),
        name="merge",
    )(a_in, b_in, w_pa_blk, w_pb_blk, proj, proj)


def _outproj_kernel(m_ref, w_ref, x_ref, gate_ref, fg_ref, o_ref, *, tm, bcast):
    i = pl.program_id(1)
    k = pl.program_id(2)
    part = jnp.dot(m_ref[...], w_ref[...], preferred_element_type=F32)

    @pl.when(k == 0)
    def _():
        o_ref[...] = part

    @pl.when(k > 0)
    def _():
        o_ref[...] += part

    @pl.when(k == pl.num_programs(2) - 1)
    def _():
        for seg in range(tm // SEG_ROWS):
            rows = slice(seg * SEG_ROWS, (seg + 1) * SEG_ROWS)
            gate = _mod_rows(gate_ref, i * (tm // SEG_ROWS) + seg, bcast)
            h = x_ref[rows, :] + gate * o_ref[rows, :]
            o_ref[rows, :] = h * lax.rsqrt(jnp.mean(h * h, axis=-1, keepdims=True) + EPS) * fg_ref[...]


def _outproj(merged, w_o_bf, x2d, gate, final_g, n_planes, m, bcast, tm=512, tk=512):
    tiles = m // tm
    q = gate.shape[0]
    kern = functools.partial(_outproj_kernel, tm=tm, bcast=bcast)
    return pl.pallas_call(
        kern,
        grid=(n_planes, tiles, D_MODEL // tk),
        in_specs=[pl.BlockSpec((tm, tk), lambda s, i, k: (s * tiles + i, k)),
                  pl.BlockSpec((tk, D_MODEL), lambda s, i, k: (k, 0)),
                  pl.BlockSpec((tm, D_MODEL), lambda s, i, k: (i, s), pipeline_mode=pl.Buffered(1)),
                  pl.BlockSpec((q, D_MODEL), lambda s, i, k: (0, 0)),
                  pl.BlockSpec((1, D_MODEL), lambda s, i, k: (0, 0))],
        out_specs=pl.BlockSpec((tm, D_MODEL), lambda s, i, k: (i, s)),
        out_shape=jax.ShapeDtypeStruct(x2d.shape, F32),
        compiler_params=_cparams(3),
        name="outproj",
    )(merged, w_o_bf, x2d, gate, final_g.reshape(1, D_MODEL))


def _state_to_blocks(state):
    n = state.shape[0]
    re = state[..., 0].reshape(n, N_BLOCKS, STATE_LANES).transpose(1, 0, 2)
    im = state[..., 1].reshape(n, N_BLOCKS, STATE_LANES).transpose(1, 0, 2)
    return re, im


def _blocks_to_state(re, im):
    n = re.shape[1]
    re = re.transpose(1, 0, 2).reshape(n, N_GROUPS, N_STATE)
    im = im.transpose(1, 0, 2).reshape(n, N_GROUPS, N_STATE)
    return jnp.stack([re, im], axis=-1)


def _group(x2d, n_planes, m, n_seq, t_len, tile_rows, bcast, mod, ssm0, conv0, p):
    shift, scale, gate = mod
    proj = _inproj(x2d, n_planes, tile_rows, p["norm_g"], scale, shift, p["w_in"], bcast)
    h0_re, h0_im = _state_to_blocks(ssm0)
    y3, hr, hi = _s5(proj.reshape(t_len, m, IN_COLS), h0_re, h0_im, p["s5"], t_len, m, n_seq)
    b_in, v = _conv_branch(proj, conv0[:, 0], conv0[:, 1], p["conv_w"], t_len, m, n_seq)
    n_tiles = (t_len * m) // tile_rows
    a_in = _glu(y3.reshape(t_len * m, WIDTH), proj, p["w_glu"], p["b_glu"], n_tiles, tile_rows)
    merged = _merge(a_in, b_in, proj, p["w_pa"], p["w_pb"], n_tiles, tile_rows)
    out = _outproj(merged, p["w_o"], x2d, gate, p["final_g"], n_planes, (t_len * m) // n_planes, bcast)
    cps = m // n_seq
    v3 = v.reshape(t_len, m, WIDTH)
    conv_new = v3[t_len - 2:, cps - 1::cps, :].transpose(1, 0, 2)
    return out, _blocks_to_state(hr, hi), conv_new


def kernel(x_prompt, x_sample, state_ssm, state_conv, c_prompt, c_sample, norm_g, w_ada, b_ada, w_in, lam_re, lam_im, log_dt, b_re, b_im, c_re, c_im, d_skip, w_glu, b_glu, w_pa, conv_w, w_pb, w_o, final_g):
    depth = norm_g.shape[0]
    assert depth == 1
    n_p, seq, _ = x_prompt.shape
    n_s, dec, _ = x_sample.shape
    t_p = 16
    t_s = dec
    m_p = n_p * seq // t_p
    assert m_p % SEG_ROWS == 0 and (seq // t_p) == SEG_ROWS and n_s == SEG_ROWS

    l = 0
    p = dict(norm_g=norm_g[l], w_in=_col_blocks(w_in[l], 512), conv_w=conv_w[l],
             w_glu=_col_blocks(w_glu[l], 512), b_glu=b_glu[l], w_pa=_col_blocks(w_pa[l], 512),
             w_pb=_col_blocks(w_pb[l], 512), w_o=w_o[l].astype(BF16), final_g=final_g,
             s5=_s5_params(lam_re[l], lam_im[l], log_dt[l], b_re[l], b_im[l], c_re[l], c_im[l], d_skip[l]))

    n_c = n_p + n_s
    pad = (-n_c) % 16
    c_all = jnp.concatenate([c_prompt, c_sample, jnp.zeros((pad, D_MODEL), F32)], axis=0)
    mod = _ada(c_all, w_ada[l], b_ada[l])
    mod_p = tuple(mod[:n_p, k * D_MODEL:(k + 1) * D_MODEL] for k in range(3))
    mod_s = tuple(mod[n_p:n_c, k * D_MODEL:(k + 1) * D_MODEL] for k in range(3))

    ssm_p0 = jnp.zeros((n_p, N_GROUPS, N_STATE, 2), F32)
    conv_p0 = jnp.zeros((n_p, 2, WIDTH), F32)
    y_p, ssm_p, conv_p = _group(x_prompt.reshape(m_p, t_p * D_MODEL), t_p, m_p, n_p, t_p, m_p, True,
                                mod_p, ssm_p0, conv_p0, p)
    y_prompt = y_p.reshape(n_p, seq, D_MODEL)

    x_s = x_sample.transpose(1, 0, 2).reshape(dec * n_s, D_MODEL)
    y_s, ssm_s, conv_s = _group(x_s, 1, n_s, n_s, t_s, dec * n_s, False,
                                mod_s, state_ssm[l], state_conv[l], p)
    y_sample = y_s.reshape(dec, n_s, D_MODEL).transpose(1, 0, 2)

    return (y_prompt, y_sample, ssm_p[None], conv_p[None], ssm_s[None], conv_s[None])
```

```python
import functools

import jax
import jax.numpy as jnp
from jax import lax
from jax.experimental import pallas as pl
from jax.experimental.pallas import tpu as pltpu

F32 = jnp.float32
BF16 = jnp.bfloat16

D_MODEL = 4096
WIDTH = 2048
N_GROUPS = 128
GROUP = 16
N_STATE = 64
GROUPS_PER_BLOCK = 8
N_BLOCKS = N_GROUPS // GROUPS_PER_BLOCK
STATE_LANES = GROUPS_PER_BLOCK * N_STATE
LANES = 128
IN_COLS = 10 * WIDTH
EPS = 1e-6
VMEM_LIMIT = 56 * 1024 * 1024
NT_DIMS = (((1,), (1,)), ((), ()))


def _cparams(n_axes):
    return pltpu.CompilerParams(dimension_semantics=("arbitrary",) * n_axes,
                                vmem_limit_bytes=VMEM_LIMIT)


def _sigmoid(x):
    return jax.nn.sigmoid(x)


def _ada_kernel(c_ref, w_ref, b_ref, o_ref):
    c = c_ref[...]
    a = (c * _sigmoid(c)).astype(BF16)
    o_ref[...] = jnp.dot(a, w_ref[...].astype(BF16), preferred_element_type=F32) + b_ref[...]


def _ada(c_all, w_ada, b_ada, tn=512):
    rows = c_all.shape[0]
    n_out = w_ada.shape[1]
    return pl.pallas_call(
        _ada_kernel,
        grid=(n_out // tn,),
        in_specs=[pl.BlockSpec((rows, D_MODEL), lambda j: (0, 0)),
                  pl.BlockSpec((D_MODEL, tn), lambda j: (0, j)),
                  pl.BlockSpec((1, tn), lambda j: (0, j))],
        out_specs=pl.BlockSpec((rows, tn), lambda j: (0, j)),
        out_shape=jax.ShapeDtypeStruct((rows, n_out), F32),
        compiler_params=_cparams(1),
        name="ada_mod",
    )(c_all, w_ada, b_ada.reshape(1, n_out))


def _norm_kernel(x_ref, g_ref, sc_ref, sh_ref, o_ref, *, t_len, tiles_per_seq):
    if tiles_per_seq:
        n = pl.program_id(0) // tiles_per_seq
        sc = sc_ref[pl.ds(n, 1), :]
        sh = sh_ref[pl.ds(n, 1), :]
    else:
        sc = sc_ref[...]
        sh = sh_ref[...]
    for s in range(t_len):
        x = x_ref[:, s, :]
        y = x * lax.rsqrt(jnp.mean(x * x, axis=-1, keepdims=True) + EPS) * g_ref[...]
        o_ref[s] = (y * (1.0 + sc) + sh).astype(BF16)


def _norm_mod(x3, norm_g, scale, shift, ct, tiles_per_seq):
    m, t_len, _ = x3.shape
    q = scale.shape[0] if tiles_per_seq else ct
    mod_spec = pl.BlockSpec((q, D_MODEL), (lambda i: (0, 0)) if tiles_per_seq else (lambda i: (i, 0)))
    return pl.pallas_call(
        functools.partial(_norm_kernel, t_len=t_len, tiles_per_seq=tiles_per_seq),
        grid=(m // ct,),
        in_specs=[pl.BlockSpec((ct, t_len, D_MODEL), lambda i: (i, 0, 0)),
                  pl.BlockSpec((1, D_MODEL), lambda i: (0, 0)),
                  mod_spec, mod_spec],
        out_specs=pl.BlockSpec((t_len, ct, D_MODEL), lambda i: (0, i, 0)),
        out_shape=jax.ShapeDtypeStruct((t_len, m, D_MODEL), BF16),
        compiler_params=_cparams(1),
        name="norm_mod",
    )(x3, norm_g.reshape(1, D_MODEL), scale, shift)


def _inproj_kernel(x_ref, w_ref, o_ref, *, tn):
    acc = jnp.dot(x_ref[...], w_ref[...], preferred_element_type=F32)
    seg = pl.program_id(1) // (WIDTH // tn)
    is_silu = jnp.logical_or(seg == 1, seg == 5)
    is_sig = seg >= 6
    sig = _sigmoid(acc)
    o_ref[...] = jnp.where(is_sig, sig, jnp.where(is_silu, acc * sig, acc))


def _inproj(xn, w_in_bf, tm, tn=1024):
    rows = xn.shape[0]
    return pl.pallas_call(
        functools.partial(_inproj_kernel, tn=tn),
        grid=(rows // tm, IN_COLS // tn),
        in_specs=[pl.BlockSpec((tm, D_MODEL), lambda i, j: (i, 0)),
                  pl.BlockSpec((D_MODEL, tn), lambda i, j: (0, j))],
        out_specs=pl.BlockSpec((tm, tn), lambda i, j: (i, j)),
        out_shape=jax.ShapeDtypeStruct((rows, IN_COLS), F32),
        compiler_params=_cparams(2),
        name="inproj",
    )(xn, w_in_bf)


def _cmul(ar, ai, br, bi):
    return ar * br - ai * bi, ar * bi + ai * br


def _s5_kernel(u_ref, h0r_ref, h0i_ref, lr_ref, li_ref, ldt_ref, btr_ref, bti_ref, ctr_ref, cti_ref, d_ref,
               y_ref, hr_out, hi_out,
               ucat, bend_r, bend_i, cend_r, cend_i, krev, hl_r, hl_i, hp_r, hp_i, *, t_len, m, n_seq):
    blk = pl.program_id(0)

    @pl.when(blk == 0)
    def _():
        for ref in (bend_r, bend_i, cend_r, cend_i):
            ref[...] = jnp.zeros(ref.shape, ref.dtype)

    lam_r = lr_ref[0]
    lam_i = li_ref[0]
    dt = jnp.exp(ldt_ref[0])
    mag = jnp.exp(lam_r * dt)
    lb_r = mag * jnp.cos(lam_i * dt)
    lb_i = mag * jnp.sin(lam_i * dt)
    den = lam_r * lam_r + lam_i * lam_i
    nr = lb_r - 1.0
    co_r = (nr * lam_r + lb_i * lam_i) / den
    co_i = (lb_i * lam_r - nr * lam_i) / den
    bb_r, bb_i = _cmul(co_r, co_i, btr_ref[0], bti_ref[0])
    c_r = ctr_ref[0]
    c_i = cti_ref[0]
    pw = [(jnp.ones_like(lb_r), jnp.zeros_like(lb_r))]
    for _ in range(t_len):
        pw.append(_cmul(pw[-1][0], pw[-1][1], lb_r, lb_i))
    ar, ai = pw[t_len]

    lane = lax.broadcasted_iota(jnp.int32, (GROUP, LANES), 1)
    lo = lane < N_STATE

    def pair_tile(x, q):
        slab = x[:, q * LANES:(q + 1) * LANES]
        return jnp.concatenate([jnp.where(lo, slab, 0.0), jnp.where(lo, 0.0, slab)], axis=0).astype(BF16)

    grp_row = lax.broadcasted_iota(jnp.int32, (LANES, STATE_LANES), 0) // GROUP
    grp_lane = lax.broadcasted_iota(jnp.int32, (LANES, STATE_LANES), 1) // N_STATE
    same_group = grp_row == grp_lane

    def expand(x):
        return jnp.where(same_group, jnp.concatenate([x] * GROUPS_PER_BLOCK, axis=0), 0.0).astype(BF16)

    bbx_r = expand(bb_r)
    bbx_i = expand(bb_i)
    for s in range(t_len):
        ucat[:, s * LANES:(s + 1) * LANES] = u_ref[s].astype(BF16)
        be_r, be_i = _cmul(pw[t_len - 1 - s][0], pw[t_len - 1 - s][1], bb_r, bb_i)
        ce_r, ce_i = _cmul(pw[s + 1][0], pw[s + 1][1], c_r, c_i)
        for q in range(4):
            r0 = s * LANES + q * 2 * GROUP
            rows = slice(r0, r0 + 2 * GROUP)
            cols = slice(q * LANES, (q + 1) * LANES)
            bend_r[rows, cols] = pair_tile(be_r, q)
            bend_i[rows, cols] = pair_tile(be_i, q)
            cend_r[rows, cols] = pair_tile(ce_r, q)
            cend_i[rows, cols] = pair_tile(-ce_i, q)
        cl_r, cl_i = _cmul(pw[s][0], pw[s][1], c_r, c_i)
        k_lag = (lax.dot_general(bbx_r, expand(cl_r), NT_DIMS, preferred_element_type=F32)
                 - lax.dot_general(bbx_i, expand(cl_i), NT_DIMS, preferred_element_type=F32))
        krev[(t_len - 1 - s) * LANES:(t_len - s) * LANES, :] = k_lag.astype(BF16)

    uc = ucat[...]
    hl_r[...] = jnp.dot(uc, bend_r[...], preferred_element_type=F32)
    hl_i[...] = jnp.dot(uc, bend_i[...], preferred_element_type=F32)

    cps = m // n_seq
    if cps == 1:
        h_r = h0r_ref[0]
        h_i = h0i_ref[0]
        hp_r[...] = h_r
        hp_i[...] = h_i
        hr_out[0] = ar * h_r - ai * h_i + hl_r[...]
        hi_out[0] = ar * h_i + ai * h_r + hl_i[...]
    else:
        def body(c, carry):
            new = []
            for n in range(n_seq):
                h_r, h_i = carry[n]
                row = pl.ds(n * cps + c, 1)
                hp_r[row, :] = h_r
                hp_i[row, :] = h_i
                new.append((ar * h_r - ai * h_i + hl_r[row, :], ar * h_i + ai * h_r + hl_i[row, :]))
            return tuple(new)

        init = tuple((h0r_ref[0, n:n + 1, :], h0i_ref[0, n:n + 1, :]) for n in range(n_seq))
        fin = lax.fori_loop(0, cps, body, init)
        for n in range(n_seq):
            hr_out[0, n:n + 1, :] = fin[n][0]
            hi_out[0, n:n + 1, :] = fin[n][1]

    ycar = (lax.dot_general(hp_r[...].astype(BF16), cend_r[...], NT_DIMS, preferred_element_type=F32)
            + lax.dot_general(hp_i[...].astype(BF16), cend_i[...], NT_DIMS, preferred_element_type=F32))
    dvec = d_ref[0]
    for t in range(t_len):
        yt = jnp.dot(ucat[:, :(t + 1) * LANES], krev[(t_len - 1 - t) * LANES:, :],
                     preferred_element_type=F32)
        y_ref[t] = yt + ycar[:, t * LANES:(t + 1) * LANES] + dvec * u_ref[t]


def _s5_params(lam_re, lam_im, log_dt, b_re, b_im, c_re, c_im, d_skip):
    row = lambda x: x.astype(F32).reshape(N_BLOCKS, 1, STATE_LANES)
    ldt = jnp.broadcast_to(log_dt.astype(F32)[:, None], (N_GROUPS, N_STATE))
    bt = lambda x: (x.astype(F32).reshape(N_BLOCKS, GROUPS_PER_BLOCK, N_STATE, GROUP)
                    .transpose(0, 3, 1, 2).reshape(N_BLOCKS, GROUP, STATE_LANES))
    ct = lambda x: (x.astype(F32).reshape(N_BLOCKS, GROUPS_PER_BLOCK, GROUP, N_STATE)
                    .transpose(0, 2, 1, 3).reshape(N_BLOCKS, GROUP, STATE_LANES))
    return (row(lam_re), row(lam_im), row(ldt), bt(b_re), bt(b_im), ct(c_re), ct(c_im),
            d_skip.astype(F32).reshape(N_BLOCKS, 1, LANES))


def _s5(proj3, h0_re, h0_im, s5p, t_len, m, n_seq):
    kern = functools.partial(_s5_kernel, t_len=t_len, m=m, n_seq=n_seq)
    blk3 = lambda b: (b, 0, 0)
    row_spec = pl.BlockSpec((1, 1, STATE_LANES), blk3)
    mat_spec = pl.BlockSpec((1, GROUP, STATE_LANES), blk3)
    state_spec = pl.BlockSpec((1, n_seq, STATE_LANES), blk3)
    end_mat = pltpu.VMEM((t_len * LANES, STATE_LANES), BF16)
    chunk_state = pltpu.VMEM((m, STATE_LANES), F32)
    return pl.pallas_call(
        kern,
        grid=(N_BLOCKS,),
        in_specs=[pl.BlockSpec((t_len, m, LANES), lambda b: (0, 0, b)),
                  state_spec, state_spec,
                  row_spec, row_spec, row_spec,
                  mat_spec, mat_spec, mat_spec, mat_spec,
                  pl.BlockSpec((1, 1, LANES), blk3)],
        out_specs=[pl.BlockSpec((t_len, m, LANES), lambda b: (0, 0, b)), state_spec, state_spec],
        out_shape=[jax.ShapeDtypeStruct((t_len, m, WIDTH), F32),
                   jax.ShapeDtypeStruct((N_BLOCKS, n_seq, STATE_LANES), F32),
                   jax.ShapeDtypeStruct((N_BLOCKS, n_seq, STATE_LANES), F32)],
        scratch_shapes=[pltpu.VMEM((m, t_len * LANES), BF16),
                        end_mat, end_mat, end_mat, end_mat,
                        pltpu.VMEM((t_len * LANES, LANES), BF16),
                        chunk_state, chunk_state, chunk_state, chunk_state],
        compiler_params=_cparams(1),
        name="s5_scan",
    )(proj3, h0_re, h0_im, *s5p)


def _conv_kernel(hb0, cb0, hb1, cb1, hb2, cb2, bb, szb, buf0, buf1, w_ref, bin_ref, v_ref, *, m, n_seq):
    s = pl.program_id(0)
    cps = m // n_seq
    v0 = hb0[...] * cb0[...]
    v1 = hb1[...] * cb1[...]
    v2 = hb2[...] * cb2[...]

    def prev_chunk(v, first_rows):
        if cps == 1:
            return first_rows
        out = pltpu.roll(v, 1, 0)
        rid = lax.broadcasted_iota(jnp.int32, v.shape, 0)
        for n in range(n_seq):
            out = jnp.where(rid == n * cps, first_rows[n:n + 1, :], out)
        return out

    b0 = buf0[...]
    b1 = buf1[...]
    vm1 = jnp.where(s >= 1, v1, prev_chunk(v1, b1))
    vm2 = jnp.where(s >= 2, v2, prev_chunk(v2, jnp.where(s == 0, b0, b1)))
    conv = vm2 * w_ref[0:1, :] + vm1 * w_ref[1:2, :] + v0 * w_ref[2:3, :]
    bin_ref[...] = (bb[...] * conv * szb[...]).astype(BF16)
    v_ref[...] = v0


def _conv_branch(proj, buf0, buf1, conv_w, t_len, m, n_seq, tn=256):
    nb = WIDTH // tn
    kern = functools.partial(_conv_kernel, m=m, n_seq=n_seq)

    def plane(seg, back):
        return pl.BlockSpec((m, tn), lambda s, j: ((s + t_len - back) % t_len, seg * nb + j))

    return pl.pallas_call(
        kern,
        grid=(t_len, nb),
        in_specs=[plane(2, 0), plane(3, 0), plane(2, 1), plane(3, 1), plane(2, 2), plane(3, 2),
                  plane(4, 0), plane(5, 0),
                  pl.BlockSpec((n_seq, tn), lambda s, j: (0, j)),
                  pl.BlockSpec((n_seq, tn), lambda s, j: (0, j)),
                  pl.BlockSpec((3, tn), lambda s, j: (0, j))],
        out_specs=[pl.BlockSpec((m, tn), lambda s, j: (s, j)),
                   pl.BlockSpec((m, tn), lambda s, j: (s, j))],
        out_shape=[jax.ShapeDtypeStruct((t_len * m, WIDTH), BF16),
                   jax.ShapeDtypeStruct((t_len * m, WIDTH), F32)],
        compiler_params=_cparams(2),
        name="conv_branch",
    )(proj, proj, proj, proj, proj, proj, proj, proj, buf0, buf1, conv_w)


def _glu_kernel(y_ref, sza_ref, w_ref, b_ref, o_ref, ya32, ya16, *, tn):
    j = pl.program_id(1)

    @pl.when(j == 0)
    def _():
        ya = jax.nn.gelu(y_ref[...])
        ya32[...] = ya
        ya16[...] = ya.astype(BF16)

    glu = jnp.dot(ya16[...], w_ref[...], preferred_element_type=F32) + b_ref[...]
    ya = ya32[:, pl.ds(pl.multiple_of(j * tn, tn), tn)]
    o_ref[...] = (ya * _sigmoid(glu) * sza_ref[...]).astype(BF16)


def _glu(y, proj, w_glu_bf, b_glu, tm, tn=512):
    rows = y.shape[0]
    nb = WIDTH // tn
    return pl.pallas_call(
        functools.partial(_glu_kernel, tn=tn),
        grid=(rows // tm, nb),
        in_specs=[pl.BlockSpec((tm, WIDTH), lambda i, j: (i, 0)),
                  pl.BlockSpec((tm, tn), lambda i, j: (i, nb + j)),
                  pl.BlockSpec((WIDTH, tn), lambda i, j: (0, j)),
                  pl.BlockSpec((1, tn), lambda i, j: (0, j))],
        out_specs=pl.BlockSpec((tm, tn), lambda i, j: (i, j)),
        out_shape=jax.ShapeDtypeStruct((rows, WIDTH), BF16),
        scratch_shapes=[pltpu.VMEM((tm, WIDTH), F32), pltpu.VMEM((tm, WIDTH), BF16)],
        compiler_params=_cparams(2),
        name="glu_gate",
    )(y, proj, w_glu_bf, b_glu.reshape(1, WIDTH))


def _merge_kernel(a_ref, b_ref, wa_ref, wb_ref, sga_ref, sgb_ref, o_ref):
    pa = jnp.dot(a_ref[...], wa_ref[...], preferred_element_type=F32)
    pb = jnp.dot(b_ref[...], wb_ref[...], preferred_element_type=F32)
    o_ref[...] = (sga_ref[...] * pa + sgb_ref[...] * pb).astype(BF16)


def _merge(a_in, b_in, proj, w_pa_bf, w_pb_bf, tm, tn=512):
    rows = a_in.shape[0]
    nb = WIDTH // tn
    w_spec = pl.BlockSpec((WIDTH, tn), lambda i, j: (0, j))
    return pl.pallas_call(
        _merge_kernel,
        grid=(rows // tm, D_MODEL // tn),
        in_specs=[pl.BlockSpec((tm, WIDTH), lambda i, j: (i, 0)),
                  pl.BlockSpec((tm, WIDTH), lambda i, j: (i, 0)),
                  w_spec, w_spec,
                  pl.BlockSpec((tm, tn), lambda i, j: (i, 6 * nb + j)),
                  pl.BlockSpec((tm, tn), lambda i, j: (i, 8 * nb + j))],
        out_specs=pl.BlockSpec((tm, tn), lambda i, j: (i, j)),
        out_shape=jax.ShapeDtypeStruct((rows, D_MODEL), BF16),
        compiler_params=_cparams(2),
        name="merge",
    )(a_in, b_in, w_pa_bf, w_pb_bf, proj, proj)


def _outproj_kernel(m_ref, w_ref, x_ref, gate_ref, fg_ref, o_ref, hres, *, t_len, ct, tn, tiles_per_seq):
    j = pl.program_id(1)
    part = jnp.dot(m_ref[...].reshape(t_len * ct, D_MODEL), w_ref[...], preferred_element_type=F32)
    if tiles_per_seq:
        gate = gate_ref[pl.ds(pl.program_id(0) // tiles_per_seq, 1), :]
    else:
        gate = gate_ref[...]
    col = pl.ds(pl.multiple_of(j * tn, tn), tn)
    for s in range(t_len):
        hres[s, :, col] = x_ref[:, s, :] + gate * part[s * ct:(s + 1) * ct, :]

    @pl.when(j == pl.num_programs(1) - 1)
    def _():
        for s in range(t_len):
            h = hres[s]
            o_ref[:, s, :] = h * lax.rsqrt(jnp.mean(h * h, axis=-1, keepdims=True) + EPS) * fg_ref[...]


def _outproj(merged3, w_o_bf, x3, gate, final_g, ct, tiles_per_seq, tn=512):
    t_len, m, _ = merged3.shape
    q = gate.shape[0] if tiles_per_seq else ct
    gate_spec = pl.BlockSpec((q, tn), (lambda i, j: (0, j)) if tiles_per_seq else (lambda i, j: (i, j)))
    kern = functools.partial(_outproj_kernel, t_len=t_len, ct=ct, tn=tn, tiles_per_seq=tiles_per_seq)
    return pl.pallas_call(
        kern,
        grid=(m // ct, D_MODEL // tn),
        in_specs=[pl.BlockSpec((t_len, ct, D_MODEL), lambda i, j: (0, i, 0)),
                  pl.BlockSpec((D_MODEL, tn), lambda i, j: (0, j)),
                  pl.BlockSpec((ct, t_len, tn), lambda i, j: (i, 0, j)),
                  gate_spec,
                  pl.BlockSpec((1, D_MODEL), lambda i, j: (0, 0))],
        out_specs=pl.BlockSpec((ct, t_len, D_MODEL), lambda i, j: (i, 0, 0)),
        out_shape=jax.ShapeDtypeStruct((m, t_len, D_MODEL), F32),
        scratch_shapes=[pltpu.VMEM((t_len, ct, D_MODEL), F32)],
        compiler_params=_cparams(2),
        name="outproj",
    )(merged3, w_o_bf, x3, gate, final_g.reshape(1, D_MODEL))


def _state_to_blocks(state):
    n = state.shape[0]
    re = state[..., 0].reshape(n, N_BLOCKS, STATE_LANES).transpose(1, 0, 2)
    im = state[..., 1].reshape(n, N_BLOCKS, STATE_LANES).transpose(1, 0, 2)
    return re, im


def _blocks_to_state(re, im):
    n = re.shape[1]
    re = re.transpose(1, 0, 2).reshape(n, N_GROUPS, N_STATE)
    im = im.transpose(1, 0, 2).reshape(n, N_GROUPS, N_STATE)
    return jnp.stack([re, im], axis=-1)


def _group(x3, n_seq, ct, tm, mod, ssm0, conv0, p):
    m, t_len, _ = x3.shape
    cps = m // n_seq
    tiles_per_seq = cps // ct if cps > 1 else 0
    shift, scale, gate = mod
    xn = _norm_mod(x3, p["norm_g"], scale, shift, ct, tiles_per_seq)
    proj = _inproj(xn.reshape(t_len * m, D_MODEL), p["w_in"], tm)
    h0_re, h0_im = _state_to_blocks(ssm0)
    y3, hr, hi = _s5(proj.reshape(t_len, m, IN_COLS), h0_re, h0_im, p["s5"], t_len, m, n_seq)
    b_in, v = _conv_branch(proj, conv0[:, 0], conv0[:, 1], p["conv_w"], t_len, m, n_seq)
    a_in = _glu(y3.reshape(t_len * m, WIDTH), proj, p["w_glu"], p["b_glu"], 512)
    merged = _merge(a_in, b_in, proj, p["w_pa"], p["w_pb"], 512)
    out3 = _outproj(merged.reshape(t_len, m, D_MODEL), p["w_o"], x3, gate, p["final_g"], ct, tiles_per_seq)
    v3 = v.reshape(t_len, m, WIDTH)
    conv_new = v3[t_len - 2:, cps - 1::cps, :].transpose(1, 0, 2)
    return out3, _blocks_to_state(hr, hi), conv_new


def kernel(x_prompt, x_sample, state_ssm, state_conv, c_prompt, c_sample, norm_g, w_ada, b_ada, w_in, lam_re, lam_im, log_dt, b_re, b_im, c_re, c_im, d_skip, w_glu, b_glu, w_pa, conv_w, w_pb, w_o, final_g):
    depth = norm_g.shape[0]
    assert depth == 1
    n_p, seq, _ = x_prompt.shape
    n_s, dec, _ = x_sample.shape
    t_p = 16
    assert seq % t_p == 0

    l = 0
    p = dict(norm_g=norm_g[l], w_in=w_in[l].astype(BF16), conv_w=conv_w[l],
             w_glu=w_glu[l].astype(BF16), b_glu=b_glu[l], w_pa=w_pa[l].astype(BF16),
             w_pb=w_pb[l].astype(BF16), w_o=w_o[l].astype(BF16), final_g=final_g,
             s5=_s5_params(lam_re[l], lam_im[l], log_dt[l], b_re[l], b_im[l], c_re[l], c_im[l], d_skip[l]))

    n_c = n_p + n_s
    pad = (-n_c) % 16
    c_all = jnp.concatenate([c_prompt, c_sample, jnp.zeros((pad, D_MODEL), F32)], axis=0)
    mod = _ada(c_all, w_ada[l], b_ada[l])
    mod_p = tuple(mod[:n_p, k * D_MODEL:(k + 1) * D_MODEL] for k in range(3))
    mod_s = tuple(mod[n_p:n_c, k * D_MODEL:(k + 1) * D_MODEL] for k in range(3))

    ssm_p0 = jnp.zeros((n_p, N_GROUPS, N_STATE, 2), F32)
    conv_p0 = jnp.zeros((n_p, 2, WIDTH), F32)
    y_p, ssm_p, conv_p = _group(x_prompt.reshape(n_p * seq // t_p, t_p, D_MODEL), n_p, 32, 1024,
                                mod_p, ssm_p0, conv_p0, p)
    y_prompt = y_p.reshape(n_p, seq, D_MODEL)

    y_sample, ssm_s, conv_s = _group(x_sample, n_s, 64, dec * n_s, mod_s, state_ssm[l], state_conv[l], p)

    return (y_prompt, y_sample, ssm_p[None], conv_p[None], ssm_s[None], conv_s[None])
```

```python
import functools

import jax
import jax.numpy as jnp
from jax import lax
from jax.experimental import pallas as pl
from jax.experimental.pallas import tpu as pltpu

F32 = jnp.float32
BF16 = jnp.bfloat16

D_MODEL = 4096
WIDTH = 2048
N_GROUPS = 128
GROUP = 16
N_STATE = 64
GROUPS_PER_BLOCK = 8
N_BLOCKS = N_GROUPS // GROUPS_PER_BLOCK
STATE_LANES = GROUPS_PER_BLOCK * N_STATE
LANES = 128
IN_COLS = 10 * WIDTH
EPS = 1e-6
VMEM_LIMIT = 56 * 1024 * 1024
NT_DIMS = (((1,), (1,)), ((), ()))


def _cparams(n_axes):
    return pltpu.CompilerParams(dimension_semantics=("arbitrary",) * n_axes,
                                vmem_limit_bytes=VMEM_LIMIT)


def _sigmoid(x):
    return jax.nn.sigmoid(x)


def _ada_kernel(c_ref, w_ref, b_ref, o_ref):
    c = c_ref[...]
    a = (c * _sigmoid(c)).astype(BF16)
    o_ref[...] = jnp.dot(a, w_ref[...].astype(BF16), preferred_element_type=F32) + b_ref[...]


def _ada(c_all, w_ada, b_ada, tn=512):
    rows = c_all.shape[0]
    n_out = w_ada.shape[1]
    return pl.pallas_call(
        _ada_kernel,
        grid=(n_out // tn,),
        in_specs=[pl.BlockSpec((rows, D_MODEL), lambda j: (0, 0)),
                  pl.BlockSpec((D_MODEL, tn), lambda j: (0, j)),
                  pl.BlockSpec((1, tn), lambda j: (0, j))],
        out_specs=pl.BlockSpec((rows, tn), lambda j: (0, j)),
        out_shape=jax.ShapeDtypeStruct((rows, n_out), F32),
        compiler_params=_cparams(1),
        name="ada_mod",
    )(c_all, w_ada, b_ada.reshape(1, n_out))


def _norm_kernel(x_ref, g_ref, sc_ref, sh_ref, o_ref, *, t_len, ct, tiles_per_seq):
    if tiles_per_seq:
        n = pl.program_id(0) // tiles_per_seq
        sc = sc_ref[pl.ds(n, 1), :]
        sh = sh_ref[pl.ds(n, 1), :]
    else:
        sc = sc_ref[...]
        sh = sh_ref[...]
    x = x_ref[...]
    y = x * lax.rsqrt(jnp.mean(x * x, axis=-1, keepdims=True) + EPS) * g_ref[...]
    y = (y * (1.0 + sc) + sh).astype(BF16)
    r = t_len * ct
    r_out = lax.broadcasted_iota(jnp.int32, (r, r), 0)
    r_in = lax.broadcasted_iota(jnp.int32, (r, r), 1)
    src_row = (r_out & (ct - 1)) * t_len + lax.shift_right_logical(r_out, ct.bit_length() - 1)
    perm = jnp.where(r_in == src_row, 1.0, 0.0).astype(BF16)
    yp = jnp.dot(perm, y, preferred_element_type=F32)
    o_ref[...] = yp.reshape(t_len, ct, D_MODEL).astype(BF16)


def _norm_mod(x2d, t_len, norm_g, scale, shift, ct, tiles_per_seq):
    assert ct & (ct - 1) == 0
    r = t_len * ct
    m = x2d.shape[0] // t_len
    if tiles_per_seq:
        mod_spec = pl.BlockSpec((scale.shape[0], D_MODEL), lambda i: (0, 0))
    else:
        mod_spec = pl.BlockSpec((r, D_MODEL), lambda i: (i, 0))
    return pl.pallas_call(
        functools.partial(_norm_kernel, t_len=t_len, ct=ct, tiles_per_seq=tiles_per_seq),
        grid=(m // ct,),
        in_specs=[pl.BlockSpec((r, D_MODEL), lambda i: (i, 0)),
                  pl.BlockSpec((1, D_MODEL), lambda i: (0, 0)),
                  mod_spec, mod_spec],
        out_specs=pl.BlockSpec((t_len, ct, D_MODEL), lambda i: (0, i, 0)),
        out_shape=jax.ShapeDtypeStruct((t_len, m, D_MODEL), BF16),
        compiler_params=_cparams(1),
        name="norm_mod",
    )(x2d, norm_g.reshape(1, D_MODEL), scale, shift)


def _inproj_kernel(x_ref, w_ref, o_ref, *, tn):
    acc = jnp.dot(x_ref[...], w_ref[...], preferred_element_type=F32)
    seg = pl.program_id(1) // (WIDTH // tn)
    is_silu = jnp.logical_or(seg == 1, seg == 5)
    is_sig = seg >= 6
    sig = _sigmoid(acc)
    o_ref[...] = jnp.where(is_sig, sig, jnp.where(is_silu, acc * sig, acc)).astype(BF16)


def _inproj(xn, w_in_bf, tm, tn=1024):
    rows = xn.shape[0]
    return pl.pallas_call(
        functools.partial(_inproj_kernel, tn=tn),
        grid=(rows // tm, IN_COLS // tn),
        in_specs=[pl.BlockSpec((tm, D_MODEL), lambda i, j: (i, 0)),
                  pl.BlockSpec((D_MODEL, tn), lambda i, j: (0, j))],
        out_specs=pl.BlockSpec((tm, tn), lambda i, j: (i, j)),
        out_shape=jax.ShapeDtypeStruct((rows, IN_COLS), BF16),
        compiler_params=_cparams(2),
        name="inproj",
    )(xn, w_in_bf)


def _cmul(ar, ai, br, bi):
    return ar * br - ai * bi, ar * bi + ai * br


def _s5_kernel(u_ref, h0r_ref, h0i_ref, lr_ref, li_ref, ldt_ref, btr_ref, bti_ref, ctr_ref, cti_ref, d_ref,
               y_ref, hr_out, hi_out,
               ucat, bend_r, bend_i, cend_r, cend_i, krev, hl_r, hl_i, hp_r, hp_i, *, t_len, m, n_seq):
    blk = pl.program_id(0)

    @pl.when(blk == 0)
    def _():
        for ref in (bend_r, bend_i, cend_r, cend_i):
            ref[...] = jnp.zeros(ref.shape, ref.dtype)

    lam_r = lr_ref[0]
    lam_i = li_ref[0]
    dt = jnp.exp(ldt_ref[0])
    mag = jnp.exp(lam_r * dt)
    lb_r = mag * jnp.cos(lam_i * dt)
    lb_i = mag * jnp.sin(lam_i * dt)
    den = lam_r * lam_r + lam_i * lam_i
    nr = lb_r - 1.0
    co_r = (nr * lam_r + lb_i * lam_i) / den
    co_i = (lb_i * lam_r - nr * lam_i) / den
    bb_r, bb_i = _cmul(co_r, co_i, btr_ref[0], bti_ref[0])
    c_r = ctr_ref[0]
    c_i = cti_ref[0]
    pw = [(jnp.ones_like(lb_r), jnp.zeros_like(lb_r))]
    for _ in range(t_len):
        pw.append(_cmul(pw[-1][0], pw[-1][1], lb_r, lb_i))
    ar, ai = pw[t_len]

    lane = lax.broadcasted_iota(jnp.int32, (GROUP, LANES), 1)
    lo = lane < N_STATE

    def pair_tile(x, q):
        slab = x[:, q * LANES:(q + 1) * LANES]
        return jnp.concatenate([jnp.where(lo, slab, 0.0), jnp.where(lo, 0.0, slab)], axis=0).astype(BF16)

    grp_row = lax.broadcasted_iota(jnp.int32, (LANES, STATE_LANES), 0) // GROUP
    grp_lane = lax.broadcasted_iota(jnp.int32, (LANES, STATE_LANES), 1) // N_STATE
    same_group = grp_row == grp_lane

    def expand(x):
        return jnp.where(same_group, jnp.concatenate([x] * GROUPS_PER_BLOCK, axis=0), 0.0).astype(BF16)

    bbx_r = expand(bb_r)
    bbx_i = expand(bb_i)
    for s in range(t_len):
        ucat[:, s * LANES:(s + 1) * LANES] = u_ref[s]
        be_r, be_i = _cmul(pw[t_len - 1 - s][0], pw[t_len - 1 - s][1], bb_r, bb_i)
        ce_r, ce_i = _cmul(pw[s + 1][0], pw[s + 1][1], c_r, c_i)
        for q in range(4):
            r0 = s * LANES + q * 2 * GROUP
            rows = slice(r0, r0 + 2 * GROUP)
            cols = slice(q * LANES, (q + 1) * LANES)
            bend_r[rows, cols] = pair_tile(be_r, q)
            bend_i[rows, cols] = pair_tile(be_i, q)
            cend_r[rows, cols] = pair_tile(ce_r, q)
            cend_i[rows, cols] = pair_tile(-ce_i, q)
        cl_r, cl_i = _cmul(pw[s][0], pw[s][1], c_r, c_i)
        k_lag = (lax.dot_general(bbx_r, expand(cl_r), NT_DIMS, preferred_element_type=F32)
                 - lax.dot_general(bbx_i, expand(cl_i), NT_DIMS, preferred_element_type=F32))
        krev[(t_len - 1 - s) * LANES:(t_len - s) * LANES, :] = k_lag.astype(BF16)

    uc = ucat[...]
    hl_r[...] = jnp.dot(uc, bend_r[...], preferred_element_type=F32)
    hl_i[...] = jnp.dot(uc, bend_i[...], preferred_element_type=F32)

    cps = m // n_seq
    if cps == 1:
        h_r = h0r_ref[0]
        h_i = h0i_ref[0]
        hp_r[...] = h_r
        hp_i[...] = h_i
        hr_out[0] = ar * h_r - ai * h_i + hl_r[...]
        hi_out[0] = ar * h_i + ai * h_r + hl_i[...]
    else:
        def body(c, carry):
            new = []
            for n in range(n_seq):
                h_r, h_i = carry[n]
                row = pl.ds(n * cps + c, 1)
                hp_r[row, :] = h_r
                hp_i[row, :] = h_i
                new.append((ar * h_r - ai * h_i + hl_r[row, :], ar * h_i + ai * h_r + hl_i[row, :]))
            return tuple(new)

        init = tuple((h0r_ref[0, n:n + 1, :], h0i_ref[0, n:n + 1, :]) for n in range(n_seq))
        fin = lax.fori_loop(0, cps, body, init)
        for n in range(n_seq):
            hr_out[0, n:n + 1, :] = fin[n][0]
            hi_out[0, n:n + 1, :] = fin[n][1]

    ycar = (lax.dot_general(hp_r[...].astype(BF16), cend_r[...], NT_DIMS, preferred_element_type=F32)
            + lax.dot_general(hp_i[...].astype(BF16), cend_i[...], NT_DIMS, preferred_element_type=F32))
    dvec = d_ref[0]
    for t in range(t_len):
        yt = jnp.dot(ucat[:, :(t + 1) * LANES], krev[(t_len - 1 - t) * LANES:, :],
                     preferred_element_type=F32)
        y = yt + ycar[:, t * LANES:(t + 1) * LANES] + dvec * u_ref[t].astype(F32)
        y_ref[t] = jax.nn.gelu(y).astype(BF16)


def _s5_params(lam_re, lam_im, log_dt, b_re, b_im, c_re, c_im, d_skip):
    row = lambda x: x.astype(F32).reshape(N_BLOCKS, 1, STATE_LANES)
    ldt = jnp.broadcast_to(log_dt.astype(F32)[:, None], (N_GROUPS, N_STATE))
    bt = lambda x: (x.astype(F32).reshape(N_BLOCKS, GROUPS_PER_BLOCK, N_STATE, GROUP)
                    .transpose(0, 3, 1, 2).reshape(N_BLOCKS, GROUP, STATE_LANES))
    ct = lambda x: (x.astype(F32).reshape(N_BLOCKS, GROUPS_PER_BLOCK, GROUP, N_STATE)
                    .transpose(0, 2, 1, 3).reshape(N_BLOCKS, GROUP, STATE_LANES))
    return (row(lam_re), row(lam_im), row(ldt), bt(b_re), bt(b_im), ct(c_re), ct(c_im),
            d_skip.astype(F32).reshape(N_BLOCKS, 1, LANES))


def _s5(proj3, h0_re, h0_im, s5p, t_len, m, n_seq):
    kern = functools.partial(_s5_kernel, t_len=t_len, m=m, n_seq=n_seq)
    blk3 = lambda b: (b, 0, 0)
    row_spec = pl.BlockSpec((1, 1, STATE_LANES), blk3)
    mat_spec = pl.BlockSpec((1, GROUP, STATE_LANES), blk3)
    state_spec = pl.BlockSpec((1, n_seq, STATE_LANES), blk3)
    end_mat = pltpu.VMEM((t_len * LANES, STATE_LANES), BF16)
    chunk_state = pltpu.VMEM((m, STATE_LANES), F32)
    return pl.pallas_call(
        kern,
        grid=(N_BLOCKS,),
        in_specs=[pl.BlockSpec((t_len, m, LANES), lambda b: (0, 0, b)),
                  state_spec, state_spec,
                  row_spec, row_spec, row_spec,
                  mat_spec, mat_spec, mat_spec, mat_spec,
                  pl.BlockSpec((1, 1, LANES), blk3)],
        out_specs=[pl.BlockSpec((t_len, m, LANES), lambda b: (0, 0, b)), state_spec, state_spec],
        out_shape=[jax.ShapeDtypeStruct((t_len, m, WIDTH), BF16),
                   jax.ShapeDtypeStruct((N_BLOCKS, n_seq, STATE_LANES), F32),
                   jax.ShapeDtypeStruct((N_BLOCKS, n_seq, STATE_LANES), F32)],
        scratch_shapes=[pltpu.VMEM((m, t_len * LANES), BF16),
                        end_mat, end_mat, end_mat, end_mat,
                        pltpu.VMEM((t_len * LANES, LANES), BF16),
                        chunk_state, chunk_state, chunk_state, chunk_state],
        compiler_params=_cparams(1),
        name="s5_scan",
    )(proj3, h0_re, h0_im, *s5p)


def _conv_kernel(hb0, cb0, hb1, cb1, hb2, cb2, bb, szb, buf0, buf1, w_ref, bin_ref, v_ref, *, m, n_seq):
    s = pl.program_id(1)
    cps = m // n_seq
    v0 = hb0[...].astype(F32) * cb0[...].astype(F32)
    v1 = hb1[...].astype(F32) * cb1[...].astype(F32)
    v2 = hb2[...].astype(F32) * cb2[...].astype(F32)

    def prev_chunk(v, first_rows):
        if cps == 1:
            return first_rows
        out = pltpu.roll(v, 1, 0)
        rid = lax.broadcasted_iota(jnp.int32, v.shape, 0)
        for n in range(n_seq):
            out = jnp.where(rid == n * cps, first_rows[n:n + 1, :], out)
        return out

    b0 = buf0[...]
    b1 = buf1[...]
    vm1 = jnp.where(s >= 1, v1, prev_chunk(v1, b1))
    vm2 = jnp.where(s >= 2, v2, prev_chunk(v2, jnp.where(s == 0, b0, b1)))
    conv = vm2 * w_ref[0:1, :] + vm1 * w_ref[1:2, :] + v0 * w_ref[2:3, :]
    bin_ref[...] = (bb[...].astype(F32) * conv * szb[...].astype(F32)).astype(BF16)
    v_ref[0] = v0


def _conv_branch(proj, buf0, buf1, conv_w, t_len, m, n_seq, tn=512):
    nb = WIDTH // tn
    kern = functools.partial(_conv_kernel, m=m, n_seq=n_seq)

    def plane(seg, back):
        return pl.BlockSpec((m, tn), lambda j, s: ((s + t_len - back) % t_len, seg * nb + j))

    return pl.pallas_call(
        kern,
        grid=(nb, t_len),
        in_specs=[plane(2, 0), plane(3, 0), plane(2, 1), plane(3, 1), plane(2, 2), plane(3, 2),
                  plane(4, 0), plane(5, 0),
                  pl.BlockSpec((n_seq, tn), lambda j, s: (0, j)),
                  pl.BlockSpec((n_seq, tn), lambda j, s: (0, j)),
                  pl.BlockSpec((3, tn), lambda j, s: (0, j))],
        out_specs=[pl.BlockSpec((m, tn), lambda j, s: (s, j)),
                   pl.BlockSpec((1, m, tn), lambda j, s: (jnp.maximum(s - (t_len - 2), 0), 0, j))],
        out_shape=[jax.ShapeDtypeStruct((t_len * m, WIDTH), BF16),
                   jax.ShapeDtypeStruct((2, m, WIDTH), F32)],
        compiler_params=_cparams(2),
        name="conv_branch",
    )(proj, proj, proj, proj, proj, proj, proj, proj, buf0, buf1, conv_w)


def _glu_kernel(ya_ref, sza_ref, w_ref, b_ref, o_ref):
    ya = ya_ref[...]
    glu = jnp.dot(ya, w_ref[...], preferred_element_type=F32) + b_ref[...]
    o_ref[...] = (ya.astype(F32) * _sigmoid(glu) * sza_ref[...].astype(F32)).astype(BF16)


def _glu(ya, proj, w_glu_bf, b_glu, tm):
    rows = ya.shape[0]
    return pl.pallas_call(
        _glu_kernel,
        grid=(rows // tm,),
        in_specs=[pl.BlockSpec((tm, WIDTH), lambda i: (i, 0)),
                  pl.BlockSpec((tm, WIDTH), lambda i: (i, 1)),
                  pl.BlockSpec((WIDTH, WIDTH), lambda i: (0, 0)),
                  pl.BlockSpec((1, WIDTH), lambda i: (0, 0))],
        out_specs=pl.BlockSpec((tm, WIDTH), lambda i: (i, 0)),
        out_shape=jax.ShapeDtypeStruct((rows, WIDTH), BF16),
        compiler_params=_cparams(1),
        name="glu_gate",
    )(ya, proj, w_glu_bf, b_glu.reshape(1, WIDTH))


def _merge_kernel(a_ref, b_ref, wa_ref, wb_ref, sga_ref, sgb_ref, o_ref):
    pa = jnp.dot(a_ref[...], wa_ref[...], preferred_element_type=F32)
    pb = jnp.dot(b_ref[...], wb_ref[...], preferred_element_type=F32)
    o_ref[...] = (sga_ref[...] * pa + sgb_ref[...] * pb).astype(BF16)


def _merge(a_in, b_in, proj, w_pa_bf, w_pb_bf, tm, tn=512):
    rows = a_in.shape[0]
    nb = WIDTH // tn
    w_spec = pl.BlockSpec((WIDTH, tn), lambda i, j: (0, j))
    return pl.pallas_call(
        _merge_kernel,
        grid=(rows // tm, D_MODEL // tn),
        in_specs=[pl.BlockSpec((tm, WIDTH), lambda i, j: (i, 0)),
                  pl.BlockSpec((tm, WIDTH), lambda i, j: (i, 0)),
                  w_spec, w_spec,
                  pl.BlockSpec((tm, tn), lambda i, j: (i, 6 * nb + j)),
                  pl.BlockSpec((tm, tn), lambda i, j: (i, 8 * nb + j))],
        out_specs=pl.BlockSpec((tm, tn), lambda i, j: (i, j)),
        out_shape=jax.ShapeDtypeStruct((rows, D_MODEL), BF16),
        compiler_params=_cparams(2),
        name="merge",
    )(a_in, b_in, w_pa_bf, w_pb_bf, proj, proj)


def _outproj_kernel(m_ref, w_ref, x_ref, gate_ref, fg_ref, o_ref, hres, *, t_len, ct, tn, tiles_per_seq):
    j = pl.program_id(1)
    part = jnp.dot(m_ref[...].reshape(t_len * ct, D_MODEL), w_ref[...], preferred_element_type=F32)
    if tiles_per_seq:
        gate = gate_ref[pl.ds(pl.program_id(0) // tiles_per_seq, 1), :]
    else:
        gate = gate_ref[...]
    col = pl.ds(pl.multiple_of(j * tn, tn), tn)
    for s in range(t_len):
        hres[s, :, col] = x_ref[:, s, :] + gate * part[s * ct:(s + 1) * ct, :]

    @pl.when(j == pl.num_programs(1) - 1)
    def _():
        for s in range(t_len):
            h = hres[s]
            o_ref[:, s, :] = h * lax.rsqrt(jnp.mean(h * h, axis=-1, keepdims=True) + EPS) * fg_ref[...]


def _outproj(merged3, w_o_bf, x3, gate, final_g, ct, tiles_per_seq, tn=512):
    t_len, m, _ = merged3.shape
    q = gate.shape[0] if tiles_per_seq else ct
    gate_spec = pl.BlockSpec((q, tn), (lambda i, j: (0, j)) if tiles_per_seq else (lambda i, j: (i, j)))
    kern = functools.partial(_outproj_kernel, t_len=t_len, ct=ct, tn=tn, tiles_per_seq=tiles_per_seq)
    return pl.pallas_call(
        kern,
        grid=(m // ct, D_MODEL // tn),
        in_specs=[pl.BlockSpec((t_len, ct, D_MODEL), lambda i, j: (0, i, 0)),
                  pl.BlockSpec((D_MODEL, tn), lambda i, j: (0, j)),
                  pl.BlockSpec((ct, t_len, tn), lambda i, j: (i, 0, j)),
                  gate_spec,
                  pl.BlockSpec((1, D_MODEL), lambda i, j: (0, 0))],
        out_specs=pl.BlockSpec((ct, t_len, D_MODEL), lambda i, j: (i, 0, 0)),
        out_shape=jax.ShapeDtypeStruct((m, t_len, D_MODEL), F32),
        scratch_shapes=[pltpu.VMEM((t_len, ct, D_MODEL), F32)],
        compiler_params=_cparams(2),
        name="outproj",
    )(merged3, w_o_bf, x3, gate, final_g.reshape(1, D_MODEL))


def _state_to_blocks(state):
    n = state.shape[0]
    re = state[..., 0].reshape(n, N_BLOCKS, STATE_LANES).transpose(1, 0, 2)
    im = state[..., 1].reshape(n, N_BLOCKS, STATE_LANES).transpose(1, 0, 2)
    return re, im


def _blocks_to_state(re, im):
    n = re.shape[1]
    re = re.transpose(1, 0, 2).reshape(n, N_GROUPS, N_STATE)
    im = im.transpose(1, 0, 2).reshape(n, N_GROUPS, N_STATE)
    return jnp.stack([re, im], axis=-1)


def _group(x3, n_seq, ct, tm, mod, ssm0, conv0, p):
    m, t_len, _ = x3.shape
    cps = m // n_seq
    tiles_per_seq = cps // ct if cps > 1 else 0
    shift, scale, gate = mod
    if tiles_per_seq:
        sc_rows, sh_rows = scale, shift
    else:
        sc_rows, sh_rows = jnp.repeat(scale, t_len, axis=0), jnp.repeat(shift, t_len, axis=0)
    xn = _norm_mod(x3.reshape(m * t_len, D_MODEL), t_len, p["norm_g"], sc_rows, sh_rows, ct, tiles_per_seq)
    proj = _inproj(xn.reshape(t_len * m, D_MODEL), p["w_in"], tm)
    h0_re, h0_im = _state_to_blocks(ssm0)
    ya3, hr, hi = _s5(proj.reshape(t_len, m, IN_COLS), h0_re, h0_im, p["s5"], t_len, m, n_seq)
    b_in, v_last = _conv_branch(proj, conv0[:, 0], conv0[:, 1], p["conv_w"], t_len, m, n_seq)
    a_in = _glu(ya3.reshape(t_len * m, WIDTH), proj, p["w_glu"], p["b_glu"], 512)
    merged = _merge(a_in, b_in, proj, p["w_pa"], p["w_pb"], tm)
    out3 = _outproj(merged.reshape(t_len, m, D_MODEL), p["w_o"], x3, gate, p["final_g"], ct, tiles_per_seq)
    conv_new = v_last[:, cps - 1::cps, :].transpose(1, 0, 2)
    return out3, _blocks_to_state(hr, hi), conv_new


def kernel(x_prompt, x_sample, state_ssm, state_conv, c_prompt, c_sample, norm_g, w_ada, b_ada, w_in, lam_re, lam_im, log_dt, b_re, b_im, c_re, c_im, d_skip, w_glu, b_glu, w_pa, conv_w, w_pb, w_o, final_g):
    depth = norm_g.shape[0]
    assert depth == 1
    n_p, seq, _ = x_prompt.shape
    n_s, dec, _ = x_sample.shape
    t_p = 16
    assert seq % t_p == 0

    l = 0
    p = dict(norm_g=norm_g[l], w_in=w_in[l].astype(BF16), conv_w=conv_w[l],
             w_glu=w_glu[l].astype(BF16), b_glu=b_glu[l], w_pa=w_pa[l].astype(BF16),
             w_pb=w_pb[l].astype(BF16), w_o=w_o[l].astype(BF16), final_g=final_g,
             s5=_s5_params(lam_re[l], lam_im[l], log_dt[l], b_re[l], b_im[l], c_re[l], c_im[l], d_skip[l]))

    n_c = n_p + n_s
    pad = (-n_c) % 16
    c_all = jnp.concatenate([c_prompt, c_sample, jnp.zeros((pad, D_MODEL), F32)], axis=0)
    mod = _ada(c_all, w_ada[l], b_ada[l])
    mod_p = tuple(mod[:n_p, k * D_MODEL:(k + 1) * D_MODEL] for k in range(3))
    mod_s = tuple(mod[n_p:n_c, k * D_MODEL:(k + 1) * D_MODEL] for k in range(3))

    ssm_p0 = jnp.zeros((n_p, N_GROUPS, N_STATE, 2), F32)
    conv_p0 = jnp.zeros((n_p, 2, WIDTH), F32)
    y_p, ssm_p, conv_p = _group(x_prompt.reshape(n_p * seq // t_p, t_p, D_MODEL), n_p, 32, 1024,
                                mod_p, ssm_p0, conv_p0, p)
    y_prompt = y_p.reshape(n_p, seq, D_MODEL)

    y_sample, ssm_s, conv_s = _group(x_sample, n_s, 64, dec * n_s, mod_s, state_ssm[l], state_conv[l], p)

    return (y_prompt, y_sample, ssm_p[None], conv_p[None], ssm_s[None], conv_s[None])
```

```python
import functools

import jax
import jax.numpy as jnp
from jax import lax
from jax.experimental import pallas as pl
from jax.experimental.pallas import tpu as pltpu

F32 = jnp.float32
BF16 = jnp.bfloat16

D_MODEL = 4096
WIDTH = 2048
N_GROUPS = 128
GROUP = 16
N_STATE = 64
GROUPS_PER_BLOCK = 8
N_BLOCKS = N_GROUPS // GROUPS_PER_BLOCK
STATE_LANES = GROUPS_PER_BLOCK * N_STATE
LANES = 128
IN_COLS = 10 * WIDTH
EPS = 1e-6
VMEM_LIMIT = 56 * 1024 * 1024
NT_DIMS = (((1,), (1,)), ((), ()))


def _cparams(n_axes):
    return pltpu.CompilerParams(dimension_semantics=("arbitrary",) * n_axes,
                                vmem_limit_bytes=VMEM_LIMIT)


def _sigmoid(x):
    return jax.nn.sigmoid(x)


def _ada_kernel(c_ref, w_ref, b_ref, o_ref):
    c = c_ref[...]
    a = (c * _sigmoid(c)).astype(BF16)
    o_ref[...] = jnp.dot(a, w_ref[...].astype(BF16), preferred_element_type=F32) + b_ref[...]


def _ada(c_all, w_ada, b_ada, tn=512):
    rows = c_all.shape[0]
    n_out = w_ada.shape[1]
    return pl.pallas_call(
        _ada_kernel,
        grid=(n_out // tn,),
        in_specs=[pl.BlockSpec((rows, D_MODEL), lambda j: (0, 0)),
                  pl.BlockSpec((D_MODEL, tn), lambda j: (0, j)),
                  pl.BlockSpec((1, tn), lambda j: (0, j))],
        out_specs=pl.BlockSpec((rows, tn), lambda j: (0, j)),
        out_shape=jax.ShapeDtypeStruct((rows, n_out), F32),
        compiler_params=_cparams(1),
        name="ada_mod",
    )(c_all, w_ada, b_ada.reshape(1, n_out))


def _norm_kernel(x_ref, g_ref, sc_ref, sh_ref, o_ref, *, t_len, ct, tiles_per_seq):
    if tiles_per_seq:
        n = pl.program_id(0) // tiles_per_seq
        sc = sc_ref[pl.ds(n, 1), :]
        sh = sh_ref[pl.ds(n, 1), :]
    else:
        sc = sc_ref[...]
        sh = sh_ref[...]
    x = x_ref[...]
    y = x * lax.rsqrt(jnp.mean(x * x, axis=-1, keepdims=True) + EPS) * g_ref[...]
    y = (y * (1.0 + sc) + sh).astype(BF16)
    r = t_len * ct
    r_out = lax.broadcasted_iota(jnp.int32, (r, r), 0)
    r_in = lax.broadcasted_iota(jnp.int32, (r, r), 1)
    src_row = (r_out & (ct - 1)) * t_len + lax.shift_right_logical(r_out, ct.bit_length() - 1)
    perm = jnp.where(r_in == src_row, 1.0, 0.0).astype(BF16)
    yp = jnp.dot(perm, y, preferred_element_type=F32)
    o_ref[...] = yp.reshape(t_len, ct, D_MODEL).astype(BF16)


def _norm_mod(x2d, t_len, norm_g, scale, shift, ct, tiles_per_seq):
    assert ct & (ct - 1) == 0
    r = t_len * ct
    m = x2d.shape[0] // t_len
    if tiles_per_seq:
        mod_spec = pl.BlockSpec((scale.shape[0], D_MODEL), lambda i: (0, 0))
    else:
        mod_spec = pl.BlockSpec((r, D_MODEL), lambda i: (i, 0))
    return pl.pallas_call(
        functools.partial(_norm_kernel, t_len=t_len, ct=ct, tiles_per_seq=tiles_per_seq),
        grid=(m // ct,),
        in_specs=[pl.BlockSpec((r, D_MODEL), lambda i: (i, 0)),
                  pl.BlockSpec((1, D_MODEL), lambda i: (0, 0)),
                  mod_spec, mod_spec],
        out_specs=pl.BlockSpec((t_len, ct, D_MODEL), lambda i: (0, i, 0)),
        out_shape=jax.ShapeDtypeStruct((t_len, m, D_MODEL), BF16),
        compiler_params=_cparams(1),
        name="norm_mod",
    )(x2d, norm_g.reshape(1, D_MODEL), scale, shift)


def _activate(acc, seg):
    is_silu = jnp.logical_or(seg == 1, seg == 5)
    is_sig = seg >= 6
    sig = _sigmoid(acc)
    return jnp.where(is_sig, sig, jnp.where(is_silu, acc * sig, acc)).astype(BF16)


def _inproj_kernel(x_ref, w_ref, o_ref, *, tn):
    acc = jnp.dot(x_ref[...], w_ref[...], preferred_element_type=F32)
    o_ref[...] = _activate(acc, pl.program_id(1) // (WIDTH // tn))


def _inproj_cast_kernel(x_ref, w_ref, o_ref, wbf_ref, *, tn):
    w = w_ref[...].astype(BF16)
    wbf_ref[...] = w
    acc = jnp.dot(x_ref[...], w, preferred_element_type=F32)
    o_ref[...] = _activate(acc, pl.program_id(0) // (WIDTH // tn))


def _inproj_cast(xn, w_in, tn=512):
    rows = xn.shape[0]
    return pl.pallas_call(
        functools.partial(_inproj_cast_kernel, tn=tn),
        grid=(IN_COLS // tn,),
        in_specs=[pl.BlockSpec((rows, D_MODEL), lambda j: (0, 0)),
                  pl.BlockSpec((D_MODEL, tn), lambda j: (0, j))],
        out_specs=[pl.BlockSpec((rows, tn), lambda j: (0, j)),
                   pl.BlockSpec((D_MODEL, tn), lambda j: (0, j))],
        out_shape=[jax.ShapeDtypeStruct((rows, IN_COLS), BF16),
                   jax.ShapeDtypeStruct((D_MODEL, IN_COLS), BF16)],
        compiler_params=_cparams(1),
        name="inproj_cast",
    )(xn, w_in)


def _inproj(xn, w_in_bf, tm, tn=1024):
    rows = xn.shape[0]
    return pl.pallas_call(
        functools.partial(_inproj_kernel, tn=tn),
        grid=(rows // tm, IN_COLS // tn),
        in_specs=[pl.BlockSpec((tm, D_MODEL), lambda i, j: (i, 0)),
                  pl.BlockSpec((D_MODEL, tn), lambda i, j: (0, j))],
        out_specs=pl.BlockSpec((tm, tn), lambda i, j: (i, j)),
        out_shape=jax.ShapeDtypeStruct((rows, IN_COLS), BF16),
        compiler_params=_cparams(2),
        name="inproj",
    )(xn, w_in_bf)


def _cmul(ar, ai, br, bi):
    return ar * br - ai * bi, ar * bi + ai * br


def _s5_kernel(u_ref, h0r_ref, h0i_ref, lr_ref, li_ref, ldt_ref, btr_ref, bti_ref, ctr_ref, cti_ref, d_ref,
               y_ref, hr_out, hi_out,
               ucat, bend_r, bend_i, cend_r, cend_i, kpair, hl_r, hl_i, hp_r, hp_i, *, t_len, m, n_seq):
    blk = pl.program_id(0)

    @pl.when(blk == 0)
    def _():
        for ref in (bend_r, bend_i, cend_r, cend_i, kpair):
            ref[...] = jnp.zeros(ref.shape, ref.dtype)

    lam_r = lr_ref[0]
    lam_i = li_ref[0]
    dt = jnp.exp(ldt_ref[0])
    mag = jnp.exp(lam_r * dt)
    lb_r = mag * jnp.cos(lam_i * dt)
    lb_i = mag * jnp.sin(lam_i * dt)
    den = lam_r * lam_r + lam_i * lam_i
    nr = lb_r - 1.0
    co_r = (nr * lam_r + lb_i * lam_i) / den
    co_i = (lb_i * lam_r - nr * lam_i) / den
    bb_r, bb_i = _cmul(co_r, co_i, btr_ref[0], bti_ref[0])
    c_r = ctr_ref[0]
    c_i = cti_ref[0]
    pw = [(jnp.ones_like(lb_r), jnp.zeros_like(lb_r))]
    for _ in range(t_len):
        pw.append(_cmul(pw[-1][0], pw[-1][1], lb_r, lb_i))
    ar, ai = pw[t_len]

    lane = lax.broadcasted_iota(jnp.int32, (GROUP, LANES), 1)
    lo = lane < N_STATE

    def pair_tile(x, q):
        slab = x[:, q * LANES:(q + 1) * LANES]
        return jnp.concatenate([jnp.where(lo, slab, 0.0), jnp.where(lo, 0.0, slab)], axis=0).astype(BF16)

    grp_row = lax.broadcasted_iota(jnp.int32, (LANES, STATE_LANES), 0) // GROUP
    grp_lane = lax.broadcasted_iota(jnp.int32, (LANES, STATE_LANES), 1) // N_STATE
    same_group = grp_row == grp_lane

    def expand(x):
        return jnp.where(same_group, jnp.concatenate([x] * GROUPS_PER_BLOCK, axis=0), 0.0).astype(BF16)

    bbx_r = expand(bb_r)
    bbx_i = expand(bb_i)
    for s in range(t_len):
        ucat[:, s * LANES:(s + 1) * LANES] = u_ref[s]
        be_r, be_i = _cmul(pw[t_len - 1 - s][0], pw[t_len - 1 - s][1], bb_r, bb_i)
        ce_r, ce_i = _cmul(pw[s + 1][0], pw[s + 1][1], c_r, c_i)
        for q in range(4):
            r0 = s * LANES + q * 2 * GROUP
            rows = slice(r0, r0 + 2 * GROUP)
            cols = slice(q * LANES, (q + 1) * LANES)
            bend_r[rows, cols] = pair_tile(be_r, q)
            bend_i[rows, cols] = pair_tile(be_i, q)
            cend_r[rows, cols] = pair_tile(ce_r, q)
            cend_i[rows, cols] = pair_tile(-ce_i, q)
        cl_r, cl_i = _cmul(pw[s][0], pw[s][1], c_r, c_i)
        k_lag = (lax.dot_general(bbx_r, expand(cl_r), NT_DIMS, preferred_element_type=F32)
                 - lax.dot_general(bbx_i, expand(cl_i), NT_DIMS, preferred_element_type=F32))
        k_lag = k_lag.astype(BF16)
        kpair[(t_len - 1 - s) * LANES:(t_len - s) * LANES, LANES:] = k_lag
        if s <= t_len - 2:
            kpair[(t_len - 2 - s) * LANES:(t_len - 1 - s) * LANES, :LANES] = k_lag

    uc = ucat[...]
    hl_r[...] = jnp.dot(uc, bend_r[...], preferred_element_type=F32)
    hl_i[...] = jnp.dot(uc, bend_i[...], preferred_element_type=F32)

    cps = m // n_seq
    if cps == 1:
        h_r = h0r_ref[0]
        h_i = h0i_ref[0]
        hp_r[...] = h_r
        hp_i[...] = h_i
        hr_out[0] = ar * h_r - ai * h_i + hl_r[...]
        hi_out[0] = ar * h_i + ai * h_r + hl_i[...]
    else:
        def body(c, carry):
            new = []
            for n in range(n_seq):
                h_r, h_i = carry[n]
                row = pl.ds(n * cps + c, 1)
                hp_r[row, :] = h_r
                hp_i[row, :] = h_i
                new.append((ar * h_r - ai * h_i + hl_r[row, :], ar * h_i + ai * h_r + hl_i[row, :]))
            return tuple(new)

        init = tuple((h0r_ref[0, n:n + 1, :], h0i_ref[0, n:n + 1, :]) for n in range(n_seq))
        fin = lax.fori_loop(0, cps, body, init)
        for n in range(n_seq):
            hr_out[0, n:n + 1, :] = fin[n][0]
            hi_out[0, n:n + 1, :] = fin[n][1]

    ycar = (lax.dot_general(hp_r[...].astype(BF16), cend_r[...], NT_DIMS, preferred_element_type=F32)
            + lax.dot_general(hp_i[...].astype(BF16), cend_i[...], NT_DIMS, preferred_element_type=F32))
    dvec = d_ref[0]
    for t in range(0, t_len, 2):
        y2 = jnp.dot(ucat[:, :(t + 2) * LANES], kpair[(t_len - 2 - t) * LANES:, :],
                     preferred_element_type=F32) + ycar[:, t * LANES:(t + 2) * LANES]
        for k in range(2):
            y = y2[:, k * LANES:(k + 1) * LANES] + dvec * u_ref[t + k].astype(F32)
            y_ref[t + k] = jax.nn.gelu(y).astype(BF16)


def _s5_params(lam_re, lam_im, log_dt, b_re, b_im, c_re, c_im, d_skip):
    row = lambda x: x.astype(F32).reshape(N_BLOCKS, 1, STATE_LANES)
    ldt = jnp.broadcast_to(log_dt.astype(F32)[:, None], (N_GROUPS, N_STATE))
    bt = lambda x: (x.astype(F32).reshape(N_BLOCKS, GROUPS_PER_BLOCK, N_STATE, GROUP)
                    .transpose(0, 3, 1, 2).reshape(N_BLOCKS, GROUP, STATE_LANES))
    ct = lambda x: (x.astype(F32).reshape(N_BLOCKS, GROUPS_PER_BLOCK, GROUP, N_STATE)
                    .transpose(0, 2, 1, 3).reshape(N_BLOCKS, GROUP, STATE_LANES))
    return (row(lam_re), row(lam_im), row(ldt), bt(b_re), bt(b_im), ct(c_re), ct(c_im),
            d_skip.astype(F32).reshape(N_BLOCKS, 1, LANES))


def _s5(proj3, h0_re, h0_im, s5p, t_len, m, n_seq):
    kern = functools.partial(_s5_kernel, t_len=t_len, m=m, n_seq=n_seq)
    blk3 = lambda b: (b, 0, 0)
    row_spec = pl.BlockSpec((1, 1, STATE_LANES), blk3)
    mat_spec = pl.BlockSpec((1, GROUP, STATE_LANES), blk3)
    state_spec = pl.BlockSpec((1, n_seq, STATE_LANES), blk3)
    end_mat = pltpu.VMEM((t_len * LANES, STATE_LANES), BF16)
    chunk_state = pltpu.VMEM((m, STATE_LANES), F32)
    return pl.pallas_call(
        kern,
        grid=(N_BLOCKS,),
        in_specs=[pl.BlockSpec((t_len, m, LANES), lambda b: (0, 0, b)),
                  state_spec, state_spec,
                  row_spec, row_spec, row_spec,
                  mat_spec, mat_spec, mat_spec, mat_spec,
                  pl.BlockSpec((1, 1, LANES), blk3)],
        out_specs=[pl.BlockSpec((t_len, m, LANES), lambda b: (0, 0, b)), state_spec, state_spec],
        out_shape=[jax.ShapeDtypeStruct((t_len, m, WIDTH), BF16),
                   jax.ShapeDtypeStruct((N_BLOCKS, n_seq, STATE_LANES), F32),
                   jax.ShapeDtypeStruct((N_BLOCKS, n_seq, STATE_LANES), F32)],
        scratch_shapes=[pltpu.VMEM((m, t_len * LANES), BF16),
                        end_mat, end_mat, end_mat, end_mat,
                        pltpu.VMEM((t_len * LANES, 2 * LANES), BF16),
                        chunk_state, chunk_state, chunk_state, chunk_state],
        compiler_params=_cparams(1),
        name="s5_scan",
    )(proj3, h0_re, h0_im, *s5p)


def _conv_kernel(hb, cb, hb_l1, cb_l1, hb_l2, cb_l2, bb, szb, buf0, buf1, w_ref, bin_ref, v_ref,
                 vm1, vm2, *, m, n_seq):
    s = pl.program_id(1)
    cps = m // n_seq

    def prev_chunk(v, first_rows):
        if cps == 1:
            return first_rows
        out = pltpu.roll(v, 1, 0)
        rid = lax.broadcasted_iota(jnp.int32, v.shape, 0)
        for n in range(n_seq):
            out = jnp.where(rid == n * cps, first_rows[n:n + 1, :], out)
        return out

    @pl.when(s == 0)
    def _():
        vm1[...] = prev_chunk(hb_l1[...].astype(F32) * cb_l1[...].astype(F32), buf1[...])
        vm2[...] = prev_chunk(hb_l2[...].astype(F32) * cb_l2[...].astype(F32), buf0[...])

    v0 = hb[...].astype(F32) * cb[...].astype(F32)
    v1 = vm1[...]
    conv = vm2[...] * w_ref[0:1, :] + v1 * w_ref[1:2, :] + v0 * w_ref[2:3, :]
    bin_ref[...] = (bb[...].astype(F32) * conv * szb[...].astype(F32)).astype(BF16)
    vm2[...] = v1
    vm1[...] = v0
    v_ref[0] = v0


def _conv_branch(proj, buf0, buf1, conv_w, t_len, m, n_seq, tn=512):
    nb = WIDTH // tn
    kern = functools.partial(_conv_kernel, m=m, n_seq=n_seq)
    cur = lambda seg: pl.BlockSpec((m, tn), lambda j, s: (s, seg * nb + j))
    last = lambda seg, back: pl.BlockSpec((m, tn), lambda j, s: (t_len - back, seg * nb + j))
    return pl.pallas_call(
        kern,
        grid=(nb, t_len),
        in_specs=[cur(2), cur(3), last(2, 1), last(3, 1), last(2, 2), last(3, 2), cur(4), cur(5),
                  pl.BlockSpec((n_seq, tn), lambda j, s: (0, j)),
                  pl.BlockSpec((n_seq, tn), lambda j, s: (0, j)),
                  pl.BlockSpec((3, tn), lambda j, s: (0, j))],
        out_specs=[pl.BlockSpec((m, tn), lambda j, s: (s, j)),
                   pl.BlockSpec((1, m, tn), lambda j, s: (jnp.maximum(s - (t_len - 2), 0), 0, j))],
        out_shape=[jax.ShapeDtypeStruct((t_len * m, WIDTH), BF16),
                   jax.ShapeDtypeStruct((2, m, WIDTH), F32)],
        scratch_shapes=[pltpu.VMEM((m, tn), F32), pltpu.VMEM((m, tn), F32)],
        compiler_params=_cparams(2),
        name="conv_branch",
    )(proj, proj, proj, proj, proj, proj, proj, proj, buf0, buf1, conv_w)


def _glu_kernel(ya_ref, sza_ref, w_ref, b_ref, o_ref):
    ya = ya_ref[...]
    glu = jnp.dot(ya, w_ref[...], preferred_element_type=F32) + b_ref[...]
    o_ref[...] = (ya.astype(F32) * _sigmoid(glu) * sza_ref[...].astype(F32)).astype(BF16)


def _glu(ya, proj, w_glu_bf, b_glu, tm):
    rows = ya.shape[0]
    return pl.pallas_call(
        _glu_kernel,
        grid=(rows // tm,),
        in_specs=[pl.BlockSpec((tm, WIDTH), lambda i: (i, 0)),
                  pl.BlockSpec((tm, WIDTH), lambda i: (i, 1)),
                  pl.BlockSpec((WIDTH, WIDTH), lambda i: (0, 0)),
                  pl.BlockSpec((1, WIDTH), lambda i: (0, 0))],
        out_specs=pl.BlockSpec((tm, WIDTH), lambda i: (i, 0)),
        out_shape=jax.ShapeDtypeStruct((rows, WIDTH), BF16),
        compiler_params=_cparams(1),
        name="glu_gate",
    )(ya, proj, w_glu_bf, b_glu.reshape(1, WIDTH))


def _merge_kernel(a_ref, b_ref, wa_ref, wb_ref, sga_ref, sgb_ref, o_ref):
    pa = jnp.dot(a_ref[...], wa_ref[...], preferred_element_type=F32)
    pb = jnp.dot(b_ref[...], wb_ref[...], preferred_element_type=F32)
    o_ref[...] = (sga_ref[...] * pa + sgb_ref[...] * pb).astype(BF16)


def _merge_cast_kernel(a_ref, b_ref, wa_ref, wb_ref, sga_ref, sgb_ref, o_ref, wabf_ref, wbbf_ref):
    wa = wa_ref[...].astype(BF16)
    wb = wb_ref[...].astype(BF16)
    wabf_ref[...] = wa
    wbbf_ref[...] = wb
    pa = jnp.dot(a_ref[...], wa, preferred_element_type=F32)
    pb = jnp.dot(b_ref[...], wb, preferred_element_type=F32)
    o_ref[...] = (sga_ref[...] * pa + sgb_ref[...] * pb).astype(BF16)


def _merge_cast(a_in, b_in, proj, w_pa, w_pb, tn=512):
    rows = a_in.shape[0]
    nb = WIDTH // tn
    w_spec = pl.BlockSpec((WIDTH, tn), lambda j: (0, j))
    return pl.pallas_call(
        _merge_cast_kernel,
        grid=(D_MODEL // tn,),
        in_specs=[pl.BlockSpec((rows, WIDTH), lambda j: (0, 0)),
                  pl.BlockSpec((rows, WIDTH), lambda j: (0, 0)),
                  w_spec, w_spec,
                  pl.BlockSpec((rows, tn), lambda j: (0, 6 * nb + j)),
                  pl.BlockSpec((rows, tn), lambda j: (0, 8 * nb + j))],
        out_specs=[pl.BlockSpec((rows, tn), lambda j: (0, j)), w_spec, w_spec],
        out_shape=[jax.ShapeDtypeStruct((rows, D_MODEL), BF16),
                   jax.ShapeDtypeStruct((WIDTH, D_MODEL), BF16),
                   jax.ShapeDtypeStruct((WIDTH, D_MODEL), BF16)],
        compiler_params=_cparams(1),
        name="merge_cast",
    )(a_in, b_in, w_pa, w_pb, proj, proj)


def _merge(a_in, b_in, proj, w_pa_bf, w_pb_bf, tm, tn=512):
    rows = a_in.shape[0]
    nb = WIDTH // tn
    w_spec = pl.BlockSpec((WIDTH, tn), lambda i, j: (0, j))
    return pl.pallas_call(
        _merge_kernel,
        grid=(rows // tm, D_MODEL // tn),
        in_specs=[pl.BlockSpec((tm, WIDTH), lambda i, j: (i, 0)),
                  pl.BlockSpec((tm, WIDTH), lambda i, j: (i, 0)),
                  w_spec, w_spec,
                  pl.BlockSpec((tm, tn), lambda i, j: (i, 6 * nb + j)),
                  pl.BlockSpec((tm, tn), lambda i, j: (i, 8 * nb + j))],
        out_specs=pl.BlockSpec((tm, tn), lambda i, j: (i, j)),
        out_shape=jax.ShapeDtypeStruct((rows, D_MODEL), BF16),
        compiler_params=_cparams(2),
        name="merge",
    )(a_in, b_in, w_pa_bf, w_pb_bf, proj, proj)


def _outproj_kernel(m_ref, w_ref, x_ref, gate_ref, fg_ref, o_ref, hres, *, t_len, ct, tn, tiles_per_seq):
    j = pl.program_id(1)
    part = jnp.dot(m_ref[...].reshape(t_len * ct, D_MODEL), w_ref[...], preferred_element_type=F32)
    if tiles_per_seq:
        gate = gate_ref[pl.ds(pl.program_id(0) // tiles_per_seq, 1), :]
    else:
        gate = gate_ref[...]
    col = pl.ds(pl.multiple_of(j * tn, tn), tn)
    for s in range(t_len):
        hres[s, :, col] = x_ref[:, s, :] + gate * part[s * ct:(s + 1) * ct, :]

    @pl.when(j == pl.num_programs(1) - 1)
    def _():
        for s in range(t_len):
            h = hres[s]
            o_ref[:, s, :] = h * lax.rsqrt(jnp.mean(h * h, axis=-1, keepdims=True) + EPS) * fg_ref[...]


def _outproj(merged3, w_o_bf, x3, gate, final_g, ct, tiles_per_seq, tn=512):
    t_len, m, _ = merged3.shape
    q = gate.shape[0] if tiles_per_seq else ct
    gate_spec = pl.BlockSpec((q, tn), (lambda i, j: (0, j)) if tiles_per_seq else (lambda i, j: (i, j)))
    kern = functools.partial(_outproj_kernel, t_len=t_len, ct=ct, tn=tn, tiles_per_seq=tiles_per_seq)
    return pl.pallas_call(
        kern,
        grid=(m // ct, D_MODEL // tn),
        in_specs=[pl.BlockSpec((t_len, ct, D_MODEL), lambda i, j: (0, i, 0)),
                  pl.BlockSpec((D_MODEL, tn), lambda i, j: (0, j)),
                  pl.BlockSpec((ct, t_len, tn), lambda i, j: (i, 0, j)),
                  gate_spec,
                  pl.BlockSpec((1, D_MODEL), lambda i, j: (0, 0))],
        out_specs=pl.BlockSpec((ct, t_len, D_MODEL), lambda i, j: (i, 0, 0)),
        out_shape=jax.ShapeDtypeStruct((m, t_len, D_MODEL), F32),
        scratch_shapes=[pltpu.VMEM((t_len, ct, D_MODEL), F32)],
        compiler_params=_cparams(2),
        name="outproj",
    )(merged3, w_o_bf, x3, gate, final_g.reshape(1, D_MODEL))


def _state_to_blocks(state):
    n = state.shape[0]
    re = state[..., 0].reshape(n, N_BLOCKS, STATE_LANES).transpose(1, 0, 2)
    im = state[..., 1].reshape(n, N_BLOCKS, STATE_LANES).transpose(1, 0, 2)
    return re, im


def _blocks_to_state(re, im):
    n = re.shape[1]
    re = re.transpose(1, 0, 2).reshape(n, N_GROUPS, N_STATE)
    im = im.transpose(1, 0, 2).reshape(n, N_GROUPS, N_STATE)
    return jnp.stack([re, im], axis=-1)


def _group(x3, n_seq, ct, tm, mod, ssm0, conv0, p):
    m, t_len, _ = x3.shape
    cps = m // n_seq
    tiles_per_seq = cps // ct if cps > 1 else 0
    shift, scale, gate = mod
    if tiles_per_seq:
        sc_rows, sh_rows = scale, shift
    else:
        sc_rows, sh_rows = jnp.repeat(scale, t_len, axis=0), jnp.repeat(shift, t_len, axis=0)
    xn = _norm_mod(x3.reshape(m * t_len, D_MODEL), t_len, p["norm_g"], sc_rows, sh_rows, ct, tiles_per_seq)
    xn = xn.reshape(t_len * m, D_MODEL)
    if "w_in_bf" in p:
        proj = _inproj(xn, p["w_in_bf"], tm)
    else:
        proj, p["w_in_bf"] = _inproj_cast(xn, p["w_in"])
    h0_re, h0_im = _state_to_blocks(ssm0)
    ya3, hr, hi = _s5(proj.reshape(t_len, m, IN_COLS), h0_re, h0_im, p["s5"], t_len, m, n_seq)
    b_in, v_last = _conv_branch(proj, conv0[:, 0], conv0[:, 1], p["conv_w"], t_len, m, n_seq)
    a_in = _glu(ya3.reshape(t_len * m, WIDTH), proj, p["w_glu"], p["b_glu"], 512)
    if "w_pa_bf" in p:
        merged = _merge(a_in, b_in, proj, p["w_pa_bf"], p["w_pb_bf"], tm)
    else:
        merged, p["w_pa_bf"], p["w_pb_bf"] = _merge_cast(a_in, b_in, proj, p["w_pa"], p["w_pb"])
    out3 = _outproj(merged.reshape(t_len, m, D_MODEL), p["w_o"], x3, gate, p["final_g"], ct, tiles_per_seq)
    conv_new = v_last[:, cps - 1::cps, :].transpose(1, 0, 2)
    return out3, _blocks_to_state(hr, hi), conv_new


def kernel(x_prompt, x_sample, state_ssm, state_conv, c_prompt, c_sample, norm_g, w_ada, b_ada, w_in, lam_re, lam_im, log_dt, b_re, b_im, c_re, c_im, d_skip, w_glu, b_glu, w_pa, conv_w, w_pb, w_o, final_g):
    depth = norm_g.shape[0]
    assert depth == 1
    n_p, seq, _ = x_prompt.shape
    n_s, dec, _ = x_sample.shape
    t_p = 16
    assert seq % t_p == 0

    l = 0
    p = dict(norm_g=norm_g[l], w_in=w_in[l], conv_w=conv_w[l],
             w_glu=w_glu[l].astype(BF16), b_glu=b_glu[l], w_pa=w_pa[l],
             w_pb=w_pb[l], w_o=w_o[l].astype(BF16), final_g=final_g,
             s5=_s5_params(lam_re[l], lam_im[l], log_dt[l], b_re[l], b_im[l], c_re[l], c_im[l], d_skip[l]))

    n_c = n_p + n_s
    pad = (-n_c) % 16
    c_all = jnp.concatenate([c_prompt, c_sample, jnp.zeros((pad, D_MODEL), F32)], axis=0)
    mod = _ada(c_all, w_ada[l], b_ada[l])
    mod_p = tuple(mod[:n_p, k * D_MODEL:(k + 1) * D_MODEL] for k in range(3))
    mod_s = tuple(mod[n_p:n_c, k * D_MODEL:(k + 1) * D_MODEL] for k in range(3))

    y_sample, ssm_s, conv_s = _group(x_sample, n_s, 64, dec * n_s, mod_s, state_ssm[l], state_conv[l], p)

    ssm_p0 = jnp.zeros((n_p, N_GROUPS, N_STATE, 2), F32)
    conv_p0 = jnp.zeros((n_p, 2, WIDTH), F32)
    y_p, ssm_p, conv_p = _group(x_prompt.reshape(n_p * seq // t_p, t_p, D_MODEL), n_p, 32, 1024,
                                mod_p, ssm_p0, conv_p0, p)
    y_prompt = y_p.reshape(n_p, seq, D_MODEL)

    return (y_prompt, y_sample, ssm_p[None], conv_p[None], ssm_s[None], conv_s[None])
```

```python
import functools

import jax
import jax.numpy as jnp
from jax import lax
from jax.experimental import pallas as pl
from jax.experimental.pallas import tpu as pltpu

F32 = jnp.float32
BF16 = jnp.bfloat16

D_MODEL = 4096
WIDTH = 2048
N_GROUPS = 128
GROUP = 16
N_STATE = 64
GROUPS_PER_BLOCK = 8
N_BLOCKS = N_GROUPS // GROUPS_PER_BLOCK
STATE_LANES = GROUPS_PER_BLOCK * N_STATE
LANES = 128
IN_COLS = 10 * WIDTH
EPS = 1e-6
VMEM_LIMIT = 56 * 1024 * 1024
NT_DIMS = (((1,), (1,)), ((), ()))


def _cparams(n_axes):
    return pltpu.CompilerParams(dimension_semantics=("arbitrary",) * n_axes,
                                vmem_limit_bytes=VMEM_LIMIT)


def _sigmoid(x):
    return jax.nn.sigmoid(x)


def _ada_kernel(c_ref, w_ref, b_ref, o_ref):
    c = c_ref[...]
    a = (c * _sigmoid(c)).astype(BF16)
    o_ref[...] = jnp.dot(a, w_ref[...].astype(BF16), preferred_element_type=F32) + b_ref[...]


def _ada(c_all, w_ada, b_ada, tn=512):
    rows = c_all.shape[0]
    n_out = w_ada.shape[1]
    return pl.pallas_call(
        _ada_kernel,
        grid=(n_out // tn,),
        in_specs=[pl.BlockSpec((rows, D_MODEL), lambda j: (0, 0)),
                  pl.BlockSpec((D_MODEL, tn), lambda j: (0, j)),
                  pl.BlockSpec((1, tn), lambda j: (0, j))],
        out_specs=pl.BlockSpec((rows, tn), lambda j: (0, j)),
        out_shape=jax.ShapeDtypeStruct((rows, n_out), F32),
        compiler_params=_cparams(1),
        name="ada_mod",
    )(c_all, w_ada, b_ada.reshape(1, n_out))


def _norm_kernel(x_ref, g_ref, sc_ref, sh_ref, o_ref, *, t_len, ct, tiles_per_seq):
    if tiles_per_seq:
        n = pl.program_id(0) // tiles_per_seq
        sc = sc_ref[pl.ds(n, 1), :]
        sh = sh_ref[pl.ds(n, 1), :]
    else:
        sc = sc_ref[...]
        sh = sh_ref[...]
    x = x_ref[...]
    y = x * lax.rsqrt(jnp.mean(x * x, axis=-1, keepdims=True) + EPS) * g_ref[...]
    y = (y * (1.0 + sc) + sh).astype(BF16)
    r = t_len * ct
    r_out = lax.broadcasted_iota(jnp.int32, (r, r), 0)
    r_in = lax.broadcasted_iota(jnp.int32, (r, r), 1)
    src_row = (r_out & (ct - 1)) * t_len + lax.shift_right_logical(r_out, ct.bit_length() - 1)
    perm = jnp.where(r_in == src_row, 1.0, 0.0).astype(BF16)
    yp = jnp.dot(perm, y, preferred_element_type=F32)
    o_ref[...] = yp.reshape(t_len, ct, D_MODEL).astype(BF16)


def _norm_mod(x2d, t_len, norm_g, scale, shift, ct, tiles_per_seq):
    assert ct & (ct - 1) == 0
    r = t_len * ct
    m = x2d.shape[0] // t_len
    if tiles_per_seq:
        mod_spec = pl.BlockSpec((scale.shape[0], D_MODEL), lambda i: (0, 0))
    else:
        mod_spec = pl.BlockSpec((r, D_MODEL), lambda i: (i, 0))
    return pl.pallas_call(
        functools.partial(_norm_kernel, t_len=t_len, ct=ct, tiles_per_seq=tiles_per_seq),
        grid=(m // ct,),
        in_specs=[pl.BlockSpec((r, D_MODEL), lambda i: (i, 0)),
                  pl.BlockSpec((1, D_MODEL), lambda i: (0, 0)),
                  mod_spec, mod_spec],
        out_specs=pl.BlockSpec((t_len, ct, D_MODEL), lambda i: (0, i, 0)),
        out_shape=jax.ShapeDtypeStruct((t_len, m, D_MODEL), BF16),
        compiler_params=_cparams(1),
        name="norm_mod",
    )(x2d, norm_g.reshape(1, D_MODEL), scale, shift)


def _activate(acc, seg):
    is_silu = jnp.logical_or(seg == 1, seg == 5)
    is_sig = seg >= 6
    sig = _sigmoid(acc)
    return jnp.where(is_sig, sig, jnp.where(is_silu, acc * sig, acc)).astype(BF16)


def _inproj_kernel(x_ref, w_ref, o_ref, *, tn):
    acc = jnp.dot(x_ref[...], w_ref[...], preferred_element_type=F32)
    o_ref[...] = _activate(acc, pl.program_id(1) // (WIDTH // tn))


def _inproj_cast_kernel(x_ref, w_ref, o_ref, wbf_ref, *, tn):
    w = w_ref[...].astype(BF16)
    wbf_ref[...] = w
    acc = jnp.dot(x_ref[...], w, preferred_element_type=F32)
    o_ref[...] = _activate(acc, pl.program_id(0) // (WIDTH // tn))


def _inproj_cast(xn, w_in, tn=512):
    rows = xn.shape[0]
    return pl.pallas_call(
        functools.partial(_inproj_cast_kernel, tn=tn),
        grid=(IN_COLS // tn,),
        in_specs=[pl.BlockSpec((rows, D_MODEL), lambda j: (0, 0)),
                  pl.BlockSpec((D_MODEL, tn), lambda j: (0, j))],
        out_specs=[pl.BlockSpec((rows, tn), lambda j: (0, j)),
                   pl.BlockSpec((D_MODEL, tn), lambda j: (0, j))],
        out_shape=[jax.ShapeDtypeStruct((rows, IN_COLS), BF16),
                   jax.ShapeDtypeStruct((D_MODEL, IN_COLS), BF16)],
        compiler_params=_cparams(1),
        name="inproj_cast",
    )(xn, w_in)


def _inproj(xn, w_in_bf, tm, tn=1024):
    rows = xn.shape[0]
    return pl.pallas_call(
        functools.partial(_inproj_kernel, tn=tn),
        grid=(rows // tm, IN_COLS // tn),
        in_specs=[pl.BlockSpec((tm, D_MODEL), lambda i, j: (i, 0)),
                  pl.BlockSpec((D_MODEL, tn), lambda i, j: (0, j))],
        out_specs=pl.BlockSpec((tm, tn), lambda i, j: (i, j)),
        out_shape=jax.ShapeDtypeStruct((rows, IN_COLS), BF16),
        compiler_params=_cparams(2),
        name="inproj",
    )(xn, w_in_bf)


def _cmul(ar, ai, br, bi):
    return ar * br - ai * bi, ar * bi + ai * br


def _s5_kernel(u_ref, h0r_ref, h0i_ref, lr_ref, li_ref, ldt_ref, btr_ref, bti_ref, ctr_ref, cti_ref, d_ref,
               y_ref, hr_out, hi_out,
               ucat, bend_r, bend_i, cend_r, cend_i, kpair, hl_r, hl_i, hp_r, hp_i, *, t_len, m, n_seq):
    blk = pl.program_id(0)

    @pl.when(blk == 0)
    def _():
        for ref in (bend_r, bend_i, cend_r, cend_i, kpair):
            ref[...] = jnp.zeros(ref.shape, ref.dtype)

    lam_r = lr_ref[0]
    lam_i = li_ref[0]
    dt = jnp.exp(ldt_ref[0])
    mag = jnp.exp(lam_r * dt)
    lb_r = mag * jnp.cos(lam_i * dt)
    lb_i = mag * jnp.sin(lam_i * dt)
    den = lam_r * lam_r + lam_i * lam_i
    nr = lb_r - 1.0
    co_r = (nr * lam_r + lb_i * lam_i) / den
    co_i = (lb_i * lam_r - nr * lam_i) / den
    bb_r, bb_i = _cmul(co_r, co_i, btr_ref[0], bti_ref[0])
    c_r = ctr_ref[0]
    c_i = cti_ref[0]
    pw = [(jnp.ones_like(lb_r), jnp.zeros_like(lb_r))]
    for _ in range(t_len):
        pw.append(_cmul(pw[-1][0], pw[-1][1], lb_r, lb_i))
    ar, ai = pw[t_len]

    lane = lax.broadcasted_iota(jnp.int32, (GROUP, LANES), 1)
    lo = lane < N_STATE

    def pair_tile(x, q):
        slab = x[:, q * LANES:(q + 1) * LANES]
        return jnp.concatenate([jnp.where(lo, slab, 0.0), jnp.where(lo, 0.0, slab)], axis=0).astype(BF16)

    grp_row = lax.broadcasted_iota(jnp.int32, (LANES, STATE_LANES), 0) // GROUP
    grp_lane = lax.broadcasted_iota(jnp.int32, (LANES, STATE_LANES), 1) // N_STATE
    same_group = grp_row == grp_lane

    def expand(x):
        return jnp.where(same_group, jnp.concatenate([x] * GROUPS_PER_BLOCK, axis=0), 0.0).astype(BF16)

    bbx_r = expand(bb_r)
    bbx_i = expand(bb_i)
    for s in range(t_len):
        ucat[:, s * LANES:(s + 1) * LANES] = u_ref[s]
        be_r, be_i = _cmul(pw[t_len - 1 - s][0], pw[t_len - 1 - s][1], bb_r, bb_i)
        ce_r, ce_i = _cmul(pw[s + 1][0], pw[s + 1][1], c_r, c_i)
        for q in range(4):
            r0 = s * LANES + q * 2 * GROUP
            rows = slice(r0, r0 + 2 * GROUP)
            cols = slice(q * LANES, (q + 1) * LANES)
            bend_r[rows, cols] = pair_tile(be_r, q)
            bend_i[rows, cols] = pair_tile(be_i, q)
            cend_r[rows, cols] = pair_tile(ce_r, q)
            cend_i[rows, cols] = pair_tile(-ce_i, q)
        cl_r, cl_i = _cmul(pw[s][0], pw[s][1], c_r, c_i)
        k_lag = (lax.dot_general(bbx_r, expand(cl_r), NT_DIMS, preferred_element_type=F32)
                 - lax.dot_general(bbx_i, expand(cl_i), NT_DIMS, preferred_element_type=F32))
        k_lag = k_lag.astype(BF16)
        kpair[(t_len - 1 - s) * LANES:(t_len - s) * LANES, LANES:] = k_lag
        if s <= t_len - 2:
            kpair[(t_len - 2 - s) * LANES:(t_len - 1 - s) * LANES, :LANES] = k_lag

    uc = ucat[...]
    hl_r[...] = jnp.dot(uc, bend_r[...], preferred_element_type=F32)
    hl_i[...] = jnp.dot(uc, bend_i[...], preferred_element_type=F32)

    cps = m // n_seq
    if cps == 1:
        h_r = h0r_ref[0]
        h_i = h0i_ref[0]
        hp_r[...] = h_r
        hp_i[...] = h_i
        hr_out[0] = ar * h_r - ai * h_i + hl_r[...]
        hi_out[0] = ar * h_i + ai * h_r + hl_i[...]
    else:
        def body(c, carry):
            new = []
            for n in range(n_seq):
                h_r, h_i = carry[n]
                row = pl.ds(n * cps + c, 1)
                hp_r[row, :] = h_r
                hp_i[row, :] = h_i
                new.append((ar * h_r - ai * h_i + hl_r[row, :], ar * h_i + ai * h_r + hl_i[row, :]))
            return tuple(new)

        init = tuple((h0r_ref[0, n:n + 1, :], h0i_ref[0, n:n + 1, :]) for n in range(n_seq))
        fin = lax.fori_loop(0, cps, body, init)
        for n in range(n_seq):
            hr_out[0, n:n + 1, :] = fin[n][0]
            hi_out[0, n:n + 1, :] = fin[n][1]

    ycar = (lax.dot_general(hp_r[...].astype(BF16), cend_r[...], NT_DIMS, preferred_element_type=F32)
            + lax.dot_general(hp_i[...].astype(BF16), cend_i[...], NT_DIMS, preferred_element_type=F32))
    dvec = d_ref[0]
    for t in range(0, t_len, 2):
        y2 = jnp.dot(ucat[:, :(t + 2) * LANES], kpair[(t_len - 2 - t) * LANES:, :],
                     preferred_element_type=F32) + ycar[:, t * LANES:(t + 2) * LANES]
        for k in range(2):
            y = y2[:, k * LANES:(k + 1) * LANES] + dvec * u_ref[t + k].astype(F32)
            y_ref[t + k] = jax.nn.gelu(y).astype(BF16)


def _s5_params(lam_re, lam_im, log_dt, b_re, b_im, c_re, c_im, d_skip):
    row = lambda x: x.astype(F32).reshape(N_BLOCKS, 1, STATE_LANES)
    ldt = jnp.broadcast_to(log_dt.astype(F32)[:, None], (N_GROUPS, N_STATE))
    bt = lambda x: (x.astype(F32).reshape(N_BLOCKS, GROUPS_PER_BLOCK, N_STATE, GROUP)
                    .transpose(0, 3, 1, 2).reshape(N_BLOCKS, GROUP, STATE_LANES))
    ct = lambda x: (x.astype(F32).reshape(N_BLOCKS, GROUPS_PER_BLOCK, GROUP, N_STATE)
                    .transpose(0, 2, 1, 3).reshape(N_BLOCKS, GROUP, STATE_LANES))
    return (row(lam_re), row(lam_im), row(ldt), bt(b_re), bt(b_im), ct(c_re), ct(c_im),
            d_skip.astype(F32).reshape(N_BLOCKS, 1, LANES))


def _s5(proj3, h0_re, h0_im, s5p, t_len, m, n_seq):
    kern = functools.partial(_s5_kernel, t_len=t_len, m=m, n_seq=n_seq)
    blk3 = lambda b: (b, 0, 0)
    row_spec = pl.BlockSpec((1, 1, STATE_LANES), blk3)
    mat_spec = pl.BlockSpec((1, GROUP, STATE_LANES), blk3)
    state_spec = pl.BlockSpec((1, n_seq, STATE_LANES), blk3)
    end_mat = pltpu.VMEM((t_len * LANES, STATE_LANES), BF16)
    chunk_state = pltpu.VMEM((m, STATE_LANES), F32)
    return pl.pallas_call(
        kern,
        grid=(N_BLOCKS,),
        in_specs=[pl.BlockSpec((t_len, m, LANES), lambda b: (0, 0, b)),
                  state_spec, state_spec,
                  row_spec, row_spec, row_spec,
                  mat_spec, mat_spec, mat_spec, mat_spec,
                  pl.BlockSpec((1, 1, LANES), blk3)],
        out_specs=[pl.BlockSpec((t_len, m, LANES), lambda b: (0, 0, b)), state_spec, state_spec],
        out_shape=[jax.ShapeDtypeStruct((t_len, m, WIDTH), BF16),
                   jax.ShapeDtypeStruct((N_BLOCKS, n_seq, STATE_LANES), F32),
                   jax.ShapeDtypeStruct((N_BLOCKS, n_seq, STATE_LANES), F32)],
        scratch_shapes=[pltpu.VMEM((m, t_len * LANES), BF16),
                        end_mat, end_mat, end_mat, end_mat,
                        pltpu.VMEM((t_len * LANES, 2 * LANES), BF16),
                        chunk_state, chunk_state, chunk_state, chunk_state],
        compiler_params=_cparams(1),
        name="s5_scan",
    )(proj3, h0_re, h0_im, *s5p)


def _conv_kernel(hb, cb, hb_l1, cb_l1, hb_l2, cb_l2, bb, szb, buf0, buf1, w_ref, bin_ref, v_ref,
                 vm1, vm2, *, m, n_seq):
    s = pl.program_id(1)
    cps = m // n_seq

    def prev_chunk(v, first_rows):
        if cps == 1:
            return first_rows
        out = pltpu.roll(v, 1, 0)
        rid = lax.broadcasted_iota(jnp.int32, v.shape, 0)
        for n in range(n_seq):
            out = jnp.where(rid == n * cps, first_rows[n:n + 1, :], out)
        return out

    @pl.when(s == 0)
    def _():
        vm1[...] = prev_chunk(hb_l1[...].astype(F32) * cb_l1[...].astype(F32), buf1[...])
        vm2[...] = prev_chunk(hb_l2[...].astype(F32) * cb_l2[...].astype(F32), buf0[...])

    v0 = hb[...].astype(F32) * cb[...].astype(F32)
    v1 = vm1[...]
    conv = vm2[...] * w_ref[0:1, :] + v1 * w_ref[1:2, :] + v0 * w_ref[2:3, :]
    bin_ref[...] = (bb[...].astype(F32) * conv * szb[...].astype(F32)).astype(BF16)
    vm2[...] = v1
    vm1[...] = v0
    v_ref[0] = v0


def _conv_branch(proj, buf0, buf1, conv_w, t_len, m, n_seq, tn=512):
    nb = WIDTH // tn
    kern = functools.partial(_conv_kernel, m=m, n_seq=n_seq)
    cur = lambda seg: pl.BlockSpec((m, tn), lambda j, s: (s, seg * nb + j))
    last = lambda seg, back: pl.BlockSpec((m, tn), lambda j, s: (t_len - back, seg * nb + j))
    return pl.pallas_call(
        kern,
        grid=(nb, t_len),
        in_specs=[cur(2), cur(3), last(2, 1), last(3, 1), last(2, 2), last(3, 2), cur(4), cur(5),
                  pl.BlockSpec((n_seq, tn), lambda j, s: (0, j)),
                  pl.BlockSpec((n_seq, tn), lambda j, s: (0, j)),
                  pl.BlockSpec((3, tn), lambda j, s: (0, j))],
        out_specs=[pl.BlockSpec((m, tn), lambda j, s: (s, j)),
                   pl.BlockSpec((1, m, tn), lambda j, s: (jnp.maximum(s - (t_len - 2), 0), 0, j))],
        out_shape=[jax.ShapeDtypeStruct((t_len * m, WIDTH), BF16),
                   jax.ShapeDtypeStruct((2, m, WIDTH), F32)],
        scratch_shapes=[pltpu.VMEM((m, tn), F32), pltpu.VMEM((m, tn), F32)],
        compiler_params=_cparams(2),
        name="conv_branch",
    )(proj, proj, proj, proj, proj, proj, proj, proj, buf0, buf1, conv_w)


def _glu_kernel(ya_ref, sza_ref, w_ref, b_ref, o_ref):
    ya = ya_ref[...]
    glu = jnp.dot(ya, w_ref[...], preferred_element_type=F32) + b_ref[...]
    o_ref[...] = (ya.astype(F32) * _sigmoid(glu) * sza_ref[...].astype(F32)).astype(BF16)


def _glu(ya, proj, w_glu_bf, b_glu, tm):
    rows = ya.shape[0]
    return pl.pallas_call(
        _glu_kernel,
        grid=(rows // tm,),
        in_specs=[pl.BlockSpec((tm, WIDTH), lambda i: (i, 0)),
                  pl.BlockSpec((tm, WIDTH), lambda i: (i, 1)),
                  pl.BlockSpec((WIDTH, WIDTH), lambda i: (0, 0)),
                  pl.BlockSpec((1, WIDTH), lambda i: (0, 0))],
        out_specs=pl.BlockSpec((tm, WIDTH), lambda i: (i, 0)),
        out_shape=jax.ShapeDtypeStruct((rows, WIDTH), BF16),
        compiler_params=_cparams(1),
        name="glu_gate",
    )(ya, proj, w_glu_bf, b_glu.reshape(1, WIDTH))


PERM_ROWS = 512


def _to_natural(pm, t_len):
    rows, n = pm.shape
    c = rows // t_len
    cg = PERM_ROWS // t_len
    r_nat = lax.broadcasted_iota(jnp.int32, (PERM_ROWS, PERM_ROWS), 0)
    r_in = lax.broadcasted_iota(jnp.int32, (PERM_ROWS, PERM_ROWS), 1)
    src = (r_nat & (t_len - 1)) * cg + lax.shift_right_logical(r_nat, t_len.bit_length() - 1)
    perm = jnp.where(r_in == src, 1.0, 0.0).astype(BF16)
    pm3 = pm.reshape(t_len, c, n)
    out = []
    for g in range(c // cg):
        grp = pm3[:, g * cg:(g + 1) * cg, :].reshape(PERM_ROWS, n).astype(BF16)
        out.append(jnp.dot(perm, grp, preferred_element_type=F32).astype(BF16))
    return out[0] if len(out) == 1 else jnp.concatenate(out, axis=0)


def _gated_merge(a_ref, b_ref, wa, wb, sga_ref, sgb_ref, o_ref):
    t_len, c, _ = a_ref.shape
    rows = t_len * c
    pa = jnp.dot(a_ref[...].reshape(rows, WIDTH), wa, preferred_element_type=F32)
    pb = jnp.dot(b_ref[...].reshape(rows, WIDTH), wb, preferred_element_type=F32)
    tn = pa.shape[1]
    pm = sga_ref[...].reshape(rows, tn) * pa + sgb_ref[...].reshape(rows, tn) * pb
    o_ref[...] = _to_natural(pm, t_len)


def _merge_kernel(a_ref, b_ref, wa_ref, wb_ref, sga_ref, sgb_ref, o_ref):
    _gated_merge(a_ref, b_ref, wa_ref[...], wb_ref[...], sga_ref, sgb_ref, o_ref)


def _merge_cast_kernel(a_ref, b_ref, wa_ref, wb_ref, sga_ref, sgb_ref, o_ref, wabf_ref, wbbf_ref):
    wa = wa_ref[...].astype(BF16)
    wb = wb_ref[...].astype(BF16)
    wabf_ref[...] = wa
    wbbf_ref[...] = wb
    _gated_merge(a_ref, b_ref, wa, wb, sga_ref, sgb_ref, o_ref)


def _merge(a_in3, b_in3, proj3, w_pa, w_pb, c, tn=512):
    t_len, m, _ = a_in3.shape
    nb = WIDTH // tn
    cast = w_pa.dtype != BF16
    assert not cast or c == m
    assert PERM_ROWS % t_len == 0 and c % (PERM_ROWS // t_len) == 0
    w_spec = pl.BlockSpec((WIDTH, tn), lambda i, j: (0, j))
    out_spec = pl.BlockSpec((t_len * c, tn), lambda i, j: (i, j))
    out_shape = jax.ShapeDtypeStruct((t_len * m, D_MODEL), BF16)
    w_shape = jax.ShapeDtypeStruct((WIDTH, D_MODEL), BF16)
    return pl.pallas_call(
        _merge_cast_kernel if cast else _merge_kernel,
        grid=(m // c, D_MODEL // tn),
        in_specs=[pl.BlockSpec((t_len, c, WIDTH), lambda i, j: (0, i, 0)),
                  pl.BlockSpec((t_len, c, WIDTH), lambda i, j: (0, i, 0)),
                  w_spec, w_spec,
                  pl.BlockSpec((t_len, c, tn), lambda i, j: (0, i, 6 * nb + j)),
                  pl.BlockSpec((t_len, c, tn), lambda i, j: (0, i, 8 * nb + j))],
        out_specs=[out_spec, w_spec, w_spec] if cast else out_spec,
        out_shape=[out_shape, w_shape, w_shape] if cast else out_shape,
        compiler_params=_cparams(2),
        name="merge_cast" if cast else "merge",
    )(a_in3, b_in3, w_pa, w_pb, proj3, proj3)


def _outproj_kernel(m_ref, w_ref, x_ref, gate_ref, fg_ref, o_ref, hres, *, tn, tiles_per_seq):
    j = pl.program_id(1)
    part = jnp.dot(m_ref[...], w_ref[...], preferred_element_type=F32)
    if tiles_per_seq:
        gate = gate_ref[pl.ds(pl.program_id(0) // tiles_per_seq, 1), :]
    else:
        gate = gate_ref[...]
    hres[:, pl.ds(pl.multiple_of(j * tn, tn), tn)] = x_ref[...] + gate * part

    @pl.when(j == pl.num_programs(1) - 1)
    def _():
        for r in range(0, hres.shape[0], LANES):
            h = hres[r:r + LANES, :]
            o_ref[r:r + LANES, :] = h * lax.rsqrt(jnp.mean(h * h, axis=-1, keepdims=True) + EPS) * fg_ref[...]


def _outproj(merged, w_o_bf, x2d, gate, final_g, tm, tiles_per_seq, tn=512):
    rows = x2d.shape[0]
    if tiles_per_seq:
        gate_spec = pl.BlockSpec((gate.shape[0], tn), lambda i, j: (0, j))
    else:
        gate_spec = pl.BlockSpec((tm, tn), lambda i, j: (i, j))
    return pl.pallas_call(
        functools.partial(_outproj_kernel, tn=tn, tiles_per_seq=tiles_per_seq),
        grid=(rows // tm, D_MODEL // tn),
        in_specs=[pl.BlockSpec((tm, D_MODEL), lambda i, j: (i, 0)),
                  pl.BlockSpec((D_MODEL, tn), lambda i, j: (0, j)),
                  pl.BlockSpec((tm, tn), lambda i, j: (i, j)),
                  gate_spec,
                  pl.BlockSpec((1, D_MODEL), lambda i, j: (0, 0))],
        out_specs=pl.BlockSpec((tm, D_MODEL), lambda i, j: (i, 0)),
        out_shape=jax.ShapeDtypeStruct((rows, D_MODEL), F32),
        scratch_shapes=[pltpu.VMEM((tm, D_MODEL), F32)],
        compiler_params=_cparams(2),
        name="outproj",
    )(merged, w_o_bf, x2d, gate, final_g.reshape(1, D_MODEL))


def _state_to_blocks(state):
    n = state.shape[0]
    re = state[..., 0].reshape(n, N_BLOCKS, STATE_LANES).transpose(1, 0, 2)
    im = state[..., 1].reshape(n, N_BLOCKS, STATE_LANES).transpose(1, 0, 2)
    return re, im


def _blocks_to_state(re, im):
    n = re.shape[1]
    re = re.transpose(1, 0, 2).reshape(n, N_GROUPS, N_STATE)
    im = im.transpose(1, 0, 2).reshape(n, N_GROUPS, N_STATE)
    return jnp.stack([re, im], axis=-1)


def _group(x3, n_seq, ct, tm, mod, ssm0, conv0, p):
    m, t_len, _ = x3.shape
    cps = m // n_seq
    tiles_per_seq = cps // ct if cps > 1 else 0
    mod_rows = mod
    if not tiles_per_seq:
        mod_rows = tuple(jnp.repeat(v, t_len, axis=0) for v in mod)
    sh_rows, sc_rows, gate_rows = mod_rows
    x2d = x3.reshape(m * t_len, D_MODEL)
    xn = _norm_mod(x2d, t_len, p["norm_g"], sc_rows, sh_rows, ct, tiles_per_seq)
    xn = xn.reshape(t_len * m, D_MODEL)
    if "w_in_bf" in p:
        proj = _inproj(xn, p["w_in_bf"], tm)
    else:
        proj, p["w_in_bf"] = _inproj_cast(xn, p["w_in"])
    proj3 = proj.reshape(t_len, m, IN_COLS)
    h0_re, h0_im = _state_to_blocks(ssm0)
    ya3, hr, hi = _s5(proj3, h0_re, h0_im, p["s5"], t_len, m, n_seq)
    b_in, v_last = _conv_branch(proj, conv0[:, 0], conv0[:, 1], p["conv_w"], t_len, m, n_seq)
    a_in = _glu(ya3.reshape(t_len * m, WIDTH), proj, p["w_glu"], p["b_glu"], 512)
    a_in3 = a_in.reshape(t_len, m, WIDTH)
    b_in3 = b_in.reshape(t_len, m, WIDTH)
    if "w_pa_bf" in p:
        merged = _merge(a_in3, b_in3, proj3, p["w_pa_bf"], p["w_pb_bf"], tm // t_len)
    else:
        merged, p["w_pa_bf"], p["w_pb_bf"] = _merge(a_in3, b_in3, proj3, p["w_pa"], p["w_pb"], m)
    out = _outproj(merged, p["w_o"], x2d, gate_rows, p["final_g"], ct * t_len, tiles_per_seq)
    conv_new = v_last[:, cps - 1::cps, :].transpose(1, 0, 2)
    return out.reshape(m, t_len, D_MODEL), _blocks_to_state(hr, hi), conv_new


def kernel(x_prompt, x_sample, state_ssm, state_conv, c_prompt, c_sample, norm_g, w_ada, b_ada, w_in, lam_re, lam_im, log_dt, b_re, b_im, c_re, c_im, d_skip, w_glu, b_glu, w_pa, conv_w, w_pb, w_o, final_g):
    depth = norm_g.shape[0]
    assert depth == 1
    n_p, seq, _ = x_prompt.shape
    n_s, dec, _ = x_sample.shape
    t_p = 16
    assert seq % t_p == 0

    l = 0
    p = dict(norm_g=norm_g[l], w_in=w_in[l], conv_w=conv_w[l],
             w_glu=w_glu[l].astype(BF16), b_glu=b_glu[l], w_pa=w_pa[l],
             w_pb=w_pb[l], w_o=w_o[l].astype(BF16), final_g=final_g,
             s5=_s5_params(lam_re[l], lam_im[l], log_dt[l], b_re[l], b_im[l], c_re[l], c_im[l], d_skip[l]))

    n_c = n_p + n_s
    pad = (-n_c) % 16
    c_all = jnp.concatenate([c_prompt, c_sample, jnp.zeros((pad, D_MODEL), F32)], axis=0)
    mod = _ada(c_all, w_ada[l], b_ada[l])
    mod_p = tuple(mod[:n_p, k * D_MODEL:(k + 1) * D_MODEL] for k in range(3))
    mod_s = tuple(mod[n_p:n_c, k * D_MODEL:(k + 1) * D_MODEL] for k in range(3))

    y_sample, ssm_s, conv_s = _group(x_sample, n_s, 64, dec * n_s, mod_s, state_ssm[l], state_conv[l], p)

    ssm_p0 = jnp.zeros((n_p, N_GROUPS, N_STATE, 2), F32)
    conv_p0 = jnp.zeros((n_p, 2, WIDTH), F32)
    y_p, ssm_p, conv_p = _group(x_prompt.reshape(n_p * seq // t_p, t_p, D_MODEL), n_p, 32, 1024,
                                mod_p, ssm_p0, conv_p0, p)
    y_prompt = y_p.reshape(n_p, seq, D_MODEL)

    return (y_prompt, y_sample, ssm_p[None], conv_p[None], ssm_s[None], conv_s[None])
```

```python
import functools

import jax
import jax.numpy as jnp
from jax import lax
from jax.experimental import pallas as pl
from jax.experimental.pallas import tpu as pltpu

F32 = jnp.float32
BF16 = jnp.bfloat16

D_MODEL = 4096
WIDTH = 2048
N_GROUPS = 128
GROUP = 16
N_STATE = 64
GROUPS_PER_BLOCK = 8
N_BLOCKS = N_GROUPS // GROUPS_PER_BLOCK
STATE_LANES = GROUPS_PER_BLOCK * N_STATE
LANES = 128
IN_COLS = 10 * WIDTH
EPS = 1e-6
VMEM_LIMIT = 56 * 1024 * 1024
NT_DIMS = (((1,), (1,)), ((), ()))


def _cparams(n_axes):
    return pltpu.CompilerParams(dimension_semantics=("arbitrary",) * n_axes,
                                vmem_limit_bytes=VMEM_LIMIT)


def _sigmoid(x):
    return jax.nn.sigmoid(x)


def _ada_kernel(c_ref, w_ref, b_ref, o_ref):
    c = c_ref[...]
    a = (c * _sigmoid(c)).astype(BF16)
    o_ref[...] = jnp.dot(a, w_ref[...].astype(BF16), preferred_element_type=F32) + b_ref[...]


def _ada(c_all, w_ada, b_ada, tn=512):
    rows = c_all.shape[0]
    n_out = w_ada.shape[1]
    return pl.pallas_call(
        _ada_kernel,
        grid=(n_out // tn,),
        in_specs=[pl.BlockSpec((rows, D_MODEL), lambda j: (0, 0)),
                  pl.BlockSpec((D_MODEL, tn), lambda j: (0, j)),
                  pl.BlockSpec((1, tn), lambda j: (0, j))],
        out_specs=pl.BlockSpec((rows, tn), lambda j: (0, j)),
        out_shape=jax.ShapeDtypeStruct((rows, n_out), F32),
        compiler_params=_cparams(1),
        name="ada_mod",
    )(c_all, w_ada, b_ada.reshape(1, n_out))


def _norm_kernel(x_ref, g_ref, sc_ref, sh_ref, o_ref, *, t_len, ct, tiles_per_seq):
    if tiles_per_seq:
        n = pl.program_id(0) // tiles_per_seq
        sc = sc_ref[pl.ds(n, 1), :]
        sh = sh_ref[pl.ds(n, 1), :]
    else:
        sc = sc_ref[...]
        sh = sh_ref[...]
    x = x_ref[...]
    y = x * lax.rsqrt(jnp.mean(x * x, axis=-1, keepdims=True) + EPS) * g_ref[...]
    y = (y * (1.0 + sc) + sh).astype(BF16)
    r = t_len * ct
    r_out = lax.broadcasted_iota(jnp.int32, (r, r), 0)
    r_in = lax.broadcasted_iota(jnp.int32, (r, r), 1)
    src_row = (r_out & (ct - 1)) * t_len + lax.shift_right_logical(r_out, ct.bit_length() - 1)
    perm = jnp.where(r_in == src_row, 1.0, 0.0).astype(BF16)
    yp = jnp.dot(perm, y, preferred_element_type=F32)
    o_ref[...] = yp.reshape(t_len, ct, D_MODEL).astype(BF16)


def _norm_mod(x2d, t_len, norm_g, scale, shift, ct, tiles_per_seq):
    assert ct & (ct - 1) == 0
    r = t_len * ct
    m = x2d.shape[0] // t_len
    if tiles_per_seq:
        mod_spec = pl.BlockSpec((scale.shape[0], D_MODEL), lambda i: (0, 0))
    else:
        mod_spec = pl.BlockSpec((r, D_MODEL), lambda i: (i, 0))
    return pl.pallas_call(
        functools.partial(_norm_kernel, t_len=t_len, ct=ct, tiles_per_seq=tiles_per_seq),
        grid=(m // ct,),
        in_specs=[pl.BlockSpec((r, D_MODEL), lambda i: (i, 0)),
                  pl.BlockSpec((1, D_MODEL), lambda i: (0, 0)),
                  mod_spec, mod_spec],
        out_specs=pl.BlockSpec((t_len, ct, D_MODEL), lambda i: (0, i, 0)),
        out_shape=jax.ShapeDtypeStruct((t_len, m, D_MODEL), BF16),
        compiler_params=_cparams(1),
        name="norm_mod",
    )(x2d, norm_g.reshape(1, D_MODEL), scale, shift)


def _activate(acc, seg):
    is_silu = jnp.logical_or(seg == 1, seg == 5)
    is_sig = seg >= 6
    sig = _sigmoid(acc)
    return jnp.where(is_sig, sig, jnp.where(is_silu, acc * sig, acc)).astype(BF16)


def _inproj_kernel(x_ref, w_ref, o_ref, *, tn):
    acc = jnp.dot(x_ref[...], w_ref[...], preferred_element_type=F32)
    o_ref[...] = _activate(acc, pl.program_id(1) // (WIDTH // tn))


def _inproj_cast_kernel(x_ref, w_ref, o_ref, wbf_ref, *, tn):
    w = w_ref[...].astype(BF16)
    wbf_ref[...] = w
    acc = jnp.dot(x_ref[...], w, preferred_element_type=F32)
    o_ref[...] = _activate(acc, pl.program_id(0) // (WIDTH // tn))


def _inproj_cast(xn, w_in, tn=512):
    rows = xn.shape[0]
    return pl.pallas_call(
        functools.partial(_inproj_cast_kernel, tn=tn),
        grid=(IN_COLS // tn,),
        in_specs=[pl.BlockSpec((rows, D_MODEL), lambda j: (0, 0)),
                  pl.BlockSpec((D_MODEL, tn), lambda j: (0, j))],
        out_specs=[pl.BlockSpec((rows, tn), lambda j: (0, j)),
                   pl.BlockSpec((D_MODEL, tn), lambda j: (0, j))],
        out_shape=[jax.ShapeDtypeStruct((rows, IN_COLS), BF16),
                   jax.ShapeDtypeStruct((D_MODEL, IN_COLS), BF16)],
        compiler_params=_cparams(1),
        name="inproj_cast",
    )(xn, w_in)


def _inproj(xn, w_in_bf, tm, tn=1024):
    rows = xn.shape[0]
    return pl.pallas_call(
        functools.partial(_inproj_kernel, tn=tn),
        grid=(rows // tm, IN_COLS // tn),
        in_specs=[pl.BlockSpec((tm, D_MODEL), lambda i, j: (i, 0)),
                  pl.BlockSpec((D_MODEL, tn), lambda i, j: (0, j))],
        out_specs=pl.BlockSpec((tm, tn), lambda i, j: (i, j)),
        out_shape=jax.ShapeDtypeStruct((rows, IN_COLS), BF16),
        compiler_params=_cparams(2),
        name="inproj",
    )(xn, w_in_bf)


def _cmul(ar, ai, br, bi):
    return ar * br - ai * bi, ar * bi + ai * br


def _short_conv(hb_ref, cb_ref, bb_ref, szb_ref, buf0_ref, buf1_ref, w_ref, bin_ref, v_ref, *, t_len, m, n_seq):
    cps = m // n_seq

    def prev_chunk(v, first_rows):
        if cps == 1:
            return first_rows
        out = pltpu.roll(v, 1, 0)
        rid = lax.broadcasted_iota(jnp.int32, v.shape, 0)
        for n in range(n_seq):
            out = jnp.where(rid == n * cps, first_rows[n:n + 1, :], out)
        return out

    def v_of(s):
        return hb_ref[s].astype(F32) * cb_ref[s].astype(F32)

    vm2 = prev_chunk(v_of(t_len - 2), buf0_ref[...])
    vm1 = prev_chunk(v_of(t_len - 1), buf1_ref[...])
    for s in range(t_len):
        v0 = v_of(s)
        conv = vm2 * w_ref[0:1, :] + vm1 * w_ref[1:2, :] + v0 * w_ref[2:3, :]
        bin_ref[s] = (bb_ref[s].astype(F32) * conv * szb_ref[s].astype(F32)).astype(BF16)
        if s >= t_len - 2:
            v_ref[s - (t_len - 2)] = v0
        vm2, vm1 = vm1, v0


def _s5_kernel(u_ref, h0r_ref, h0i_ref, lr_ref, li_ref, ldt_ref, btr_ref, bti_ref, ctr_ref, cti_ref, d_ref,
               hb_ref, cb_ref, bb_ref, szb_ref, buf0_ref, buf1_ref, cw_ref,
               y_ref, hr_out, hi_out, bin_ref, v_ref,
               ucat, bend_r, bend_i, cend_r, cend_i, kpair, hl_r, hl_i, hp_r, hp_i, *, t_len, m, n_seq):
    blk = pl.program_id(0)
    _short_conv(hb_ref, cb_ref, bb_ref, szb_ref, buf0_ref, buf1_ref, cw_ref, bin_ref, v_ref,
                t_len=t_len, m=m, n_seq=n_seq)

    @pl.when(blk == 0)
    def _():
        for ref in (bend_r, bend_i, cend_r, cend_i, kpair):
            ref[...] = jnp.zeros(ref.shape, ref.dtype)

    lam_r = lr_ref[0]
    lam_i = li_ref[0]
    dt = jnp.exp(ldt_ref[0])
    mag = jnp.exp(lam_r * dt)
    lb_r = mag * jnp.cos(lam_i * dt)
    lb_i = mag * jnp.sin(lam_i * dt)
    den = lam_r * lam_r + lam_i * lam_i
    nr = lb_r - 1.0
    co_r = (nr * lam_r + lb_i * lam_i) / den
    co_i = (lb_i * lam_r - nr * lam_i) / den
    bb_r, bb_i = _cmul(co_r, co_i, btr_ref[0], bti_ref[0])
    c_r = ctr_ref[0]
    c_i = cti_ref[0]
    pw = [(jnp.ones_like(lb_r), jnp.zeros_like(lb_r))]
    for _ in range(t_len):
        pw.append(_cmul(pw[-1][0], pw[-1][1], lb_r, lb_i))
    ar, ai = pw[t_len]

    lane = lax.broadcasted_iota(jnp.int32, (GROUP, LANES), 1)
    lo = lane < N_STATE

    def pair_tile(x, q):
        slab = x[:, q * LANES:(q + 1) * LANES]
        return jnp.concatenate([jnp.where(lo, slab, 0.0), jnp.where(lo, 0.0, slab)], axis=0).astype(BF16)

    grp_row = lax.broadcasted_iota(jnp.int32, (LANES, STATE_LANES), 0) // GROUP
    grp_lane = lax.broadcasted_iota(jnp.int32, (LANES, STATE_LANES), 1) // N_STATE
    same_group = grp_row == grp_lane

    def expand(x):
        return jnp.where(same_group, jnp.concatenate([x] * GROUPS_PER_BLOCK, axis=0), 0.0).astype(BF16)

    bbx_r = expand(bb_r)
    bbx_i = expand(bb_i)
    for s in range(t_len):
        ucat[:, s * LANES:(s + 1) * LANES] = u_ref[s]
        be_r, be_i = _cmul(pw[t_len - 1 - s][0], pw[t_len - 1 - s][1], bb_r, bb_i)
        ce_r, ce_i = _cmul(pw[s + 1][0], pw[s + 1][1], c_r, c_i)
        for q in range(4):
            r0 = s * LANES + q * 2 * GROUP
            rows = slice(r0, r0 + 2 * GROUP)
            cols = slice(q * LANES, (q + 1) * LANES)
            bend_r[rows, cols] = pair_tile(be_r, q)
            bend_i[rows, cols] = pair_tile(be_i, q)
            cend_r[rows, cols] = pair_tile(ce_r, q)
            cend_i[rows, cols] = pair_tile(-ce_i, q)
        cl_r, cl_i = _cmul(pw[s][0], pw[s][1], c_r, c_i)
        k_lag = (lax.dot_general(bbx_r, expand(cl_r), NT_DIMS, preferred_element_type=F32)
                 - lax.dot_general(bbx_i, expand(cl_i), NT_DIMS, preferred_element_type=F32))
        k_lag = k_lag.astype(BF16)
        kpair[(t_len - 1 - s) * LANES:(t_len - s) * LANES, LANES:] = k_lag
        if s <= t_len - 2:
            kpair[(t_len - 2 - s) * LANES:(t_len - 1 - s) * LANES, :LANES] = k_lag

    uc = ucat[...]
    hl_r[...] = jnp.dot(uc, bend_r[...], preferred_element_type=F32)
    hl_i[...] = jnp.dot(uc, bend_i[...], preferred_element_type=F32)

    cps = m // n_seq
    if cps == 1:
        h_r = h0r_ref[0]
        h_i = h0i_ref[0]
        hp_r[...] = h_r
        hp_i[...] = h_i
        hr_out[0] = ar * h_r - ai * h_i + hl_r[...]
        hi_out[0] = ar * h_i + ai * h_r + hl_i[...]
    else:
        def body(c, carry):
            new = []
            for n in range(n_seq):
                h_r, h_i = carry[n]
                row = pl.ds(n * cps + c, 1)
                hp_r[row, :] = h_r
                hp_i[row, :] = h_i
                new.append((ar * h_r - ai * h_i + hl_r[row, :], ar * h_i + ai * h_r + hl_i[row, :]))
            return tuple(new)

        init = tuple((h0r_ref[0, n:n + 1, :], h0i_ref[0, n:n + 1, :]) for n in range(n_seq))
        fin = lax.fori_loop(0, cps, body, init)
        for n in range(n_seq):
            hr_out[0, n:n + 1, :] = fin[n][0]
            hi_out[0, n:n + 1, :] = fin[n][1]

    ycar = (lax.dot_general(hp_r[...].astype(BF16), cend_r[...], NT_DIMS, preferred_element_type=F32)
            + lax.dot_general(hp_i[...].astype(BF16), cend_i[...], NT_DIMS, preferred_element_type=F32))
    dvec = d_ref[0]
    for t in range(0, t_len, 2):
        y2 = jnp.dot(ucat[:, :(t + 2) * LANES], kpair[(t_len - 2 - t) * LANES:, :],
                     preferred_element_type=F32) + ycar[:, t * LANES:(t + 2) * LANES]
        for k in range(2):
            y = y2[:, k * LANES:(k + 1) * LANES] + dvec * u_ref[t + k].astype(F32)
            y_ref[t + k] = jax.nn.gelu(y).astype(BF16)


def _s5_params(lam_re, lam_im, log_dt, b_re, b_im, c_re, c_im, d_skip):
    row = lambda x: x.astype(F32).reshape(N_BLOCKS, 1, STATE_LANES)
    ldt = jnp.broadcast_to(log_dt.astype(F32)[:, None], (N_GROUPS, N_STATE))
    bt = lambda x: (x.astype(F32).reshape(N_BLOCKS, GROUPS_PER_BLOCK, N_STATE, GROUP)
                    .transpose(0, 3, 1, 2).reshape(N_BLOCKS, GROUP, STATE_LANES))
    ct = lambda x: (x.astype(F32).reshape(N_BLOCKS, GROUPS_PER_BLOCK, GROUP, N_STATE)
                    .transpose(0, 2, 1, 3).reshape(N_BLOCKS, GROUP, STATE_LANES))
    return (row(lam_re), row(lam_im), row(ldt), bt(b_re), bt(b_im), ct(c_re), ct(c_im),
            d_skip.astype(F32).reshape(N_BLOCKS, 1, LANES))


def _s5(proj3, h0_re, h0_im, s5p, buf0, buf1, conv_w, t_len, m, n_seq):
    kern = functools.partial(_s5_kernel, t_len=t_len, m=m, n_seq=n_seq)
    blk3 = lambda b: (b, 0, 0)
    row_spec = pl.BlockSpec((1, 1, STATE_LANES), blk3)
    mat_spec = pl.BlockSpec((1, GROUP, STATE_LANES), blk3)
    state_spec = pl.BlockSpec((1, n_seq, STATE_LANES), blk3)
    end_mat = pltpu.VMEM((t_len * LANES, STATE_LANES), BF16)
    chunk_state = pltpu.VMEM((m, STATE_LANES), F32)
    nbw = WIDTH // LANES
    seg_spec = lambda seg: pl.BlockSpec((t_len, m, LANES), lambda b: (0, 0, seg * nbw + b))
    return pl.pallas_call(
        kern,
        grid=(N_BLOCKS,),
        in_specs=[seg_spec(0),
                  state_spec, state_spec,
                  row_spec, row_spec, row_spec,
                  mat_spec, mat_spec, mat_spec, mat_spec,
                  pl.BlockSpec((1, 1, LANES), blk3),
                  seg_spec(2), seg_spec(3), seg_spec(4), seg_spec(5),
                  pl.BlockSpec((n_seq, LANES), lambda b: (0, b)),
                  pl.BlockSpec((n_seq, LANES), lambda b: (0, b)),
                  pl.BlockSpec((3, LANES), lambda b: (0, b))],
        out_specs=[seg_spec(0), state_spec, state_spec, seg_spec(0),
                   pl.BlockSpec((2, m, LANES), lambda b: (0, 0, b))],
        out_shape=[jax.ShapeDtypeStruct((t_len, m, WIDTH), BF16),
                   jax.ShapeDtypeStruct((N_BLOCKS, n_seq, STATE_LANES), F32),
                   jax.ShapeDtypeStruct((N_BLOCKS, n_seq, STATE_LANES), F32),
                   jax.ShapeDtypeStruct((t_len, m, WIDTH), BF16),
                   jax.ShapeDtypeStruct((2, m, WIDTH), F32)],
        scratch_shapes=[pltpu.VMEM((m, t_len * LANES), BF16),
                        end_mat, end_mat, end_mat, end_mat,
                        pltpu.VMEM((t_len * LANES, 2 * LANES), BF16),
                        chunk_state, chunk_state, chunk_state, chunk_state],
        compiler_params=_cparams(1),
        name="s5_scan",
    )(proj3, h0_re, h0_im, *s5p, proj3, proj3, proj3, proj3, buf0, buf1, conv_w)


def _glu_kernel(ya_ref, sza_ref, w_ref, b_ref, o_ref):
    ya = ya_ref[...]
    glu = jnp.dot(ya, w_ref[...], preferred_element_type=F32) + b_ref[...]
    o_ref[...] = (ya.astype(F32) * _sigmoid(glu) * sza_ref[...].astype(F32)).astype(BF16)


def _glu(ya, proj, w_glu_bf, b_glu, tm):
    rows = ya.shape[0]
    return pl.pallas_call(
        _glu_kernel,
        grid=(rows // tm,),
        in_specs=[pl.BlockSpec((tm, WIDTH), lambda i: (i, 0)),
                  pl.BlockSpec((tm, WIDTH), lambda i: (i, 1)),
                  pl.BlockSpec((WIDTH, WIDTH), lambda i: (0, 0)),
                  pl.BlockSpec((1, WIDTH), lambda i: (0, 0))],
        out_specs=pl.BlockSpec((tm, WIDTH), lambda i: (i, 0)),
        out_shape=jax.ShapeDtypeStruct((rows, WIDTH), BF16),
        compiler_params=_cparams(1),
        name="glu_gate",
    )(ya, proj, w_glu_bf, b_glu.reshape(1, WIDTH))


PERM_ROWS = 512


def _to_natural(pm, t_len):
    rows, n = pm.shape
    c = rows // t_len
    cg = PERM_ROWS // t_len
    r_nat = lax.broadcasted_iota(jnp.int32, (PERM_ROWS, PERM_ROWS), 0)
    r_in = lax.broadcasted_iota(jnp.int32, (PERM_ROWS, PERM_ROWS), 1)
    src = (r_nat & (t_len - 1)) * cg + lax.shift_right_logical(r_nat, t_len.bit_length() - 1)
    perm = jnp.where(r_in == src, 1.0, 0.0).astype(BF16)
    pm3 = pm.reshape(t_len, c, n)
    out = []
    for g in range(c // cg):
        grp = pm3[:, g * cg:(g + 1) * cg, :].reshape(PERM_ROWS, n).astype(BF16)
        out.append(jnp.dot(perm, grp, preferred_element_type=F32).astype(BF16))
    return out[0] if len(out) == 1 else jnp.concatenate(out, axis=0)


def _gated_merge(a_ref, b_ref, wa, wb, sga_ref, sgb_ref, o_ref):
    t_len, c, _ = a_ref.shape
    rows = t_len * c
    pa = jnp.dot(a_ref[...].reshape(rows, WIDTH), wa, preferred_element_type=F32)
    pb = jnp.dot(b_ref[...].reshape(rows, WIDTH), wb, preferred_element_type=F32)
    tn = pa.shape[1]
    pm = sga_ref[...].reshape(rows, tn) * pa + sgb_ref[...].reshape(rows, tn) * pb
    o_ref[...] = _to_natural(pm, t_len)


def _merge_kernel(a_ref, b_ref, wa_ref, wb_ref, sga_ref, sgb_ref, o_ref):
    _gated_merge(a_ref, b_ref, wa_ref[...], wb_ref[...], sga_ref, sgb_ref, o_ref)


def _merge_cast_kernel(a_ref, b_ref, wa_ref, wb_ref, sga_ref, sgb_ref, o_ref, wabf_ref, wbbf_ref):
    wa = wa_ref[...].astype(BF16)
    wb = wb_ref[...].astype(BF16)
    wabf_ref[...] = wa
    wbbf_ref[...] = wb
    _gated_merge(a_ref, b_ref, wa, wb, sga_ref, sgb_ref, o_ref)


def _merge(a_in3, b_in3, proj3, w_pa, w_pb, c, tn=512):
    t_len, m, _ = a_in3.shape
    nb = WIDTH // tn
    cast = w_pa.dtype != BF16
    assert not cast or c == m
    assert PERM_ROWS % t_len == 0 and c % (PERM_ROWS // t_len) == 0
    w_spec = pl.BlockSpec((WIDTH, tn), lambda i, j: (0, j))
    out_spec = pl.BlockSpec((t_len * c, tn), lambda i, j: (i, j))
    out_shape = jax.ShapeDtypeStruct((t_len * m, D_MODEL), BF16)
    w_shape = jax.ShapeDtypeStruct((WIDTH, D_MODEL), BF16)
    return pl.pallas_call(
        _merge_cast_kernel if cast else _merge_kernel,
        grid=(m // c, D_MODEL // tn),
        in_specs=[pl.BlockSpec((t_len, c, WIDTH), lambda i, j: (0, i, 0)),
                  pl.BlockSpec((t_len, c, WIDTH), lambda i, j: (0, i, 0)),
                  w_spec, w_spec,
                  pl.BlockSpec((t_len, c, tn), lambda i, j: (0, i, 6 * nb + j)),
                  pl.BlockSpec((t_len, c, tn), lambda i, j: (0, i, 8 * nb + j))],
        out_specs=[out_spec, w_spec, w_spec] if cast else out_spec,
        out_shape=[out_shape, w_shape, w_shape] if cast else out_shape,
        compiler_params=_cparams(2),
        name="merge_cast" if cast else "merge",
    )(a_in3, b_in3, w_pa, w_pb, proj3, proj3)


def _outproj_kernel(m_ref, w_ref, x_ref, gate_ref, fg_ref, o_ref, hres, *, tn, tiles_per_seq):
    j = pl.program_id(1)
    part = jnp.dot(m_ref[...], w_ref[...], preferred_element_type=F32)
    if tiles_per_seq:
        gate = gate_ref[pl.ds(pl.program_id(0) // tiles_per_seq, 1), :]
    else:
        gate = gate_ref[...]
    hres[:, pl.ds(pl.multiple_of(j * tn, tn), tn)] = x_ref[...] + gate * part

    @pl.when(j == pl.num_programs(1) - 1)
    def _():
        for r in range(0, hres.shape[0], LANES):
            h = hres[r:r + LANES, :]
            o_ref[r:r + LANES, :] = h * lax.rsqrt(jnp.mean(h * h, axis=-1, keepdims=True) + EPS) * fg_ref[...]


def _outproj(merged, w_o_bf, x2d, gate, final_g, tm, tiles_per_seq, tn=512):
    rows = x2d.shape[0]
    if tiles_per_seq:
        gate_spec = pl.BlockSpec((gate.shape[0], tn), lambda i, j: (0, j))
    else:
        gate_spec = pl.BlockSpec((tm, tn), lambda i, j: (i, j))
    return pl.pallas_call(
        functools.partial(_outproj_kernel, tn=tn, tiles_per_seq=tiles_per_seq),
        grid=(rows // tm, D_MODEL // tn),
        in_specs=[pl.BlockSpec((tm, D_MODEL), lambda i, j: (i, 0)),
                  pl.BlockSpec((D_MODEL, tn), lambda i, j: (0, j)),
                  pl.BlockSpec((tm, tn), lambda i, j: (i, j)),
                  gate_spec,
                  pl.BlockSpec((1, D_MODEL), lambda i, j: (0, 0))],
        out_specs=pl.BlockSpec((tm, D_MODEL), lambda i, j: (i, 0)),
        out_shape=jax.ShapeDtypeStruct((rows, D_MODEL), F32),
        scratch_shapes=[pltpu.VMEM((tm, D_MODEL), F32)],
        compiler_params=_cparams(2),
        name="outproj",
    )(merged, w_o_bf, x2d, gate, final_g.reshape(1, D_MODEL))


def _state_to_blocks(state):
    n = state.shape[0]
    re = state[..., 0].reshape(n, N_BLOCKS, STATE_LANES).transpose(1, 0, 2)
    im = state[..., 1].reshape(n, N_BLOCKS, STATE_LANES).transpose(1, 0, 2)
    return re, im


def _blocks_to_state(re, im):
    n = re.shape[1]
    re = re.transpose(1, 0, 2).reshape(n, N_GROUPS, N_STATE)
    im = im.transpose(1, 0, 2).reshape(n, N_GROUPS, N_STATE)
    return jnp.stack([re, im], axis=-1)


def _group(x3, n_seq, ct, tm, mod, ssm0, conv0, p):
    m, t_len, _ = x3.shape
    cps = m // n_seq
    tiles_per_seq = cps // ct if cps > 1 else 0
    mod_rows = mod
    if not tiles_per_seq:
        mod_rows = tuple(jnp.repeat(v, t_len, axis=0) for v in mod)
    sh_rows, sc_rows, gate_rows = mod_rows
    x2d = x3.reshape(m * t_len, D_MODEL)
    xn = _norm_mod(x2d, t_len, p["norm_g"], sc_rows, sh_rows, ct, tiles_per_seq)
    xn = xn.reshape(t_len * m, D_MODEL)
    if "w_in_bf" in p:
        proj = _inproj(xn, p["w_in_bf"], tm)
    else:
        proj, p["w_in_bf"] = _inproj_cast(xn, p["w_in"])
    proj3 = proj.reshape(t_len, m, IN_COLS)
    h0_re, h0_im = _state_to_blocks(ssm0)
    ya3, hr, hi, b_in3, v_last = _s5(proj3, h0_re, h0_im, p["s5"], conv0[:, 0], conv0[:, 1], p["conv_w"],
                                     t_len, m, n_seq)
    a_in = _glu(ya3.reshape(t_len * m, WIDTH), proj, p["w_glu"], p["b_glu"], 512)
    a_in3 = a_in.reshape(t_len, m, WIDTH)
    if "w_pa_bf" in p:
        merged = _merge(a_in3, b_in3, proj3, p["w_pa_bf"], p["w_pb_bf"], tm // t_len)
    else:
        merged, p["w_pa_bf"], p["w_pb_bf"] = _merge(a_in3, b_in3, proj3, p["w_pa"], p["w_pb"], m)
    out = _outproj(merged, p["w_o"], x2d, gate_rows, p["final_g"], ct * t_len, tiles_per_seq)
    conv_new = v_last[:, cps - 1::cps, :].transpose(1, 0, 2)
    return out.reshape(m, t_len, D_MODEL), _blocks_to_state(hr, hi), conv_new


def kernel(x_prompt, x_sample, state_ssm, state_conv, c_prompt, c_sample, norm_g, w_ada, b_ada, w_in, lam_re, lam_im, log_dt, b_re, b_im, c_re, c_im, d_skip, w_glu, b_glu, w_pa, conv_w, w_pb, w_o, final_g):
    depth = norm_g.shape[0]
    assert depth == 1
    n_p, seq, _ = x_prompt.shape
    n_s, dec, _ = x_sample.shape
    t_p = 16
    assert seq % t_p == 0

    l = 0
    p = dict(norm_g=norm_g[l], w_in=w_in[l], conv_w=conv_w[l],
             w_glu=w_glu[l].astype(BF16), b_glu=b_glu[l], w_pa=w_pa[l],
             w_pb=w_pb[l], w_o=w_o[l].astype(BF16), final_g=final_g,
             s5=_s5_params(lam_re[l], lam_im[l], log_dt[l], b_re[l], b_im[l], c_re[l], c_im[l], d_skip[l]))

    n_c = n_p + n_s
    pad = (-n_c) % 16
    c_all = jnp.concatenate([c_prompt, c_sample, jnp.zeros((pad, D_MODEL), F32)], axis=0)
    mod = _ada(c_all, w_ada[l], b_ada[l])
    mod_p = tuple(mod[:n_p, k * D_MODEL:(k + 1) * D_MODEL] for k in range(3))
    mod_s = tuple(mod[n_p:n_c, k * D_MODEL:(k + 1) * D_MODEL] for k in range(3))

    y_sample, ssm_s, conv_s = _group(x_sample, n_s, 64, dec * n_s, mod_s, state_ssm[l], state_conv[l], p)

    ssm_p0 = jnp.zeros((n_p, N_GROUPS, N_STATE, 2), F32)
    conv_p0 = jnp.zeros((n_p, 2, WIDTH), F32)
    y_p, ssm_p, conv_p = _group(x_prompt.reshape(n_p * seq // t_p, t_p, D_MODEL), n_p, 32, 1024,
                                mod_p, ssm_p0, conv_p0, p)
    y_prompt = y_p.reshape(n_p, seq, D_MODEL)

    return (y_prompt, y_sample, ssm_p[None], conv_p[None], ssm_s[None], conv_s[None])
```

```python
import functools

import jax
import jax.numpy as jnp
from jax import lax
from jax.experimental import pallas as pl
from jax.experimental.pallas import tpu as pltpu

F32 = jnp.float32
BF16 = jnp.bfloat16

D_MODEL = 4096
WIDTH = 2048
N_GROUPS = 128
GROUP = 16
N_STATE = 64
GROUPS_PER_BLOCK = 8
N_BLOCKS = N_GROUPS // GROUPS_PER_BLOCK
STATE_LANES = GROUPS_PER_BLOCK * N_STATE
LANES = 128
IN_COLS = 10 * WIDTH
EPS = 1e-6
VMEM_LIMIT = 56 * 1024 * 1024
NT_DIMS = (((1,), (1,)), ((), ()))


def _cparams(n_axes):
    return pltpu.CompilerParams(dimension_semantics=("arbitrary",) * n_axes,
                                vmem_limit_bytes=VMEM_LIMIT)


def _sigmoid(x):
    return jax.nn.sigmoid(x)


def _ada_kernel(c_ref, w_ref, b_ref, o_ref):
    c = c_ref[...]
    a = (c * _sigmoid(c)).astype(BF16)
    o_ref[...] = jnp.dot(a, w_ref[...].astype(BF16), preferred_element_type=F32) + b_ref[...]


def _ada(c_all, w_ada, b_ada, tn=512):
    rows = c_all.shape[0]
    n_out = w_ada.shape[1]
    return pl.pallas_call(
        _ada_kernel,
        grid=(n_out // tn,),
        in_specs=[pl.BlockSpec((rows, D_MODEL), lambda j: (0, 0)),
                  pl.BlockSpec((D_MODEL, tn), lambda j: (0, j)),
                  pl.BlockSpec((1, tn), lambda j: (0, j))],
        out_specs=pl.BlockSpec((rows, tn), lambda j: (0, j)),
        out_shape=jax.ShapeDtypeStruct((rows, n_out), F32),
        compiler_params=_cparams(1),
        name="ada_mod",
    )(c_all, w_ada, b_ada.reshape(1, n_out))


def _norm_kernel(x_ref, g_ref, sc_ref, sh_ref, o_ref, *, t_len, ct, tiles_per_seq):
    def norm_mod(x, sc, sh):
        y = x * lax.rsqrt(jnp.mean(x * x, axis=-1, keepdims=True) + EPS) * g_ref[...]
        return (y * (1.0 + sc) + sh).astype(BF16)

    if not tiles_per_seq:
        for s in range(t_len):
            o_ref[s] = norm_mod(x_ref[s * ct:(s + 1) * ct, :], sc_ref[...], sh_ref[...])
        return

    n = pl.program_id(0) // tiles_per_seq
    y = norm_mod(x_ref[...], sc_ref[pl.ds(n, 1), :], sh_ref[pl.ds(n, 1), :])
    r = t_len * ct
    r_out = lax.broadcasted_iota(jnp.int32, (r, r), 0)
    r_in = lax.broadcasted_iota(jnp.int32, (r, r), 1)
    src_row = (r_out & (ct - 1)) * t_len + lax.shift_right_logical(r_out, ct.bit_length() - 1)
    perm = jnp.where(r_in == src_row, 1.0, 0.0).astype(BF16)
    yp = jnp.dot(perm, y, preferred_element_type=F32)
    o_ref[...] = yp.reshape(t_len, ct, D_MODEL).astype(BF16)


def _norm_mod(x2d, t_len, norm_g, scale, shift, ct, tiles_per_seq):
    assert ct & (ct - 1) == 0
    r = t_len * ct
    m = x2d.shape[0] // t_len
    assert tiles_per_seq or ct == m == scale.shape[0]
    mod_spec = pl.BlockSpec((scale.shape[0], D_MODEL), lambda i: (0, 0))
    return pl.pallas_call(
        functools.partial(_norm_kernel, t_len=t_len, ct=ct, tiles_per_seq=tiles_per_seq),
        grid=(m // ct,),
        in_specs=[pl.BlockSpec((r, D_MODEL), lambda i: (i, 0)),
                  pl.BlockSpec((1, D_MODEL), lambda i: (0, 0)),
                  mod_spec, mod_spec],
        out_specs=pl.BlockSpec((t_len, ct, D_MODEL), lambda i: (0, i, 0)),
        out_shape=jax.ShapeDtypeStruct((t_len, m, D_MODEL), BF16),
        compiler_params=_cparams(1),
        name="norm_mod",
    )(x2d, norm_g.reshape(1, D_MODEL), scale, shift)


def _activate(acc, seg):
    is_silu = jnp.logical_or(seg == 1, seg == 5)
    is_sig = seg >= 6
    sig = _sigmoid(acc)
    return jnp.where(is_sig, sig, jnp.where(is_silu, acc * sig, acc)).astype(BF16)


def _inproj_kernel(x_ref, w_ref, o_ref, *, tn):
    acc = jnp.dot(x_ref[...], w_ref[...], preferred_element_type=F32)
    o_ref[...] = _activate(acc, pl.program_id(1) // (WIDTH // tn))


def _inproj_cast_kernel(x_ref, w_ref, o_ref, wbf_ref, *, tn):
    w = w_ref[...].astype(BF16)
    wbf_ref[...] = w
    acc = jnp.dot(x_ref[...], w, preferred_element_type=F32)
    o_ref[...] = _activate(acc, pl.program_id(0) // (WIDTH // tn))


def _inproj_cast(xn, w_in, tn=512):
    rows = xn.shape[0]
    return pl.pallas_call(
        functools.partial(_inproj_cast_kernel, tn=tn),
        grid=(IN_COLS // tn,),
        in_specs=[pl.BlockSpec((rows, D_MODEL), lambda j: (0, 0)),
                  pl.BlockSpec((D_MODEL, tn), lambda j: (0, j))],
        out_specs=[pl.BlockSpec((rows, tn), lambda j: (0, j)),
                   pl.BlockSpec((D_MODEL, tn), lambda j: (0, j))],
        out_shape=[jax.ShapeDtypeStruct((rows, IN_COLS), BF16),
                   jax.ShapeDtypeStruct((D_MODEL, IN_COLS), BF16)],
        compiler_params=_cparams(1),
        name="inproj_cast",
    )(xn, w_in)


def _inproj(xn, w_in_bf, tm, tn=1024):
    rows = xn.shape[0]
    return pl.pallas_call(
        functools.partial(_inproj_kernel, tn=tn),
        grid=(rows // tm, IN_COLS // tn),
        in_specs=[pl.BlockSpec((tm, D_MODEL), lambda i, j: (i, 0)),
                  pl.BlockSpec((D_MODEL, tn), lambda i, j: (0, j))],
        out_specs=pl.BlockSpec((tm, tn), lambda i, j: (i, j)),
        out_shape=jax.ShapeDtypeStruct((rows, IN_COLS), BF16),
        compiler_params=_cparams(2),
        name="inproj",
    )(xn, w_in_bf)


def _cmul(ar, ai, br, bi):
    return ar * br - ai * bi, ar * bi + ai * br


def _short_conv(hb_ref, cb_ref, bb_ref, szb_ref, buf0_ref, buf1_ref, w_ref, bin_ref, v_ref, *, t_len, m, n_seq):
    cps = m // n_seq

    def prev_chunk(v, first_rows):
        if cps == 1:
            return first_rows
        out = pltpu.roll(v, 1, 0)
        rid = lax.broadcasted_iota(jnp.int32, v.shape, 0)
        for n in range(n_seq):
            out = jnp.where(rid == n * cps, first_rows[n:n + 1, :], out)
        return out

    def v_of(s):
        return hb_ref[s].astype(F32) * cb_ref[s].astype(F32)

    vm2 = prev_chunk(v_of(t_len - 2), buf0_ref[...])
    vm1 = prev_chunk(v_of(t_len - 1), buf1_ref[...])
    for s in range(t_len):
        v0 = v_of(s)
        conv = vm2 * w_ref[0:1, :] + vm1 * w_ref[1:2, :] + v0 * w_ref[2:3, :]
        bin_ref[s] = (bb_ref[s].astype(F32) * conv * szb_ref[s].astype(F32)).astype(BF16)
        if s >= t_len - 2:
            v_ref[s - (t_len - 2)] = v0
        vm2, vm1 = vm1, v0


def _s5_kernel(u_ref, h0r_ref, h0i_ref, lr_ref, li_ref, ldt_ref, btr_ref, bti_ref, ctr_ref, cti_ref, d_ref,
               hb_ref, cb_ref, bb_ref, szb_ref, buf0_ref, buf1_ref, cw_ref,
               y_ref, hr_out, hi_out, bin_ref, v_ref,
               ucat, bend_r, bend_i, cend_r, cend_i, kpair, hl_r, hl_i, hp_r, hp_i, *, t_len, m, n_seq):
    blk = pl.program_id(0)
    _short_conv(hb_ref, cb_ref, bb_ref, szb_ref, buf0_ref, buf1_ref, cw_ref, bin_ref, v_ref,
                t_len=t_len, m=m, n_seq=n_seq)

    @pl.when(blk == 0)
    def _():
        for ref in (bend_r, bend_i, cend_r, cend_i, kpair):
            ref[...] = jnp.zeros(ref.shape, ref.dtype)

    lam_r = lr_ref[0]
    lam_i = li_ref[0]
    dt = jnp.exp(ldt_ref[0])
    mag = jnp.exp(lam_r * dt)
    lb_r = mag * jnp.cos(lam_i * dt)
    lb_i = mag * jnp.sin(lam_i * dt)
    den = lam_r * lam_r + lam_i * lam_i
    nr = lb_r - 1.0
    co_r = (nr * lam_r + lb_i * lam_i) / den
    co_i = (lb_i * lam_r - nr * lam_i) / den
    bb_r, bb_i = _cmul(co_r, co_i, btr_ref[0], bti_ref[0])
    c_r = ctr_ref[0]
    c_i = cti_ref[0]
    pw = [(jnp.ones_like(lb_r), jnp.zeros_like(lb_r))]
    for _ in range(t_len):
        pw.append(_cmul(pw[-1][0], pw[-1][1], lb_r, lb_i))
    ar, ai = pw[t_len]

    lane = lax.broadcasted_iota(jnp.int32, (GROUP, LANES), 1)
    lo = lane < N_STATE

    def pair_tile(x, q):
        slab = x[:, q * LANES:(q + 1) * LANES]
        return jnp.concatenate([jnp.where(lo, slab, 0.0), jnp.where(lo, 0.0, slab)], axis=0).astype(BF16)

    grp_row = lax.broadcasted_iota(jnp.int32, (LANES, STATE_LANES), 0) // GROUP
    grp_lane = lax.broadcasted_iota(jnp.int32, (LANES, STATE_LANES), 1) // N_STATE
    same_group = grp_row == grp_lane

    def expand(x):
        return jnp.where(same_group, jnp.concatenate([x] * GROUPS_PER_BLOCK, axis=0), 0.0).astype(BF16)

    bbx_r = expand(bb_r)
    bbx_i = expand(bb_i)
    for s in range(t_len):
        ucat[:, s * LANES:(s + 1) * LANES] = u_ref[s]
        be_r, be_i = _cmul(pw[t_len - 1 - s][0], pw[t_len - 1 - s][1], bb_r, bb_i)
        ce_r, ce_i = _cmul(pw[s + 1][0], pw[s + 1][1], c_r, c_i)
        for q in range(4):
            r0 = s * LANES + q * 2 * GROUP
            rows = slice(r0, r0 + 2 * GROUP)
            cols = slice(q * LANES, (q + 1) * LANES)
            bend_r[rows, cols] = pair_tile(be_r, q)
            bend_i[rows, cols] = pair_tile(be_i, q)
            cend_r[rows, cols] = pair_tile(ce_r, q)
            cend_i[rows, cols] = pair_tile(-ce_i, q)
        cl_r, cl_i = _cmul(pw[s][0], pw[s][1], c_r, c_i)
        k_lag = (lax.dot_general(bbx_r, expand(cl_r), NT_DIMS, preferred_element_type=F32)
                 - lax.dot_general(bbx_i, expand(cl_i), NT_DIMS, preferred_element_type=F32))
        k_lag = k_lag.astype(BF16)
        kpair[(t_len - 1 - s) * LANES:(t_len - s) * LANES, LANES:] = k_lag
        if s <= t_len - 2:
            kpair[(t_len - 2 - s) * LANES:(t_len - 1 - s) * LANES, :LANES] = k_lag

    uc = ucat[...]
    hl_r[...] = jnp.dot(uc, bend_r[...], preferred_element_type=F32)
    hl_i[...] = jnp.dot(uc, bend_i[...], preferred_element_type=F32)

    cps = m // n_seq
    if cps == 1:
        h_r = h0r_ref[0]
        h_i = h0i_ref[0]
        hp_r[...] = h_r
        hp_i[...] = h_i
        hr_out[0] = ar * h_r - ai * h_i + hl_r[...]
        hi_out[0] = ar * h_i + ai * h_r + hl_i[...]
    else:
        def body(c, carry):
            new = []
            for n in range(n_seq):
                h_r, h_i = carry[n]
                row = pl.ds(n * cps + c, 1)
                hp_r[row, :] = h_r
                hp_i[row, :] = h_i
                new.append((ar * h_r - ai * h_i + hl_r[row, :], ar * h_i + ai * h_r + hl_i[row, :]))
            return tuple(new)

        init = tuple((h0r_ref[0, n:n + 1, :], h0i_ref[0, n:n + 1, :]) for n in range(n_seq))
        fin = lax.fori_loop(0, cps, body, init)
        for n in range(n_seq):
            hr_out[0, n:n + 1, :] = fin[n][0]
            hi_out[0, n:n + 1, :] = fin[n][1]

    ycar = (lax.dot_general(hp_r[...].astype(BF16), cend_r[...], NT_DIMS, preferred_element_type=F32)
            + lax.dot_general(hp_i[...].astype(BF16), cend_i[...], NT_DIMS, preferred_element_type=F32))
    dvec = d_ref[0]
    for t in range(0, t_len, 2):
        y2 = jnp.dot(ucat[:, :(t + 2) * LANES], kpair[(t_len - 2 - t) * LANES:, :],
                     preferred_element_type=F32) + ycar[:, t * LANES:(t + 2) * LANES]
        for k in range(2):
            y = y2[:, k * LANES:(k + 1) * LANES] + dvec * u_ref[t + k].astype(F32)
            y_ref[t + k] = jax.nn.gelu(y).astype(BF16)


def _s5_params(lam_re, lam_im, log_dt, b_re, b_im, c_re, c_im, d_skip):
    row = lambda x: x.astype(F32).reshape(N_BLOCKS, 1, STATE_LANES)
    ldt = jnp.broadcast_to(log_dt.astype(F32)[:, None], (N_GROUPS, N_STATE))
    bt = lambda x: (x.astype(F32).reshape(N_BLOCKS, GROUPS_PER_BLOCK, N_STATE, GROUP)
                    .transpose(0, 3, 1, 2).reshape(N_BLOCKS, GROUP, STATE_LANES))
    ct = lambda x: (x.astype(F32).reshape(N_BLOCKS, GROUPS_PER_BLOCK, GROUP, N_STATE)
                    .transpose(0, 2, 1, 3).reshape(N_BLOCKS, GROUP, STATE_LANES))
    return (row(lam_re), row(lam_im), row(ldt), bt(b_re), bt(b_im), ct(c_re), ct(c_im),
            d_skip.astype(F32).reshape(N_BLOCKS, 1, LANES))


def _s5(proj3, h0_re, h0_im, s5p, buf0, buf1, conv_w, t_len, m, n_seq):
    kern = functools.partial(_s5_kernel, t_len=t_len, m=m, n_seq=n_seq)
    blk3 = lambda b: (b, 0, 0)
    row_spec = pl.BlockSpec((1, 1, STATE_LANES), blk3)
    mat_spec = pl.BlockSpec((1, GROUP, STATE_LANES), blk3)
    state_spec = pl.BlockSpec((1, n_seq, STATE_LANES), blk3)
    end_mat = pltpu.VMEM((t_len * LANES, STATE_LANES), BF16)
    chunk_state = pltpu.VMEM((m, STATE_LANES), F32)
    nbw = WIDTH // LANES
    seg_spec = lambda seg: pl.BlockSpec((t_len, m, LANES), lambda b: (0, 0, seg * nbw + b))
    return pl.pallas_call(
        kern,
        grid=(N_BLOCKS,),
        in_specs=[seg_spec(0),
                  state_spec, state_spec,
                  row_spec, row_spec, row_spec,
                  mat_spec, mat_spec, mat_spec, mat_spec,
                  pl.BlockSpec((1, 1, LANES), blk3),
                  seg_spec(2), seg_spec(3), seg_spec(4), seg_spec(5),
                  pl.BlockSpec((n_seq, LANES), lambda b: (0, b)),
                  pl.BlockSpec((n_seq, LANES), lambda b: (0, b)),
                  pl.BlockSpec((3, LANES), lambda b: (0, b))],
        out_specs=[seg_spec(0), state_spec, state_spec, seg_spec(0),
                   pl.BlockSpec((2, m, LANES), lambda b: (0, 0, b))],
        out_shape=[jax.ShapeDtypeStruct((t_len, m, WIDTH), BF16),
                   jax.ShapeDtypeStruct((N_BLOCKS, n_seq, STATE_LANES), F32),
                   jax.ShapeDtypeStruct((N_BLOCKS, n_seq, STATE_LANES), F32),
                   jax.ShapeDtypeStruct((t_len, m, WIDTH), BF16),
                   jax.ShapeDtypeStruct((2, m, WIDTH), F32)],
        scratch_shapes=[pltpu.VMEM((m, t_len * LANES), BF16),
                        end_mat, end_mat, end_mat, end_mat,
                        pltpu.VMEM((t_len * LANES, 2 * LANES), BF16),
                        chunk_state, chunk_state, chunk_state, chunk_state],
        compiler_params=_cparams(1),
        name="s5_scan",
    )(proj3, h0_re, h0_im, *s5p, proj3, proj3, proj3, proj3, buf0, buf1, conv_w)


def _glu_kernel(ya_ref, sza_ref, w_ref, b_ref, o_ref):
    ya = ya_ref[...]
    glu = jnp.dot(ya, w_ref[...], preferred_element_type=F32) + b_ref[...]
    o_ref[...] = (ya.astype(F32) * _sigmoid(glu) * sza_ref[...].astype(F32)).astype(BF16)


def _glu(ya, proj, w_glu_bf, b_glu, tm):
    rows = ya.shape[0]
    return pl.pallas_call(
        _glu_kernel,
        grid=(rows // tm,),
        in_specs=[pl.BlockSpec((tm, WIDTH), lambda i: (i, 0)),
                  pl.BlockSpec((tm, WIDTH), lambda i: (i, 1)),
                  pl.BlockSpec((WIDTH, WIDTH), lambda i: (0, 0)),
                  pl.BlockSpec((1, WIDTH), lambda i: (0, 0))],
        out_specs=pl.BlockSpec((tm, WIDTH), lambda i: (i, 0)),
        out_shape=jax.ShapeDtypeStruct((rows, WIDTH), BF16),
        compiler_params=_cparams(1),
        name="glu_gate",
    )(ya, proj, w_glu_bf, b_glu.reshape(1, WIDTH))


PERM_ROWS = 512


def _to_natural(pm, t_len):
    rows, n = pm.shape
    c = rows // t_len
    cg = PERM_ROWS // t_len
    r_nat = lax.broadcasted_iota(jnp.int32, (PERM_ROWS, PERM_ROWS), 0)
    r_in = lax.broadcasted_iota(jnp.int32, (PERM_ROWS, PERM_ROWS), 1)
    src = (r_nat & (t_len - 1)) * cg + lax.shift_right_logical(r_nat, t_len.bit_length() - 1)
    perm = jnp.where(r_in == src, 1.0, 0.0).astype(BF16)
    pm3 = pm.reshape(t_len, c, n)
    out = []
    for g in range(c // cg):
        grp = pm3[:, g * cg:(g + 1) * cg, :].reshape(PERM_ROWS, n).astype(BF16)
        out.append(jnp.dot(perm, grp, preferred_element_type=F32).astype(BF16))
    return out[0] if len(out) == 1 else jnp.concatenate(out, axis=0)


def _gated_merge(a_ref, b_ref, wa, wb, sga_ref, sgb_ref, o_ref, natural):
    t_len, c, _ = a_ref.shape
    rows = t_len * c
    pa = jnp.dot(a_ref[...].reshape(rows, WIDTH), wa, preferred_element_type=F32)
    pb = jnp.dot(b_ref[...].reshape(rows, WIDTH), wb, preferred_element_type=F32)
    tn = pa.shape[1]
    pm = sga_ref[...].reshape(rows, tn) * pa + sgb_ref[...].reshape(rows, tn) * pb
    o_ref[...] = _to_natural(pm, t_len) if natural else pm.astype(BF16)


def _merge_kernel(a_ref, b_ref, wa_ref, wb_ref, sga_ref, sgb_ref, o_ref, *, natural):
    _gated_merge(a_ref, b_ref, wa_ref[...], wb_ref[...], sga_ref, sgb_ref, o_ref, natural)


def _merge_cast_kernel(a_ref, b_ref, wa_ref, wb_ref, sga_ref, sgb_ref, o_ref, wabf_ref, wbbf_ref, *, natural):
    wa = wa_ref[...].astype(BF16)
    wb = wb_ref[...].astype(BF16)
    wabf_ref[...] = wa
    wbbf_ref[...] = wb
    _gated_merge(a_ref, b_ref, wa, wb, sga_ref, sgb_ref, o_ref, natural)


def _merge(a_in3, b_in3, proj3, w_pa, w_pb, c, natural, tn=512):
    t_len, m, _ = a_in3.shape
    nb = WIDTH // tn
    cast = w_pa.dtype != BF16
    assert not cast or c == m
    assert not natural or (PERM_ROWS % t_len == 0 and c % (PERM_ROWS // t_len) == 0)
    w_spec = pl.BlockSpec((WIDTH, tn), lambda i, j: (0, j))
    out_spec = pl.BlockSpec((t_len * c, tn), lambda i, j: (i, j))
    out_shape = jax.ShapeDtypeStruct((t_len * m, D_MODEL), BF16)
    w_shape = jax.ShapeDtypeStruct((WIDTH, D_MODEL), BF16)
    return pl.pallas_call(
        functools.partial(_merge_cast_kernel if cast else _merge_kernel, natural=natural),
        grid=(m // c, D_MODEL // tn),
        in_specs=[pl.BlockSpec((t_len, c, WIDTH), lambda i, j: (0, i, 0)),
                  pl.BlockSpec((t_len, c, WIDTH), lambda i, j: (0, i, 0)),
                  w_spec, w_spec,
                  pl.BlockSpec((t_len, c, tn), lambda i, j: (0, i, 6 * nb + j)),
                  pl.BlockSpec((t_len, c, tn), lambda i, j: (0, i, 8 * nb + j))],
        out_specs=[out_spec, w_spec, w_spec] if cast else out_spec,
        out_shape=[out_shape, w_shape, w_shape] if cast else out_shape,
        compiler_params=_cparams(2),
        name="merge_cast" if cast else "merge",
    )(a_in3, b_in3, w_pa, w_pb, proj3, proj3)


def _outproj_kernel(m_ref, w_ref, x_ref, gate_ref, fg_ref, o_ref, hres, *, tn, tiles_per_seq):
    j = pl.program_id(1)
    part = jnp.dot(m_ref[...], w_ref[...], preferred_element_type=F32)
    col = pl.ds(pl.multiple_of(j * tn, tn), tn)
    if tiles_per_seq:
        gate = gate_ref[pl.ds(pl.program_id(0) // tiles_per_seq, 1), :]
        hres[:, col] = x_ref[...] + gate * part
    else:
        n_rows = gate_ref.shape[0]
        for r in range(0, hres.shape[0], n_rows):
            hres[r:r + n_rows, col] = x_ref[r:r + n_rows, :] + gate_ref[...] * part[r:r + n_rows, :]

    @pl.when(j == pl.num_programs(1) - 1)
    def _():
        for r in range(0, hres.shape[0], LANES):
            h = hres[r:r + LANES, :]
            o_ref[r:r + LANES, :] = h * lax.rsqrt(jnp.mean(h * h, axis=-1, keepdims=True) + EPS) * fg_ref[...]


def _outproj(merged, w_o_bf, x2d, gate, final_g, tm, tiles_per_seq, tn=512):
    rows = x2d.shape[0]
    assert tiles_per_seq or tm % gate.shape[0] == 0
    gate_spec = pl.BlockSpec((gate.shape[0], tn), lambda i, j: (0, j))
    return pl.pallas_call(
        functools.partial(_outproj_kernel, tn=tn, tiles_per_seq=tiles_per_seq),
        grid=(rows // tm, D_MODEL // tn),
        in_specs=[pl.BlockSpec((tm, D_MODEL), lambda i, j: (i, 0)),
                  pl.BlockSpec((D_MODEL, tn), lambda i, j: (0, j)),
                  pl.BlockSpec((tm, tn), lambda i, j: (i, j)),
                  gate_spec,
                  pl.BlockSpec((1, D_MODEL), lambda i, j: (0, 0))],
        out_specs=pl.BlockSpec((tm, D_MODEL), lambda i, j: (i, 0)),
        out_shape=jax.ShapeDtypeStruct((rows, D_MODEL), F32),
        scratch_shapes=[pltpu.VMEM((tm, D_MODEL), F32)],
        compiler_params=_cparams(2),
        name="outproj",
    )(merged, w_o_bf, x2d, gate, final_g.reshape(1, D_MODEL))


def _state_to_blocks(state):
    n = state.shape[0]
    re = state[..., 0].reshape(n, N_BLOCKS, STATE_LANES).transpose(1, 0, 2)
    im = state[..., 1].reshape(n, N_BLOCKS, STATE_LANES).transpose(1, 0, 2)
    return re, im


def _blocks_to_state(re, im):
    n = re.shape[1]
    re = re.transpose(1, 0, 2).reshape(n, N_GROUPS, N_STATE)
    im = im.transpose(1, 0, 2).reshape(n, N_GROUPS, N_STATE)
    return jnp.stack([re, im], axis=-1)


def _group(x2d, t_len, n_seq, ct, tm, mod, ssm0, conv0, p):
    m = x2d.shape[0] // t_len
    cps = m // n_seq
    natural = cps > 1
    tiles_per_seq = cps // ct if natural else 0
    shift, scale, gate = mod
    xn = _norm_mod(x2d, t_len, p["norm_g"], scale, shift, ct, tiles_per_seq)
    xn = xn.reshape(t_len * m, D_MODEL)
    if "w_in_bf" in p:
        proj = _inproj(xn, p["w_in_bf"], tm)
    else:
        proj, p["w_in_bf"] = _inproj_cast(xn, p["w_in"])
    proj3 = proj.reshape(t_len, m, IN_COLS)
    h0_re, h0_im = _state_to_blocks(ssm0)
    ya3, hr, hi, b_in3, v_last = _s5(proj3, h0_re, h0_im, p["s5"], conv0[:, 0], conv0[:, 1], p["conv_w"],
                                     t_len, m, n_seq)
    a_in = _glu(ya3.reshape(t_len * m, WIDTH), proj, p["w_glu"], p["b_glu"], 512)
    a_in3 = a_in.reshape(t_len, m, WIDTH)
    if "w_pa_bf" in p:
        merged = _merge(a_in3, b_in3, proj3, p["w_pa_bf"], p["w_pb_bf"], tm // t_len, natural)
    else:
        merged, p["w_pa_bf"], p["w_pb_bf"] = _merge(a_in3, b_in3, proj3, p["w_pa"], p["w_pb"], m, natural)
    out = _outproj(merged, p["w_o"], x2d, gate, p["final_g"], ct * t_len, tiles_per_seq)
    conv_new = v_last[:, cps - 1::cps, :].transpose(1, 0, 2)
    return out, _blocks_to_state(hr, hi), conv_new


def kernel(x_prompt, x_sample, state_ssm, state_conv, c_prompt, c_sample, norm_g, w_ada, b_ada, w_in, lam_re, lam_im, log_dt, b_re, b_im, c_re, c_im, d_skip, w_glu, b_glu, w_pa, conv_w, w_pb, w_o, final_g):
    depth = norm_g.shape[0]
    assert depth == 1
    n_p, seq, _ = x_prompt.shape
    n_s, dec, _ = x_sample.shape
    t_p = 16
    assert seq % t_p == 0

    l = 0
    p = dict(norm_g=norm_g[l], w_in=w_in[l], conv_w=conv_w[l],
             w_glu=w_glu[l].astype(BF16), b_glu=b_glu[l], w_pa=w_pa[l],
             w_pb=w_pb[l], w_o=w_o[l].astype(BF16), final_g=final_g,
             s5=_s5_params(lam_re[l], lam_im[l], log_dt[l], b_re[l], b_im[l], c_re[l], c_im[l], d_skip[l]))

    n_c = n_p + n_s
    pad = (-n_c) % 16
    c_all = jnp.concatenate([c_prompt, c_sample, jnp.zeros((pad, D_MODEL), F32)], axis=0)
    mod = _ada(c_all, w_ada[l], b_ada[l])
    mod_p = tuple(mod[:n_p, k * D_MODEL:(k + 1) * D_MODEL] for k in range(3))
    mod_s = tuple(mod[n_p:n_c, k * D_MODEL:(k + 1) * D_MODEL] for k in range(3))

    x_s = x_sample.transpose(1, 0, 2).reshape(dec * n_s, D_MODEL)
    y_s, ssm_s, conv_s = _group(x_s, dec, n_s, n_s, dec * n_s, mod_s, state_ssm[l], state_conv[l], p)
    y_sample = y_s.reshape(dec, n_s, D_MODEL).transpose(1, 0, 2)

    ssm_p0 = jnp.zeros((n_p, N_GROUPS, N_STATE, 2), F32)
    conv_p0 = jnp.zeros((n_p, 2, WIDTH), F32)
    y_p, ssm_p, conv_p = _group(x_prompt.reshape(n_p * seq, D_MODEL), t_p, n_p, 32, 1024,
                                mod_p, ssm_p0, conv_p0, p)
    y_prompt = y_p.reshape(n_p, seq, D_MODEL)

    return (y_prompt, y_sample, ssm_p[None], conv_p[None], ssm_s[None], conv_s[None])
```

```python
import functools

import jax
import jax.numpy as jnp
from jax import lax
from jax.experimental import pallas as pl
from jax.experimental.pallas import tpu as pltpu

F32 = jnp.float32
BF16 = jnp.bfloat16

D_MODEL = 4096
WIDTH = 2048
N_GROUPS = 128
GROUP = 16
N_STATE = 64
GROUPS_PER_BLOCK = 8
N_BLOCKS = N_GROUPS // GROUPS_PER_BLOCK
STATE_LANES = GROUPS_PER_BLOCK * N_STATE
LANES = 128
IN_COLS = 10 * WIDTH
EPS = 1e-6
VMEM_LIMIT = 56 * 1024 * 1024
NT_DIMS = (((1,), (1,)), ((), ()))


def _cparams(n_axes):
    return pltpu.CompilerParams(dimension_semantics=("arbitrary",) * n_axes,
                                vmem_limit_bytes=VMEM_LIMIT)


def _sigmoid(x):
    return jax.nn.sigmoid(x)


def _ada_kernel(c_ref, w_ref, b_ref, o_ref):
    c = c_ref[...]
    a = (c * _sigmoid(c)).astype(BF16)
    o_ref[...] = jnp.dot(a, w_ref[...].astype(BF16), preferred_element_type=F32) + b_ref[...]


def _ada(c_all, w_ada, b_ada, tn=512):
    rows = c_all.shape[0]
    n_out = w_ada.shape[1]
    return pl.pallas_call(
        _ada_kernel,
        grid=(n_out // tn,),
        in_specs=[pl.BlockSpec((rows, D_MODEL), lambda j: (0, 0)),
                  pl.BlockSpec((D_MODEL, tn), lambda j: (0, j)),
                  pl.BlockSpec((1, tn), lambda j: (0, j))],
        out_specs=pl.BlockSpec((rows, tn), lambda j: (0, j)),
        out_shape=jax.ShapeDtypeStruct((rows, n_out), F32),
        compiler_params=_cparams(1),
        name="ada_mod",
    )(c_all, w_ada, b_ada.reshape(1, n_out))


def _norm_kernel(x_ref, g_ref, sc_ref, sh_ref, o_ref, *, t_len, ct, tiles_per_seq):
    def norm_mod(x, sc, sh):
        y = x * lax.rsqrt(jnp.mean(x * x, axis=-1, keepdims=True) + EPS) * g_ref[...]
        return (y * (1.0 + sc) + sh).astype(BF16)

    if not tiles_per_seq:
        for s in range(t_len):
            o_ref[s] = norm_mod(x_ref[s * ct:(s + 1) * ct, :], sc_ref[...], sh_ref[...])
        return

    n = pl.program_id(0) // tiles_per_seq
    y = norm_mod(x_ref[...], sc_ref[pl.ds(n, 1), :], sh_ref[pl.ds(n, 1), :])
    r = t_len * ct
    r_out = lax.broadcasted_iota(jnp.int32, (r, r), 0)
    r_in = lax.broadcasted_iota(jnp.int32, (r, r), 1)
    src_row = (r_out & (ct - 1)) * t_len + lax.shift_right_logical(r_out, ct.bit_length() - 1)
    perm = jnp.where(r_in == src_row, 1.0, 0.0).astype(BF16)
    yp = jnp.dot(perm, y, preferred_element_type=F32)
    o_ref[...] = yp.reshape(t_len, ct, D_MODEL).astype(BF16)


def _norm_mod(x2d, t_len, norm_g, scale, shift, ct, tiles_per_seq):
    assert ct & (ct - 1) == 0
    r = t_len * ct
    m = x2d.shape[0] // t_len
    assert tiles_per_seq or ct == m == scale.shape[0]
    mod_spec = pl.BlockSpec((scale.shape[0], D_MODEL), lambda i: (0, 0))
    return pl.pallas_call(
        functools.partial(_norm_kernel, t_len=t_len, ct=ct, tiles_per_seq=tiles_per_seq),
        grid=(m // ct,),
        in_specs=[pl.BlockSpec((r, D_MODEL), lambda i: (i, 0)),
                  pl.BlockSpec((1, D_MODEL), lambda i: (0, 0)),
                  mod_spec, mod_spec],
        out_specs=pl.BlockSpec((t_len, ct, D_MODEL), lambda i: (0, i, 0)),
        out_shape=jax.ShapeDtypeStruct((t_len, m, D_MODEL), BF16),
        compiler_params=_cparams(1),
        name="norm_mod",
    )(x2d, norm_g.reshape(1, D_MODEL), scale, shift)


def _activate(acc, seg):
    is_silu = jnp.logical_or(seg == 1, seg == 5)
    is_sig = seg >= 6
    sig = _sigmoid(acc)
    return jnp.where(is_sig, sig, jnp.where(is_silu, acc * sig, acc)).astype(BF16)


def _inproj_kernel(x_ref, w_ref, o_ref, *, tn):
    acc = jnp.dot(x_ref[...], w_ref[...], preferred_element_type=F32)
    o_ref[...] = _activate(acc, pl.program_id(1) // (WIDTH // tn))


def _inproj_cast_kernel(x_ref, w_ref, o_ref, wbf_ref, *, tn):
    w = w_ref[...].astype(BF16)
    wbf_ref[...] = w
    acc = jnp.dot(x_ref[...], w, preferred_element_type=F32)
    o_ref[...] = _activate(acc, pl.program_id(0) // (WIDTH // tn))


def _inproj_cast(xn, w_in, tn=512):
    rows = xn.shape[0]
    return pl.pallas_call(
        functools.partial(_inproj_cast_kernel, tn=tn),
        grid=(IN_COLS // tn,),
        in_specs=[pl.BlockSpec((rows, D_MODEL), lambda j: (0, 0)),
                  pl.BlockSpec((D_MODEL, tn), lambda j: (0, j))],
        out_specs=[pl.BlockSpec((rows, tn), lambda j: (0, j)),
                   pl.BlockSpec((D_MODEL, tn), lambda j: (0, j))],
        out_shape=[jax.ShapeDtypeStruct((rows, IN_COLS), BF16),
                   jax.ShapeDtypeStruct((D_MODEL, IN_COLS), BF16)],
        compiler_params=_cparams(1),
        name="inproj_cast",
    )(xn, w_in)


def _inproj(xn, w_in_bf, tm, tn=1024):
    rows = xn.shape[0]
    return pl.pallas_call(
        functools.partial(_inproj_kernel, tn=tn),
        grid=(rows // tm, IN_COLS // tn),
        in_specs=[pl.BlockSpec((tm, D_MODEL), lambda i, j: (i, 0)),
                  pl.BlockSpec((D_MODEL, tn), lambda i, j: (0, j))],
        out_specs=pl.BlockSpec((tm, tn), lambda i, j: (i, j)),
        out_shape=jax.ShapeDtypeStruct((rows, IN_COLS), BF16),
        compiler_params=_cparams(2),
        name="inproj",
    )(xn, w_in_bf)


def _cmul(ar, ai, br, bi):
    return ar * br - ai * bi, ar * bi + ai * br


def _short_conv(hb_ref, cb_ref, bb_ref, szb_ref, buf0_ref, buf1_ref, w_ref, bin_ref, v_ref, *, t_len, m, n_seq):
    cps = m // n_seq

    def prev_chunk(v, first_rows):
        if cps == 1:
            return first_rows
        out = pltpu.roll(v, 1, 0)
        rid = lax.broadcasted_iota(jnp.int32, v.shape, 0)
        for n in range(n_seq):
            out = jnp.where(rid == n * cps, first_rows[n:n + 1, :], out)
        return out

    def v_of(s):
        return hb_ref[s].astype(F32) * cb_ref[s].astype(F32)

    vm2 = prev_chunk(v_of(t_len - 2), buf0_ref[...])
    vm1 = prev_chunk(v_of(t_len - 1), buf1_ref[...])
    for s in range(t_len):
        v0 = v_of(s)
        conv = vm2 * w_ref[0:1, :] + vm1 * w_ref[1:2, :] + v0 * w_ref[2:3, :]
        bin_ref[s] = (bb_ref[s].astype(F32) * conv * szb_ref[s].astype(F32)).astype(BF16)
        if s >= t_len - 2:
            v_ref[s - (t_len - 2)] = v0
        vm2, vm1 = vm1, v0


def _s5_kernel(u_ref, h0r_ref, h0i_ref, lr_ref, li_ref, ldt_ref, btr_ref, bti_ref, ctr_ref, cti_ref, d_ref,
               hb_ref, cb_ref, bb_ref, szb_ref, buf0_ref, buf1_ref, cw_ref,
               y_ref, hr_out, hi_out, bin_ref, v_ref,
               ucat, bend_r, bend_i, cend_r, cend_i, kpair, hl_r, hl_i, hp_r, hp_i, *, t_len, m, n_seq):
    blk = pl.program_id(0)
    _short_conv(hb_ref, cb_ref, bb_ref, szb_ref, buf0_ref, buf1_ref, cw_ref, bin_ref, v_ref,
                t_len=t_len, m=m, n_seq=n_seq)

    @pl.when(blk == 0)
    def _():
        for ref in (bend_r, bend_i, cend_r, cend_i, kpair):
            ref[...] = jnp.zeros(ref.shape, ref.dtype)

    lam_r = lr_ref[0]
    lam_i = li_ref[0]
    dt = jnp.exp(ldt_ref[0])
    mag = jnp.exp(lam_r * dt)
    lb_r = mag * jnp.cos(lam_i * dt)
    lb_i = mag * jnp.sin(lam_i * dt)
    den = lam_r * lam_r + lam_i * lam_i
    nr = lb_r - 1.0
    co_r = (nr * lam_r + lb_i * lam_i) / den
    co_i = (lb_i * lam_r - nr * lam_i) / den
    bb_r, bb_i = _cmul(co_r, co_i, btr_ref[0], bti_ref[0])
    c_r = ctr_ref[0]
    c_i = cti_ref[0]
    pw = [(jnp.ones_like(lb_r), jnp.zeros_like(lb_r))]
    for _ in range(t_len):
        pw.append(_cmul(pw[-1][0], pw[-1][1], lb_r, lb_i))
    ar, ai = pw[t_len]

    lane = lax.broadcasted_iota(jnp.int32, (GROUP, LANES), 1)
    lo = lane < N_STATE

    def pair_tile(x, q):
        slab = x[:, q * LANES:(q + 1) * LANES]
        return jnp.concatenate([jnp.where(lo, slab, 0.0), jnp.where(lo, 0.0, slab)], axis=0).astype(BF16)

    grp_row = lax.broadcasted_iota(jnp.int32, (LANES, STATE_LANES), 0) // GROUP
    grp_lane = lax.broadcasted_iota(jnp.int32, (LANES, STATE_LANES), 1) // N_STATE
    same_group = grp_row == grp_lane

    def expand(x):
        return jnp.where(same_group, jnp.concatenate([x] * GROUPS_PER_BLOCK, axis=0), 0.0).astype(BF16)

    bbx_r = expand(bb_r)
    bbx_i = expand(bb_i)
    for s in range(t_len):
        ucat[:, s * LANES:(s + 1) * LANES] = u_ref[s]
        be_r, be_i = _cmul(pw[t_len - 1 - s][0], pw[t_len - 1 - s][1], bb_r, bb_i)
        ce_r, ce_i = _cmul(pw[s + 1][0], pw[s + 1][1], c_r, c_i)
        for q in range(4):
            r0 = s * LANES + q * 2 * GROUP
            rows = slice(r0, r0 + 2 * GROUP)
            cols = slice(q * LANES, (q + 1) * LANES)
            bend_r[rows, cols] = pair_tile(be_r, q)
            bend_i[rows, cols] = pair_tile(be_i, q)
            cend_r[rows, cols] = pair_tile(ce_r, q)
            cend_i[rows, cols] = pair_tile(-ce_i, q)
        cl_r, cl_i = _cmul(pw[s][0], pw[s][1], c_r, c_i)
        k_lag = (lax.dot_general(bbx_r, expand(cl_r), NT_DIMS, preferred_element_type=F32)
                 - lax.dot_general(bbx_i, expand(cl_i), NT_DIMS, preferred_element_type=F32))
        k_lag = k_lag.astype(BF16)
        kpair[(t_len - 1 - s) * LANES:(t_len - s) * LANES, LANES:] = k_lag
        if s <= t_len - 2:
            kpair[(t_len - 2 - s) * LANES:(t_len - 1 - s) * LANES, :LANES] = k_lag

    uc = ucat[...]
    hl_r[...] = jnp.dot(uc, bend_r[...], preferred_element_type=F32)
    hl_i[...] = jnp.dot(uc, bend_i[...], preferred_element_type=F32)

    cps = m // n_seq
    if cps == 1:
        h_r = h0r_ref[0]
        h_i = h0i_ref[0]
        hp_r[...] = h_r
        hp_i[...] = h_i
        hr_out[0] = ar * h_r - ai * h_i + hl_r[...]
        hi_out[0] = ar * h_i + ai * h_r + hl_i[...]
    else:
        def body(c, carry):
            new = []
            for n in range(n_seq):
                h_r, h_i = carry[n]
                row = pl.ds(n * cps + c, 1)
                hp_r[row, :] = h_r
                hp_i[row, :] = h_i
                new.append((ar * h_r - ai * h_i + hl_r[row, :], ar * h_i + ai * h_r + hl_i[row, :]))
            return tuple(new)

        init = tuple((h0r_ref[0, n:n + 1, :], h0i_ref[0, n:n + 1, :]) for n in range(n_seq))
        fin = lax.fori_loop(0, cps, body, init)
        for n in range(n_seq):
            hr_out[0, n:n + 1, :] = fin[n][0]
            hi_out[0, n:n + 1, :] = fin[n][1]

    ycar = (lax.dot_general(hp_r[...].astype(BF16), cend_r[...], NT_DIMS, preferred_element_type=F32)
            + lax.dot_general(hp_i[...].astype(BF16), cend_i[...], NT_DIMS, preferred_element_type=F32))
    dvec = d_ref[0]
    for t in range(0, t_len, 2):
        y2 = jnp.dot(ucat[:, :(t + 2) * LANES], kpair[(t_len - 2 - t) * LANES:, :],
                     preferred_element_type=F32) + ycar[:, t * LANES:(t + 2) * LANES]
        for k in range(2):
            y = y2[:, k * LANES:(k + 1) * LANES] + dvec * u_ref[t + k].astype(F32)
            y_ref[t + k] = jax.nn.gelu(y).astype(BF16)


def _s5_params(lam_re, lam_im, log_dt, b_re, b_im, c_re, c_im, d_skip):
    row = lambda x: x.astype(F32).reshape(N_BLOCKS, 1, STATE_LANES)
    ldt = jnp.broadcast_to(log_dt.astype(F32)[:, None], (N_GROUPS, N_STATE))
    bt = lambda x: (x.astype(F32).reshape(N_BLOCKS, GROUPS_PER_BLOCK, N_STATE, GROUP)
                    .transpose(0, 3, 1, 2).reshape(N_BLOCKS, GROUP, STATE_LANES))
    ct = lambda x: (x.astype(F32).reshape(N_BLOCKS, GROUPS_PER_BLOCK, GROUP, N_STATE)
                    .transpose(0, 2, 1, 3).reshape(N_BLOCKS, GROUP, STATE_LANES))
    return (row(lam_re), row(lam_im), row(ldt), bt(b_re), bt(b_im), ct(c_re), ct(c_im),
            d_skip.astype(F32).reshape(N_BLOCKS, 1, LANES))


def _s5(proj3, h0_re, h0_im, s5p, buf0, buf1, conv_w, t_len, m, n_seq):
    kern = functools.partial(_s5_kernel, t_len=t_len, m=m, n_seq=n_seq)
    blk3 = lambda b: (b, 0, 0)
    row_spec = pl.BlockSpec((1, 1, STATE_LANES), blk3)
    mat_spec = pl.BlockSpec((1, GROUP, STATE_LANES), blk3)
    state_spec = pl.BlockSpec((1, n_seq, STATE_LANES), blk3)
    end_mat = pltpu.VMEM((t_len * LANES, STATE_LANES), BF16)
    chunk_state = pltpu.VMEM((m, STATE_LANES), F32)
    nbw = WIDTH // LANES
    seg_spec = lambda seg: pl.BlockSpec((t_len, m, LANES), lambda b: (0, 0, seg * nbw + b))
    return pl.pallas_call(
        kern,
        grid=(N_BLOCKS,),
        in_specs=[seg_spec(0),
                  state_spec, state_spec,
                  row_spec, row_spec, row_spec,
                  mat_spec, mat_spec, mat_spec, mat_spec,
                  pl.BlockSpec((1, 1, LANES), blk3),
                  seg_spec(2), seg_spec(3), seg_spec(4), seg_spec(5),
                  pl.BlockSpec((n_seq, LANES), lambda b: (0, b)),
                  pl.BlockSpec((n_seq, LANES), lambda b: (0, b)),
                  pl.BlockSpec((3, LANES), lambda b: (0, b))],
        out_specs=[seg_spec(0), state_spec, state_spec, seg_spec(0),
                   pl.BlockSpec((2, m, LANES), lambda b: (0, 0, b))],
        out_shape=[jax.ShapeDtypeStruct((t_len, m, WIDTH), BF16),
                   jax.ShapeDtypeStruct((N_BLOCKS, n_seq, STATE_LANES), F32),
                   jax.ShapeDtypeStruct((N_BLOCKS, n_seq, STATE_LANES), F32),
                   jax.ShapeDtypeStruct((t_len, m, WIDTH), BF16),
                   jax.ShapeDtypeStruct((2, m, WIDTH), F32)],
        scratch_shapes=[pltpu.VMEM((m, t_len * LANES), BF16),
                        end_mat, end_mat, end_mat, end_mat,
                        pltpu.VMEM((t_len * LANES, 2 * LANES), BF16),
                        chunk_state, chunk_state, chunk_state, chunk_state],
        compiler_params=_cparams(1),
        name="s5_scan",
    )(proj3, h0_re, h0_im, *s5p, proj3, proj3, proj3, proj3, buf0, buf1, conv_w)


def _glu_kernel(ya_ref, sza_ref, w_ref, b_ref, o_ref):
    ya = ya_ref[...]
    glu = jnp.dot(ya, w_ref[...], preferred_element_type=F32) + b_ref[...]
    o_ref[...] = (ya.astype(F32) * _sigmoid(glu) * sza_ref[...].astype(F32)).astype(BF16)


def _glu_cast_kernel(ya_ref, sza_ref, w_ref, b_ref, o_ref, wbf_ref, *, tn):
    w = w_ref[...].astype(BF16)
    wbf_ref[...] = w
    glu = jnp.dot(ya_ref[...], w, preferred_element_type=F32) + b_ref[...]
    ya = ya_ref[:, pl.ds(pl.multiple_of(pl.program_id(0) * tn, tn), tn)].astype(F32)
    o_ref[...] = (ya * _sigmoid(glu) * sza_ref[...].astype(F32)).astype(BF16)


def _glu(ya, proj, w_glu, b_glu, tm=1024, tn=512):
    rows = ya.shape[0]
    out_shape = jax.ShapeDtypeStruct((rows, WIDTH), BF16)
    if w_glu.dtype == BF16:
        return pl.pallas_call(
            _glu_kernel,
            grid=(rows // tm,),
            in_specs=[pl.BlockSpec((tm, WIDTH), lambda i: (i, 0)),
                      pl.BlockSpec((tm, WIDTH), lambda i: (i, 1)),
                      pl.BlockSpec((WIDTH, WIDTH), lambda i: (0, 0), pipeline_mode=pl.Buffered(1)),
                      pl.BlockSpec((1, WIDTH), lambda i: (0, 0))],
            out_specs=pl.BlockSpec((tm, WIDTH), lambda i: (i, 0)),
            out_shape=out_shape,
            compiler_params=_cparams(1),
            name="glu_gate",
        )(ya, proj, w_glu, b_glu.reshape(1, WIDTH))
    nb = WIDTH // tn
    w_spec = pl.BlockSpec((WIDTH, tn), lambda j: (0, j))
    return pl.pallas_call(
        functools.partial(_glu_cast_kernel, tn=tn),
        grid=(nb,),
        in_specs=[pl.BlockSpec((rows, WIDTH), lambda j: (0, 0)),
                  pl.BlockSpec((rows, tn), lambda j: (0, nb + j)),
                  w_spec,
                  pl.BlockSpec((1, tn), lambda j: (0, j))],
        out_specs=[pl.BlockSpec((rows, tn), lambda j: (0, j)), w_spec],
        out_shape=[out_shape, jax.ShapeDtypeStruct((WIDTH, WIDTH), BF16)],
        compiler_params=_cparams(1),
        name="glu_gate_cast",
    )(ya, proj, w_glu, b_glu.reshape(1, WIDTH))


PERM_ROWS = 512


def _to_natural(pm, t_len):
    rows, n = pm.shape
    c = rows // t_len
    cg = PERM_ROWS // t_len
    r_nat = lax.broadcasted_iota(jnp.int32, (PERM_ROWS, PERM_ROWS), 0)
    r_in = lax.broadcasted_iota(jnp.int32, (PERM_ROWS, PERM_ROWS), 1)
    src = (r_nat & (t_len - 1)) * cg + lax.shift_right_logical(r_nat, t_len.bit_length() - 1)
    perm = jnp.where(r_in == src, 1.0, 0.0).astype(BF16)
    pm3 = pm.reshape(t_len, c, n)
    out = []
    for g in range(c // cg):
        grp = pm3[:, g * cg:(g + 1) * cg, :].reshape(PERM_ROWS, n).astype(BF16)
        out.append(jnp.dot(perm, grp, preferred_element_type=F32).astype(BF16))
    return out[0] if len(out) == 1 else jnp.concatenate(out, axis=0)


def _gated_merge(a_ref, b_ref, wa, wb, sga_ref, sgb_ref, o_ref, natural):
    t_len, c, _ = a_ref.shape
    rows = t_len * c
    pa = jnp.dot(a_ref[...].reshape(rows, WIDTH), wa, preferred_element_type=F32)
    pb = jnp.dot(b_ref[...].reshape(rows, WIDTH), wb, preferred_element_type=F32)
    tn = pa.shape[1]
    pm = sga_ref[...].reshape(rows, tn) * pa + sgb_ref[...].reshape(rows, tn) * pb
    o_ref[...] = _to_natural(pm, t_len) if natural else pm.astype(BF16)


def _merge_kernel(a_ref, b_ref, wa_ref, wb_ref, sga_ref, sgb_ref, o_ref, *, natural):
    _gated_merge(a_ref, b_ref, wa_ref[...], wb_ref[...], sga_ref, sgb_ref, o_ref, natural)


def _merge_cast_kernel(a_ref, b_ref, wa_ref, wb_ref, sga_ref, sgb_ref, o_ref, wabf_ref, wbbf_ref, *, natural):
    wa = wa_ref[...].astype(BF16)
    wb = wb_ref[...].astype(BF16)
    wabf_ref[...] = wa
    wbbf_ref[...] = wb
    _gated_merge(a_ref, b_ref, wa, wb, sga_ref, sgb_ref, o_ref, natural)


def _merge(a_in3, b_in3, proj3, w_pa, w_pb, c, natural, tn=512):
    t_len, m, _ = a_in3.shape
    nb = WIDTH // tn
    cast = w_pa.dtype != BF16
    assert not cast or c == m
    assert not natural or (PERM_ROWS % t_len == 0 and c % (PERM_ROWS // t_len) == 0)
    w_spec = pl.BlockSpec((WIDTH, tn), lambda i, j: (0, j))
    out_spec = pl.BlockSpec((t_len * c, tn), lambda i, j: (i, j))
    out_shape = jax.ShapeDtypeStruct((t_len * m, D_MODEL), BF16)
    w_shape = jax.ShapeDtypeStruct((WIDTH, D_MODEL), BF16)
    return pl.pallas_call(
        functools.partial(_merge_cast_kernel if cast else _merge_kernel, natural=natural),
        grid=(m // c, D_MODEL // tn),
        in_specs=[pl.BlockSpec((t_len, c, WIDTH), lambda i, j: (0, i, 0)),
                  pl.BlockSpec((t_len, c, WIDTH), lambda i, j: (0, i, 0)),
                  w_spec, w_spec,
                  pl.BlockSpec((t_len, c, tn), lambda i, j: (0, i, 6 * nb + j)),
                  pl.BlockSpec((t_len, c, tn), lambda i, j: (0, i, 8 * nb + j))],
        out_specs=[out_spec, w_spec, w_spec] if cast else out_spec,
        out_shape=[out_shape, w_shape, w_shape] if cast else out_shape,
        compiler_params=_cparams(2),
        name="merge_cast" if cast else "merge",
    )(a_in3, b_in3, w_pa, w_pb, proj3, proj3)


def _outproj_kernel(m_ref, w_ref, x_ref, gate_ref, fg_ref, o_ref, *rest, tn, tiles_per_seq):
    j = pl.program_id(1)
    w = w_ref[...]
    if len(rest) == 2:
        wbf_ref, hres = rest
        w = w.astype(BF16)
        wbf_ref[...] = w
    else:
        (hres,) = rest
    part = jnp.dot(m_ref[...], w, preferred_element_type=F32)
    col = pl.ds(pl.multiple_of(j * tn, tn), tn)
    if tiles_per_seq:
        gate = gate_ref[pl.ds(pl.program_id(0) // tiles_per_seq, 1), :]
        hres[:, col] = x_ref[...] + gate * part
    else:
        n_rows = gate_ref.shape[0]
        for r in range(0, hres.shape[0], n_rows):
            hres[r:r + n_rows, col] = x_ref[r:r + n_rows, :] + gate_ref[...] * part[r:r + n_rows, :]

    @pl.when(j == pl.num_programs(1) - 1)
    def _():
        for r in range(0, hres.shape[0], LANES):
            h = hres[r:r + LANES, :]
            o_ref[r:r + LANES, :] = h * lax.rsqrt(jnp.mean(h * h, axis=-1, keepdims=True) + EPS) * fg_ref[...]


def _outproj(merged, w_o, x2d, gate, final_g, tm, tiles_per_seq):
    rows = x2d.shape[0]
    cast = w_o.dtype != BF16
    tn = 256 if cast else 512
    assert tiles_per_seq or tm % gate.shape[0] == 0
    assert not cast or tm == rows
    w_spec = pl.BlockSpec((D_MODEL, tn), lambda i, j: (0, j))
    out_spec = pl.BlockSpec((tm, D_MODEL), lambda i, j: (i, 0))
    out_shape = jax.ShapeDtypeStruct((rows, D_MODEL), F32)
    return pl.pallas_call(
        functools.partial(_outproj_kernel, tn=tn, tiles_per_seq=tiles_per_seq),
        grid=(rows // tm, D_MODEL // tn),
        in_specs=[pl.BlockSpec((tm, D_MODEL), lambda i, j: (i, 0)),
                  w_spec,
                  pl.BlockSpec((tm, tn), lambda i, j: (i, j)),
                  pl.BlockSpec((gate.shape[0], tn), lambda i, j: (0, j)),
                  pl.BlockSpec((1, D_MODEL), lambda i, j: (0, 0))],
        out_specs=[out_spec, w_spec] if cast else out_spec,
        out_shape=[out_shape, jax.ShapeDtypeStruct((D_MODEL, D_MODEL), BF16)] if cast else out_shape,
        scratch_shapes=[pltpu.VMEM((tm, D_MODEL), F32)],
        compiler_params=_cparams(2),
        name="outproj_cast" if cast else "outproj",
    )(merged, w_o, x2d, gate, final_g.reshape(1, D_MODEL))


def _state_to_blocks(state):
    n = state.shape[0]
    re = state[..., 0].reshape(n, N_BLOCKS, STATE_LANES).transpose(1, 0, 2)
    im = state[..., 1].reshape(n, N_BLOCKS, STATE_LANES).transpose(1, 0, 2)
    return re, im


def _blocks_to_state(re, im):
    n = re.shape[1]
    re = re.transpose(1, 0, 2).reshape(n, N_GROUPS, N_STATE)
    im = im.transpose(1, 0, 2).reshape(n, N_GROUPS, N_STATE)
    return jnp.stack([re, im], axis=-1)


def _group(x2d, t_len, n_seq, ct, tm, mod, ssm0, conv0, p):
    m = x2d.shape[0] // t_len
    cps = m // n_seq
    natural = cps > 1
    tiles_per_seq = cps // ct if natural else 0
    shift, scale, gate = mod
    xn = _norm_mod(x2d, t_len, p["norm_g"], scale, shift, ct, tiles_per_seq)
    xn = xn.reshape(t_len * m, D_MODEL)
    if "w_in_bf" in p:
        proj = _inproj(xn, p["w_in_bf"], tm)
    else:
        proj, p["w_in_bf"] = _inproj_cast(xn, p["w_in"])
    proj3 = proj.reshape(t_len, m, IN_COLS)
    h0_re, h0_im = _state_to_blocks(ssm0)
    ya3, hr, hi, b_in3, v_last = _s5(proj3, h0_re, h0_im, p["s5"], conv0[:, 0], conv0[:, 1], p["conv_w"],
                                     t_len, m, n_seq)
    ya = ya3.reshape(t_len * m, WIDTH)
    if "w_glu_bf" in p:
        a_in = _glu(ya, proj, p["w_glu_bf"], p["b_glu"])
    else:
        a_in, p["w_glu_bf"] = _glu(ya, proj, p["w_glu"], p["b_glu"])
    a_in3 = a_in.reshape(t_len, m, WIDTH)
    if "w_pa_bf" in p:
        merged = _merge(a_in3, b_in3, proj3, p["w_pa_bf"], p["w_pb_bf"], tm // t_len, natural)
    else:
        merged, p["w_pa_bf"], p["w_pb_bf"] = _merge(a_in3, b_in3, proj3, p["w_pa"], p["w_pb"], m, natural)
    if "w_o_bf" in p:
        out = _outproj(merged, p["w_o_bf"], x2d, gate, p["final_g"], ct * t_len, tiles_per_seq)
    else:
        out, p["w_o_bf"] = _outproj(merged, p["w_o"], x2d, gate, p["final_g"], ct * t_len, tiles_per_seq)
    conv_new = v_last[:, cps - 1::cps, :].transpose(1, 0, 2)
    return out, _blocks_to_state(hr, hi), conv_new


def kernel(x_prompt, x_sample, state_ssm, state_conv, c_prompt, c_sample, norm_g, w_ada, b_ada, w_in, lam_re, lam_im, log_dt, b_re, b_im, c_re, c_im, d_skip, w_glu, b_glu, w_pa, conv_w, w_pb, w_o, final_g):
    depth = norm_g.shape[0]
    assert depth == 1
    n_p, seq, _ = x_prompt.shape
    n_s, dec, _ = x_sample.shape
    t_p = 16
    assert seq % t_p == 0

    l = 0
    p = dict(norm_g=norm_g[l], w_in=w_in[l], conv_w=conv_w[l], w_glu=w_glu[l], b_glu=b_glu[l],
             w_pa=w_pa[l], w_pb=w_pb[l], w_o=w_o[l], final_g=final_g,
             s5=_s5_params(lam_re[l], lam_im[l], log_dt[l], b_re[l], b_im[l], c_re[l], c_im[l], d_skip[l]))

    n_c = n_p + n_s
    pad = (-n_c) % 16
    c_all = jnp.concatenate([c_prompt, c_sample, jnp.zeros((pad, D_MODEL), F32)], axis=0)
    mod = _ada(c_all, w_ada[l], b_ada[l])
    mod_p = tuple(mod[:n_p, k * D_MODEL:(k + 1) * D_MODEL] for k in range(3))
    mod_s = tuple(mod[n_p:n_c, k * D_MODEL:(k + 1) * D_MODEL] for k in range(3))

    x_s = x_sample.transpose(1, 0, 2).reshape(dec * n_s, D_MODEL)
    y_s, ssm_s, conv_s = _group(x_s, dec, n_s, n_s, dec * n_s, mod_s, state_ssm[l], state_conv[l], p)
    y_sample = y_s.reshape(dec, n_s, D_MODEL).transpose(1, 0, 2)

    ssm_p0 = jnp.zeros((n_p, N_GROUPS, N_STATE, 2), F32)
    conv_p0 = jnp.zeros((n_p, 2, WIDTH), F32)
    y_p, ssm_p, conv_p = _group(x_prompt.reshape(n_p * seq, D_MODEL), t_p, n_p, 32, 1024,
                                mod_p, ssm_p0, conv_p0, p)
    y_prompt = y_p.reshape(n_p, seq, D_MODEL)

    return (y_prompt, y_sample, ssm_p[None], conv_p[None], ssm_s[None], conv_s[None])
```

```python
import functools

import jax
import jax.numpy as jnp
from jax import lax
from jax.experimental import pallas as pl
from jax.experimental.pallas import tpu as pltpu

F32 = jnp.float32
BF16 = jnp.bfloat16

D_MODEL = 4096
WIDTH = 2048
N_GROUPS = 128
GROUP = 16
N_STATE = 64
GROUPS_PER_BLOCK = 8
N_BLOCKS = N_GROUPS // GROUPS_PER_BLOCK
STATE_LANES = GROUPS_PER_BLOCK * N_STATE
LANES = 128
IN_COLS = 10 * WIDTH
EPS = 1e-6
MIB = 1024 * 1024
V7X_VMEM_BYTES = 64 * MIB
VMEM_LIMIT = V7X_VMEM_BYTES - 8 * MIB
NT_DIMS = (((1,), (1,)), ((), ()))


def _cparams(n_axes, vmem_limit=VMEM_LIMIT):
    return pltpu.CompilerParams(dimension_semantics=("arbitrary",) * n_axes,
                                vmem_limit_bytes=vmem_limit)


def _sigmoid(x):
    return jax.nn.sigmoid(x)


def _ada_kernel(c_ref, w_ref, b_ref, o_ref):
    c = c_ref[...]
    a = (c * _sigmoid(c)).astype(BF16)
    o_ref[...] = jnp.dot(a, w_ref[...].astype(BF16), preferred_element_type=F32) + b_ref[...]


def _ada(c_all, w_ada, b_ada, tn=512):
    rows = c_all.shape[0]
    n_out = w_ada.shape[1]
    return pl.pallas_call(
        _ada_kernel,
        grid=(n_out // tn,),
        in_specs=[pl.BlockSpec((rows, D_MODEL), lambda j: (0, 0)),
                  pl.BlockSpec((D_MODEL, tn), lambda j: (0, j)),
                  pl.BlockSpec((1, tn), lambda j: (0, j))],
        out_specs=pl.BlockSpec((rows, tn), lambda j: (0, j)),
        out_shape=jax.ShapeDtypeStruct((rows, n_out), F32),
        compiler_params=_cparams(1),
        name="ada_mod",
    )(c_all, w_ada, b_ada.reshape(1, n_out))


def _norm_kernel(x_ref, g_ref, sc_ref, sh_ref, o_ref, *, t_len, ct, tiles_per_seq):
    def norm_mod(x, sc, sh):
        y = x * lax.rsqrt(jnp.mean(x * x, axis=-1, keepdims=True) + EPS) * g_ref[...]
        return (y * (1.0 + sc) + sh).astype(BF16)

    if not tiles_per_seq:
        for s in range(t_len):
            o_ref[s] = norm_mod(x_ref[s * ct:(s + 1) * ct, :], sc_ref[...], sh_ref[...])
        return

    n = pl.program_id(0) // tiles_per_seq
    y = norm_mod(x_ref[...], sc_ref[pl.ds(n, 1), :], sh_ref[pl.ds(n, 1), :])
    r = t_len * ct
    r_out = lax.broadcasted_iota(jnp.int32, (r, r), 0)
    r_in = lax.broadcasted_iota(jnp.int32, (r, r), 1)
    src_row = (r_out & (ct - 1)) * t_len + lax.shift_right_logical(r_out, ct.bit_length() - 1)
    perm = jnp.where(r_in == src_row, 1.0, 0.0).astype(BF16)
    yp = jnp.dot(perm, y, preferred_element_type=F32)
    o_ref[...] = yp.reshape(t_len, ct, D_MODEL).astype(BF16)


def _norm_mod(x2d, t_len, norm_g, scale, shift, ct, tiles_per_seq):
    assert ct & (ct - 1) == 0
    r = t_len * ct
    m = x2d.shape[0] // t_len
    assert tiles_per_seq or ct == m == scale.shape[0]
    mod_spec = pl.BlockSpec((scale.shape[0], D_MODEL), lambda i: (0, 0))
    return pl.pallas_call(
        functools.partial(_norm_kernel, t_len=t_len, ct=ct, tiles_per_seq=tiles_per_seq),
        grid=(m // ct,),
        in_specs=[pl.BlockSpec((r, D_MODEL), lambda i: (i, 0)),
                  pl.BlockSpec((1, D_MODEL), lambda i: (0, 0)),
                  mod_spec, mod_spec],
        out_specs=pl.BlockSpec((t_len, ct, D_MODEL), lambda i: (0, i, 0)),
        out_shape=jax.ShapeDtypeStruct((t_len, m, D_MODEL), BF16),
        compiler_params=_cparams(1),
        name="norm_mod",
    )(x2d, norm_g.reshape(1, D_MODEL), scale, shift)


def _activate(acc, seg):
    is_silu = jnp.logical_or(seg == 1, seg == 5)
    is_sig = seg >= 6
    sig = _sigmoid(acc)
    return jnp.where(is_sig, sig, jnp.where(is_silu, acc * sig, acc)).astype(BF16)


def _inproj_kernel(x_ref, w_ref, o_ref, *, tn):
    acc = jnp.dot(x_ref[...], w_ref[...], preferred_element_type=F32)
    o_ref[...] = _activate(acc, pl.program_id(1) // (WIDTH // tn))


def _inproj_cast_kernel(x_ref, w_ref, o_ref, wbf_ref, *, tn):
    w = w_ref[...].astype(BF16)
    wbf_ref[...] = w
    acc = jnp.dot(x_ref[...], w, preferred_element_type=F32)
    o_ref[...] = _activate(acc, pl.program_id(0) // (WIDTH // tn))


def _inproj_cast(xn, w_in, tn=512):
    rows = xn.shape[0]
    return pl.pallas_call(
        functools.partial(_inproj_cast_kernel, tn=tn),
        grid=(IN_COLS // tn,),
        in_specs=[pl.BlockSpec((rows, D_MODEL), lambda j: (0, 0)),
                  pl.BlockSpec((D_MODEL, tn), lambda j: (0, j))],
        out_specs=[pl.BlockSpec((rows, tn), lambda j: (0, j)),
                   pl.BlockSpec((D_MODEL, tn), lambda j: (0, j))],
        out_shape=[jax.ShapeDtypeStruct((rows, IN_COLS), BF16),
                   jax.ShapeDtypeStruct((D_MODEL, IN_COLS), BF16)],
        compiler_params=_cparams(1),
        name="inproj_cast",
    )(xn, w_in)


def _inproj(xn, w_in_bf, tm, tn=1024):
    rows = xn.shape[0]
    return pl.pallas_call(
        functools.partial(_inproj_kernel, tn=tn),
        grid=(rows // tm, IN_COLS // tn),
        in_specs=[pl.BlockSpec((tm, D_MODEL), lambda i, j: (i, 0)),
                  pl.BlockSpec((D_MODEL, tn), lambda i, j: (0, j))],
        out_specs=pl.BlockSpec((tm, tn), lambda i, j: (i, j)),
        out_shape=jax.ShapeDtypeStruct((rows, IN_COLS), BF16),
        compiler_params=_cparams(2),
        name="inproj",
    )(xn, w_in_bf)


def _cmul(ar, ai, br, bi):
    return ar * br - ai * bi, ar * bi + ai * br


def _short_conv(hb_ref, cb_ref, bb_ref, szb_ref, buf0_ref, buf1_ref, w_ref, bin_ref, v_ref, *, t_len, m, n_seq):
    cps = m // n_seq

    def prev_chunk(v, first_rows):
        if cps == 1:
            return first_rows
        out = pltpu.roll(v, 1, 0)
        rid = lax.broadcasted_iota(jnp.int32, v.shape, 0)
        for n in range(n_seq):
            out = jnp.where(rid == n * cps, first_rows[n:n + 1, :], out)
        return out

    def v_of(s):
        return hb_ref[s].astype(F32) * cb_ref[s].astype(F32)

    vm2 = prev_chunk(v_of(t_len - 2), buf0_ref[...])
    vm1 = prev_chunk(v_of(t_len - 1), buf1_ref[...])
    for s in range(t_len):
        v0 = v_of(s)
        conv = vm2 * w_ref[0:1, :] + vm1 * w_ref[1:2, :] + v0 * w_ref[2:3, :]
        bin_ref[s] = (bb_ref[s].astype(F32) * conv * szb_ref[s].astype(F32)).astype(BF16)
        if s >= t_len - 2:
            v_ref[s - (t_len - 2)] = v0
        vm2, vm1 = vm1, v0


def _s5_kernel(u_ref, h0r_ref, h0i_ref, lr_ref, li_ref, ldt_ref, btr_ref, bti_ref, ctr_ref, cti_ref, d_ref,
               hb_ref, cb_ref, bb_ref, szb_ref, buf0_ref, buf1_ref, cw_ref,
               y_ref, hr_out, hi_out, bin_ref, v_ref,
               ucat, bend_r, bend_i, cend_r, cend_i, kpair, hl_r, hl_i, hp_r, hp_i, *, t_len, m, n_seq):
    blk = pl.program_id(0)
    _short_conv(hb_ref, cb_ref, bb_ref, szb_ref, buf0_ref, buf1_ref, cw_ref, bin_ref, v_ref,
                t_len=t_len, m=m, n_seq=n_seq)

    @pl.when(blk == 0)
    def _():
        for ref in (bend_r, bend_i, cend_r, cend_i, kpair):
            ref[...] = jnp.zeros(ref.shape, ref.dtype)

    lam_r = lr_ref[0]
    lam_i = li_ref[0]
    dt = jnp.exp(ldt_ref[0])
    mag = jnp.exp(lam_r * dt)
    lb_r = mag * jnp.cos(lam_i * dt)
    lb_i = mag * jnp.sin(lam_i * dt)
    den = lam_r * lam_r + lam_i * lam_i
    nr = lb_r - 1.0
    co_r = (nr * lam_r + lb_i * lam_i) / den
    co_i = (lb_i * lam_r - nr * lam_i) / den
    bb_r, bb_i = _cmul(co_r, co_i, btr_ref[0], bti_ref[0])
    c_r = ctr_ref[0]
    c_i = cti_ref[0]
    pw = [(jnp.ones_like(lb_r), jnp.zeros_like(lb_r))]
    for _ in range(t_len):
        pw.append(_cmul(pw[-1][0], pw[-1][1], lb_r, lb_i))
    ar, ai = pw[t_len]

    lane = lax.broadcasted_iota(jnp.int32, (GROUP, LANES), 1)
    lo = lane < N_STATE

    def pair_tile(x, q):
        slab = x[:, q * LANES:(q + 1) * LANES]
        return jnp.concatenate([jnp.where(lo, slab, 0.0), jnp.where(lo, 0.0, slab)], axis=0).astype(BF16)

    grp_row = lax.broadcasted_iota(jnp.int32, (LANES, STATE_LANES), 0) // GROUP
    grp_lane = lax.broadcasted_iota(jnp.int32, (LANES, STATE_LANES), 1) // N_STATE
    same_group = grp_row == grp_lane

    def expand(x):
        return jnp.where(same_group, jnp.concatenate([x] * GROUPS_PER_BLOCK, axis=0), 0.0).astype(BF16)

    bbx_r = expand(bb_r)
    bbx_i = expand(bb_i)
    for s in range(t_len):
        ucat[:, s * LANES:(s + 1) * LANES] = u_ref[s]
        be_r, be_i = _cmul(pw[t_len - 1 - s][0], pw[t_len - 1 - s][1], bb_r, bb_i)
        ce_r, ce_i = _cmul(pw[s + 1][0], pw[s + 1][1], c_r, c_i)
        for q in range(4):
            r0 = s * LANES + q * 2 * GROUP
            rows = slice(r0, r0 + 2 * GROUP)
            cols = slice(q * LANES, (q + 1) * LANES)
            bend_r[rows, cols] = pair_tile(be_r, q)
            bend_i[rows, cols] = pair_tile(be_i, q)
            cend_r[rows, cols] = pair_tile(ce_r, q)
            cend_i[rows, cols] = pair_tile(-ce_i, q)
        cl_r, cl_i = _cmul(pw[s][0], pw[s][1], c_r, c_i)
        k_lag = (lax.dot_general(bbx_r, expand(cl_r), NT_DIMS, preferred_element_type=F32)
                 - lax.dot_general(bbx_i, expand(cl_i), NT_DIMS, preferred_element_type=F32))
        k_lag = k_lag.astype(BF16)
        kpair[(t_len - 1 - s) * LANES:(t_len - s) * LANES, LANES:] = k_lag
        if s <= t_len - 2:
            kpair[(t_len - 2 - s) * LANES:(t_len - 1 - s) * LANES, :LANES] = k_lag

    uc = ucat[...]
    hl_r[...] = jnp.dot(uc, bend_r[...], preferred_element_type=F32)
    hl_i[...] = jnp.dot(uc, bend_i[...], preferred_element_type=F32)

    cps = m // n_seq
    if cps == 1:
        h_r = h0r_ref[0]
        h_i = h0i_ref[0]
        hp_r[...] = h_r
        hp_i[...] = h_i
        hr_out[0] = ar * h_r - ai * h_i + hl_r[...]
        hi_out[0] = ar * h_i + ai * h_r + hl_i[...]
    else:
        def body(c, carry):
            new = []
            for n in range(n_seq):
                h_r, h_i = carry[n]
                row = pl.ds(n * cps + c, 1)
                hp_r[row, :] = h_r
                hp_i[row, :] = h_i
                new.append((ar * h_r - ai * h_i + hl_r[row, :], ar * h_i + ai * h_r + hl_i[row, :]))
            return tuple(new)

        init = tuple((h0r_ref[0, n:n + 1, :], h0i_ref[0, n:n + 1, :]) for n in range(n_seq))
        fin = lax.fori_loop(0, cps, body, init)
        for n in range(n_seq):
            hr_out[0, n:n + 1, :] = fin[n][0]
            hi_out[0, n:n + 1, :] = fin[n][1]

    ycar = (lax.dot_general(hp_r[...].astype(BF16), cend_r[...], NT_DIMS, preferred_element_type=F32)
            + lax.dot_general(hp_i[...].astype(BF16), cend_i[...], NT_DIMS, preferred_element_type=F32))
    dvec = d_ref[0]
    for t in range(0, t_len, 2):
        y2 = jnp.dot(ucat[:, :(t + 2) * LANES], kpair[(t_len - 2 - t) * LANES:, :],
                     preferred_element_type=F32) + ycar[:, t * LANES:(t + 2) * LANES]
        for k in range(2):
            y = y2[:, k * LANES:(k + 1) * LANES] + dvec * u_ref[t + k].astype(F32)
            y_ref[t + k] = jax.nn.gelu(y).astype(BF16)


def _s5_params(lam_re, lam_im, log_dt, b_re, b_im, c_re, c_im, d_skip):
    row = lambda x: x.astype(F32).reshape(N_BLOCKS, 1, STATE_LANES)
    ldt = jnp.broadcast_to(log_dt.astype(F32)[:, None], (N_GROUPS, N_STATE))
    bt = lambda x: (x.astype(F32).reshape(N_BLOCKS, GROUPS_PER_BLOCK, N_STATE, GROUP)
                    .transpose(0, 3, 1, 2).reshape(N_BLOCKS, GROUP, STATE_LANES))
    ct = lambda x: (x.astype(F32).reshape(N_BLOCKS, GROUPS_PER_BLOCK, GROUP, N_STATE)
                    .transpose(0, 2, 1, 3).reshape(N_BLOCKS, GROUP, STATE_LANES))
    return (row(lam_re), row(lam_im), row(ldt), bt(b_re), bt(b_im), ct(c_re), ct(c_im),
            d_skip.astype(F32).reshape(N_BLOCKS, 1, LANES))


def _s5(proj3, h0_re, h0_im, s5p, buf0, buf1, conv_w, t_len, m, n_seq):
    kern = functools.partial(_s5_kernel, t_len=t_len, m=m, n_seq=n_seq)
    blk3 = lambda b: (b, 0, 0)
    row_spec = pl.BlockSpec((1, 1, STATE_LANES), blk3)
    mat_spec = pl.BlockSpec((1, GROUP, STATE_LANES), blk3)
    state_spec = pl.BlockSpec((1, n_seq, STATE_LANES), blk3)
    end_mat = pltpu.VMEM((t_len * LANES, STATE_LANES), BF16)
    chunk_state = pltpu.VMEM((m, STATE_LANES), F32)
    nbw = WIDTH // LANES
    seg_spec = lambda seg: pl.BlockSpec((t_len, m, LANES), lambda b: (0, 0, seg * nbw + b))
    return pl.pallas_call(
        kern,
        grid=(N_BLOCKS,),
        in_specs=[seg_spec(0),
                  state_spec, state_spec,
                  row_spec, row_spec, row_spec,
                  mat_spec, mat_spec, mat_spec, mat_spec,
                  pl.BlockSpec((1, 1, LANES), blk3),
                  seg_spec(2), seg_spec(3), seg_spec(4), seg_spec(5),
                  pl.BlockSpec((n_seq, LANES), lambda b: (0, b)),
                  pl.BlockSpec((n_seq, LANES), lambda b: (0, b)),
                  pl.BlockSpec((3, LANES), lambda b: (0, b))],
        out_specs=[seg_spec(0), state_spec, state_spec, seg_spec(0),
                   pl.BlockSpec((2, m, LANES), lambda b: (0, 0, b))],
        out_shape=[jax.ShapeDtypeStruct((t_len, m, WIDTH), BF16),
                   jax.ShapeDtypeStruct((N_BLOCKS, n_seq, STATE_LANES), F32),
                   jax.ShapeDtypeStruct((N_BLOCKS, n_seq, STATE_LANES), F32),
                   jax.ShapeDtypeStruct((t_len, m, WIDTH), BF16),
                   jax.ShapeDtypeStruct((2, m, WIDTH), F32)],
        scratch_shapes=[pltpu.VMEM((m, t_len * LANES), BF16),
                        end_mat, end_mat, end_mat, end_mat,
                        pltpu.VMEM((t_len * LANES, 2 * LANES), BF16),
                        chunk_state, chunk_state, chunk_state, chunk_state],
        compiler_params=_cparams(1),
        name="s5_scan",
    )(proj3, h0_re, h0_im, *s5p, proj3, proj3, proj3, proj3, buf0, buf1, conv_w)


def _glu_kernel(ya_ref, sza_ref, w_ref, b_ref, o_ref):
    ya = ya_ref[...]
    glu = jnp.dot(ya, w_ref[...], preferred_element_type=F32) + b_ref[...]
    o_ref[...] = (ya.astype(F32) * _sigmoid(glu) * sza_ref[...].astype(F32)).astype(BF16)


def _glu_cast_kernel(ya_ref, sza_ref, w_ref, b_ref, o_ref, wbf_ref, *, tn):
    w = w_ref[...].astype(BF16)
    wbf_ref[...] = w
    glu = jnp.dot(ya_ref[...], w, preferred_element_type=F32) + b_ref[...]
    ya = ya_ref[:, pl.ds(pl.multiple_of(pl.program_id(0) * tn, tn), tn)].astype(F32)
    o_ref[...] = (ya * _sigmoid(glu) * sza_ref[...].astype(F32)).astype(BF16)


def _glu(ya, proj, w_glu, b_glu, tm=1024, tn=512):
    rows = ya.shape[0]
    out_shape = jax.ShapeDtypeStruct((rows, WIDTH), BF16)
    if w_glu.dtype == BF16:
        return pl.pallas_call(
            _glu_kernel,
            grid=(rows // tm,),
            in_specs=[pl.BlockSpec((tm, WIDTH), lambda i: (i, 0)),
                      pl.BlockSpec((tm, WIDTH), lambda i: (i, 1)),
                      pl.BlockSpec((WIDTH, WIDTH), lambda i: (0, 0), pipeline_mode=pl.Buffered(1)),
                      pl.BlockSpec((1, WIDTH), lambda i: (0, 0))],
            out_specs=pl.BlockSpec((tm, WIDTH), lambda i: (i, 0)),
            out_shape=out_shape,
            compiler_params=_cparams(1),
            name="glu_gate",
        )(ya, proj, w_glu, b_glu.reshape(1, WIDTH))
    nb = WIDTH // tn
    w_spec = pl.BlockSpec((WIDTH, tn), lambda j: (0, j))
    return pl.pallas_call(
        functools.partial(_glu_cast_kernel, tn=tn),
        grid=(nb,),
        in_specs=[pl.BlockSpec((rows, WIDTH), lambda j: (0, 0)),
                  pl.BlockSpec((rows, tn), lambda j: (0, nb + j)),
                  w_spec,
                  pl.BlockSpec((1, tn), lambda j: (0, j))],
        out_specs=[pl.BlockSpec((rows, tn), lambda j: (0, j)), w_spec],
        out_shape=[out_shape, jax.ShapeDtypeStruct((WIDTH, WIDTH), BF16)],
        compiler_params=_cparams(1),
        name="glu_gate_cast",
    )(ya, proj, w_glu, b_glu.reshape(1, WIDTH))


PERM_ROWS = 512


def _to_natural(pm, t_len):
    rows, n = pm.shape
    c = rows // t_len
    cg = PERM_ROWS // t_len
    r_nat = lax.broadcasted_iota(jnp.int32, (PERM_ROWS, PERM_ROWS), 0)
    r_in = lax.broadcasted_iota(jnp.int32, (PERM_ROWS, PERM_ROWS), 1)
    src = (r_nat & (t_len - 1)) * cg + lax.shift_right_logical(r_nat, t_len.bit_length() - 1)
    perm = jnp.where(r_in == src, 1.0, 0.0).astype(BF16)
    pm3 = pm.reshape(t_len, c, n)
    out = []
    for g in range(c // cg):
        grp = pm3[:, g * cg:(g + 1) * cg, :].reshape(PERM_ROWS, n).astype(BF16)
        out.append(jnp.dot(perm, grp, preferred_element_type=F32).astype(BF16))
    return out[0] if len(out) == 1 else jnp.concatenate(out, axis=0)


def _gated_merge(a_ref, b_ref, wa, wb, sga_ref, sgb_ref, o_ref, natural):
    t_len, c, _ = a_ref.shape
    rows = t_len * c
    pa = jnp.dot(a_ref[...].reshape(rows, WIDTH), wa, preferred_element_type=F32)
    pb = jnp.dot(b_ref[...].reshape(rows, WIDTH), wb, preferred_element_type=F32)
    tn = pa.shape[1]
    pm = sga_ref[...].reshape(rows, tn) * pa + sgb_ref[...].reshape(rows, tn) * pb
    o_ref[...] = _to_natural(pm, t_len) if natural else pm.astype(BF16)


def _merge_kernel(a_ref, b_ref, wa_ref, wb_ref, sga_ref, sgb_ref, o_ref, *, natural):
    _gated_merge(a_ref, b_ref, wa_ref[...], wb_ref[...], sga_ref, sgb_ref, o_ref, natural)


def _merge_cast_kernel(a_ref, b_ref, wa_ref, wb_ref, sga_ref, sgb_ref, o_ref, wabf_ref, wbbf_ref, *, natural):
    wa = wa_ref[...].astype(BF16)
    wb = wb_ref[...].astype(BF16)
    wabf_ref[...] = wa
    wbbf_ref[...] = wb
    _gated_merge(a_ref, b_ref, wa, wb, sga_ref, sgb_ref, o_ref, natural)


def _merge(a_in3, b_in3, proj3, w_pa, w_pb, c, natural, tn=512):
    t_len, m, _ = a_in3.shape
    nb = WIDTH // tn
    cast = w_pa.dtype != BF16
    assert not cast or c == m
    assert not natural or (PERM_ROWS % t_len == 0 and c % (PERM_ROWS // t_len) == 0)
    w_spec = pl.BlockSpec((WIDTH, tn), lambda i, j: (0, j))
    out_spec = pl.BlockSpec((t_len * c, tn), lambda i, j: (i, j))
    out_shape = jax.ShapeDtypeStruct((t_len * m, D_MODEL), BF16)
    w_shape = jax.ShapeDtypeStruct((WIDTH, D_MODEL), BF16)
    return pl.pallas_call(
        functools.partial(_merge_cast_kernel if cast else _merge_kernel, natural=natural),
        grid=(m // c, D_MODEL // tn),
        in_specs=[pl.BlockSpec((t_len, c, WIDTH), lambda i, j: (0, i, 0)),
                  pl.BlockSpec((t_len, c, WIDTH), lambda i, j: (0, i, 0)),
                  w_spec, w_spec,
                  pl.BlockSpec((t_len, c, tn), lambda i, j: (0, i, 6 * nb + j)),
                  pl.BlockSpec((t_len, c, tn), lambda i, j: (0, i, 8 * nb + j))],
        out_specs=[out_spec, w_spec, w_spec] if cast else out_spec,
        out_shape=[out_shape, w_shape, w_shape] if cast else out_shape,
        compiler_params=_cparams(2),
        name="merge_cast" if cast else "merge",
    )(a_in3, b_in3, w_pa, w_pb, proj3, proj3)


def _outproj_kernel(m_ref, w_ref, x_ref, gate_ref, fg_ref, o_ref, *rest, tn, tiles_per_seq):
    j = pl.program_id(1)
    w = w_ref[...]
    if len(rest) == 2:
        wbf_ref, hres = rest
        w = w.astype(BF16)
        wbf_ref[...] = w
    else:
        (hres,) = rest
    part = jnp.dot(m_ref[...], w, preferred_element_type=F32)
    col = pl.ds(pl.multiple_of(j * tn, tn), tn)
    if tiles_per_seq:
        gate = gate_ref[pl.ds(pl.program_id(0) // tiles_per_seq, 1), :]
        hres[:, col] = x_ref[...] + gate * part
    else:
        n_rows = gate_ref.shape[0]
        for r in range(0, hres.shape[0], n_rows):
            hres[r:r + n_rows, col] = x_ref[r:r + n_rows, :] + gate_ref[...] * part[r:r + n_rows, :]

    @pl.when(j == pl.num_programs(1) - 1)
    def _():
        for r in range(0, hres.shape[0], LANES):
            h = hres[r:r + LANES, :]
            o_ref[r:r + LANES, :] = h * lax.rsqrt(jnp.mean(h * h, axis=-1, keepdims=True) + EPS) * fg_ref[...]


def _outproj(merged, w_o, x2d, gate, final_g, tm, tiles_per_seq):
    rows = x2d.shape[0]
    cast = w_o.dtype != BF16
    tn = 256 if cast else 1024
    assert tiles_per_seq or tm % gate.shape[0] == 0
    assert not cast or tm == rows
    w_spec = pl.BlockSpec((D_MODEL, tn), lambda i, j: (0, j))
    out_spec = pl.BlockSpec((tm, D_MODEL), lambda i, j: (i, 0))
    out_shape = jax.ShapeDtypeStruct((rows, D_MODEL), F32)
    return pl.pallas_call(
        functools.partial(_outproj_kernel, tn=tn, tiles_per_seq=tiles_per_seq),
        grid=(rows // tm, D_MODEL // tn),
        in_specs=[pl.BlockSpec((tm, D_MODEL), lambda i, j: (i, 0)),
                  w_spec,
                  pl.BlockSpec((tm, tn), lambda i, j: (i, j)),
                  pl.BlockSpec((gate.shape[0], tn), lambda i, j: (0, j)),
                  pl.BlockSpec((1, D_MODEL), lambda i, j: (0, 0))],
        out_specs=[out_spec, w_spec] if cast else out_spec,
        out_shape=[out_shape, jax.ShapeDtypeStruct((D_MODEL, D_MODEL), BF16)] if cast else out_shape,
        scratch_shapes=[pltpu.VMEM((tm, D_MODEL), F32)],
        compiler_params=_cparams(2, V7X_VMEM_BYTES - 4 * MIB),
        name="outproj_cast" if cast else "outproj",
    )(merged, w_o, x2d, gate, final_g.reshape(1, D_MODEL))


def _state_to_blocks(state):
    n = state.shape[0]
    re = state[..., 0].reshape(n, N_BLOCKS, STATE_LANES).transpose(1, 0, 2)
    im = state[..., 1].reshape(n, N_BLOCKS, STATE_LANES).transpose(1, 0, 2)
    return re, im


def _blocks_to_state(re, im):
    n = re.shape[1]
    re = re.transpose(1, 0, 2).reshape(n, N_GROUPS, N_STATE)
    im = im.transpose(1, 0, 2).reshape(n, N_GROUPS, N_STATE)
    return jnp.stack([re, im], axis=-1)


def _group(x2d, t_len, n_seq, ct, tm, mod, ssm0, conv0, p):
    m = x2d.shape[0] // t_len
    cps = m // n_seq
    natural = cps > 1
    tiles_per_seq = cps // ct if natural else 0
    shift, scale, gate = mod
    xn = _norm_mod(x2d, t_len, p["norm_g"], scale, shift, ct, tiles_per_seq)
    xn = xn.reshape(t_len * m, D_MODEL)
    if "w_in_bf" in p:
        proj = _inproj(xn, p["w_in_bf"], tm)
    else:
        proj, p["w_in_bf"] = _inproj_cast(xn, p["w_in"])
    proj3 = proj.reshape(t_len, m, IN_COLS)
    h0_re, h0_im = _state_to_blocks(ssm0)
    ya3, hr, hi, b_in3, v_last = _s5(proj3, h0_re, h0_im, p["s5"], conv0[:, 0], conv0[:, 1], p["conv_w"],
                                     t_len, m, n_seq)
    ya = ya3.reshape(t_len * m, WIDTH)
    if "w_glu_bf" in p:
        a_in = _glu(ya, proj, p["w_glu_bf"], p["b_glu"])
    else:
        a_in, p["w_glu_bf"] = _glu(ya, proj, p["w_glu"], p["b_glu"])
    a_in3 = a_in.reshape(t_len, m, WIDTH)
    if "w_pa_bf" in p:
        merged = _merge(a_in3, b_in3, proj3, p["w_pa_bf"], p["w_pb_bf"], tm // t_len, natural)
    else:
        merged, p["w_pa_bf"], p["w_pb_bf"] = _merge(a_in3, b_in3, proj3, p["w_pa"], p["w_pb"], m, natural)
    if "w_o_bf" in p:
        out = _outproj(merged, p["w_o_bf"], x2d, gate, p["final_g"], ct * t_len, tiles_per_seq)
    else:
        out, p["w_o_bf"] = _outproj(merged, p["w_o"], x2d, gate, p["final_g"], ct * t_len, tiles_per_seq)
    conv_new = v_last[:, cps - 1::cps, :].transpose(1, 0, 2)
    return out, _blocks_to_state(hr, hi), conv_new


def kernel(x_prompt, x_sample, state_ssm, state_conv, c_prompt, c_sample, norm_g, w_ada, b_ada, w_in, lam_re, lam_im, log_dt, b_re, b_im, c_re, c_im, d_skip, w_glu, b_glu, w_pa, conv_w, w_pb, w_o, final_g):
    depth = norm_g.shape[0]
    assert depth == 1
    n_p, seq, _ = x_prompt.shape
    n_s, dec, _ = x_sample.shape
    t_p = 16
    assert seq % t_p == 0

    l = 0
    p = dict(norm_g=norm_g[l], w_in=w_in[l], conv_w=conv_w[l], w_glu=w_glu[l], b_glu=b_glu[l],
             w_pa=w_pa[l], w_pb=w_pb[l], w_o=w_o[l], final_g=final_g,
             s5=_s5_params(lam_re[l], lam_im[l], log_dt[l], b_re[l], b_im[l], c_re[l], c_im[l], d_skip[l]))

    n_c = n_p + n_s
    pad = (-n_c) % 16
    c_all = jnp.concatenate([c_prompt, c_sample, jnp.zeros((pad, D_MODEL), F32)], axis=0)
    mod = _ada(c_all, w_ada[l], b_ada[l])
    mod_p = tuple(mod[:n_p, k * D_MODEL:(k + 1) * D_MODEL] for k in range(3))
    mod_s = tuple(mod[n_p:n_c, k * D_MODEL:(k + 1) * D_MODEL] for k in range(3))

    x_s = x_sample.transpose(1, 0, 2).reshape(dec * n_s, D_MODEL)
    y_s, ssm_s, conv_s = _group(x_s, dec, n_s, n_s, dec * n_s, mod_s, state_ssm[l], state_conv[l], p)
    y_sample = y_s.reshape(dec, n_s, D_MODEL).transpose(1, 0, 2)

    ssm_p0 = jnp.zeros((n_p, N_GROUPS, N_STATE, 2), F32)
    conv_p0 = jnp.zeros((n_p, 2, WIDTH), F32)
    y_p, ssm_p, conv_p = _group(x_prompt.reshape(n_p * seq, D_MODEL), t_p, n_p, 32, 1024,
                                mod_p, ssm_p0, conv_p0, p)
    y_prompt = y_p.reshape(n_p, seq, D_MODEL)

    return (y_prompt, y_sample, ssm_p[None], conv_p[None], ssm_s[None], conv_s[None])
```

```python
import functools

import jax
import jax.numpy as jnp
from jax import lax
from jax.experimental import pallas as pl
from jax.experimental.pallas import tpu as pltpu

F32 = jnp.float32
BF16 = jnp.bfloat16

D_MODEL = 4096
WIDTH = 2048
N_GROUPS = 128
GROUP = 16
N_STATE = 64
GROUPS_PER_BLOCK = 8
N_BLOCKS = N_GROUPS // GROUPS_PER_BLOCK
STATE_LANES = GROUPS_PER_BLOCK * N_STATE
LANES = 128
IN_COLS = 10 * WIDTH
EPS = 1e-6
MIB = 1024 * 1024
V7X_VMEM_BYTES = 64 * MIB
VMEM_LIMIT = V7X_VMEM_BYTES - 8 * MIB
NT_DIMS = (((1,), (1,)), ((), ()))


def _cparams(n_axes, vmem_limit=VMEM_LIMIT):
    return pltpu.CompilerParams(dimension_semantics=("arbitrary",) * n_axes,
                                vmem_limit_bytes=vmem_limit)


def _sigmoid(x):
    return jax.nn.sigmoid(x)


def _ada_kernel(c_ref, w_ref, b_ref, o_ref):
    c = c_ref[...]
    a = (c * _sigmoid(c)).astype(BF16)
    o_ref[...] = jnp.dot(a, w_ref[...].astype(BF16), preferred_element_type=F32) + b_ref[...]


def _ada(c_all, w_ada, b_ada, tn=1024):
    rows = c_all.shape[0]
    n_out = w_ada.shape[1]
    return pl.pallas_call(
        _ada_kernel,
        grid=(n_out // tn,),
        in_specs=[pl.BlockSpec((rows, D_MODEL), lambda j: (0, 0)),
                  pl.BlockSpec((D_MODEL, tn), lambda j: (0, j)),
                  pl.BlockSpec((1, tn), lambda j: (0, j))],
        out_specs=pl.BlockSpec((rows, tn), lambda j: (0, j)),
        out_shape=jax.ShapeDtypeStruct((rows, n_out), F32),
        compiler_params=_cparams(1),
        name="ada_mod",
    )(c_all, w_ada, b_ada.reshape(1, n_out))


def _norm_kernel(x_ref, g_ref, sc_ref, sh_ref, o_ref, *, t_len, ct, tiles_per_seq):
    def norm_mod(x, sc, sh):
        y = x * lax.rsqrt(jnp.mean(x * x, axis=-1, keepdims=True) + EPS) * g_ref[...]
        return (y * (1.0 + sc) + sh).astype(BF16)

    if not tiles_per_seq:
        for s in range(t_len):
            o_ref[s] = norm_mod(x_ref[s * ct:(s + 1) * ct, :], sc_ref[...], sh_ref[...])
        return

    n = pl.program_id(0) // tiles_per_seq
    y = norm_mod(x_ref[...], sc_ref[pl.ds(n, 1), :], sh_ref[pl.ds(n, 1), :])
    r = t_len * ct
    r_out = lax.broadcasted_iota(jnp.int32, (r, r), 0)
    r_in = lax.broadcasted_iota(jnp.int32, (r, r), 1)
    src_row = (r_out & (ct - 1)) * t_len + lax.shift_right_logical(r_out, ct.bit_length() - 1)
    perm = jnp.where(r_in == src_row, 1.0, 0.0).astype(BF16)
    yp = jnp.dot(perm, y, preferred_element_type=F32)
    o_ref[...] = yp.reshape(t_len, ct, D_MODEL).astype(BF16)


def _norm_mod(x2d, t_len, norm_g, scale, shift, ct, tiles_per_seq):
    assert ct & (ct - 1) == 0
    r = t_len * ct
    m = x2d.shape[0] // t_len
    assert tiles_per_seq or ct == m == scale.shape[0]
    mod_spec = pl.BlockSpec((scale.shape[0], D_MODEL), lambda i: (0, 0))
    return pl.pallas_call(
        functools.partial(_norm_kernel, t_len=t_len, ct=ct, tiles_per_seq=tiles_per_seq),
        grid=(m // ct,),
        in_specs=[pl.BlockSpec((r, D_MODEL), lambda i: (i, 0)),
                  pl.BlockSpec((1, D_MODEL), lambda i: (0, 0)),
                  mod_spec, mod_spec],
        out_specs=pl.BlockSpec((t_len, ct, D_MODEL), lambda i: (0, i, 0)),
        out_shape=jax.ShapeDtypeStruct((t_len, m, D_MODEL), BF16),
        compiler_params=_cparams(1),
        name="norm_mod",
    )(x2d, norm_g.reshape(1, D_MODEL), scale, shift)


def _activate(acc, seg):
    is_silu = jnp.logical_or(seg == 1, seg == 5)
    is_sig = seg >= 6
    sig = _sigmoid(acc)
    return jnp.where(is_sig, sig, jnp.where(is_silu, acc * sig, acc)).astype(BF16)


def _inproj_kernel(x_ref, w_ref, o_ref, *, tn):
    acc = jnp.dot(x_ref[...], w_ref[...], preferred_element_type=F32)
    o_ref[...] = _activate(acc, pl.program_id(1) // (WIDTH // tn))


def _inproj_cast_kernel(x_ref, w_ref, o_ref, wbf_ref, *, tn):
    w = w_ref[...].astype(BF16)
    wbf_ref[...] = w
    acc = jnp.dot(x_ref[...], w, preferred_element_type=F32)
    o_ref[...] = _activate(acc, pl.program_id(0) // (WIDTH // tn))


def _inproj_cast(xn, w_in, tn=512):
    rows = xn.shape[0]
    return pl.pallas_call(
        functools.partial(_inproj_cast_kernel, tn=tn),
        grid=(IN_COLS // tn,),
        in_specs=[pl.BlockSpec((rows, D_MODEL), lambda j: (0, 0)),
                  pl.BlockSpec((D_MODEL, tn), lambda j: (0, j))],
        out_specs=[pl.BlockSpec((rows, tn), lambda j: (0, j)),
                   pl.BlockSpec((D_MODEL, tn), lambda j: (0, j))],
        out_shape=[jax.ShapeDtypeStruct((rows, IN_COLS), BF16),
                   jax.ShapeDtypeStruct((D_MODEL, IN_COLS), BF16)],
        compiler_params=_cparams(1),
        name="inproj_cast",
    )(xn, w_in)


def _inproj(xn, w_in_bf, tm, tn=1024):
    rows = xn.shape[0]
    return pl.pallas_call(
        functools.partial(_inproj_kernel, tn=tn),
        grid=(rows // tm, IN_COLS // tn),
        in_specs=[pl.BlockSpec((tm, D_MODEL), lambda i, j: (i, 0)),
                  pl.BlockSpec((D_MODEL, tn), lambda i, j: (0, j))],
        out_specs=pl.BlockSpec((tm, tn), lambda i, j: (i, j)),
        out_shape=jax.ShapeDtypeStruct((rows, IN_COLS), BF16),
        compiler_params=_cparams(2),
        name="inproj",
    )(xn, w_in_bf)


def _cmul(ar, ai, br, bi):
    return ar * br - ai * bi, ar * bi + ai * br


def _short_conv(hb_ref, cb_ref, bb_ref, szb_ref, buf0_ref, buf1_ref, w_ref, bin_ref, v_ref, *, t_len, m, n_seq):
    cps = m // n_seq

    def prev_chunk(v, first_rows):
        if cps == 1:
            return first_rows
        out = pltpu.roll(v, 1, 0)
        rid = lax.broadcasted_iota(jnp.int32, v.shape, 0)
        for n in range(n_seq):
            out = jnp.where(rid == n * cps, first_rows[n:n + 1, :], out)
        return out

    def v_of(s):
        return hb_ref[s].astype(F32) * cb_ref[s].astype(F32)

    vm2 = prev_chunk(v_of(t_len - 2), buf0_ref[...])
    vm1 = prev_chunk(v_of(t_len - 1), buf1_ref[...])
    for s in range(t_len):
        v0 = v_of(s)
        conv = vm2 * w_ref[0:1, :] + vm1 * w_ref[1:2, :] + v0 * w_ref[2:3, :]
        bin_ref[s] = (bb_ref[s].astype(F32) * conv * szb_ref[s].astype(F32)).astype(BF16)
        if s >= t_len - 2:
            v_ref[s - (t_len - 2)] = v0
        vm2, vm1 = vm1, v0


def _s5_kernel(u_ref, h0r_ref, h0i_ref, lr_ref, li_ref, ldt_ref, btr_ref, bti_ref, ctr_ref, cti_ref, d_ref,
               hb_ref, cb_ref, bb_ref, szb_ref, buf0_ref, buf1_ref, cw_ref,
               y_ref, hr_out, hi_out, bin_ref, v_ref,
               ucat, bend_r, bend_i, cend_r, cend_i, kpair, hl_r, hl_i, hp_r, hp_i, *, t_len, m, n_seq):
    blk = pl.program_id(0)
    _short_conv(hb_ref, cb_ref, bb_ref, szb_ref, buf0_ref, buf1_ref, cw_ref, bin_ref, v_ref,
                t_len=t_len, m=m, n_seq=n_seq)

    @pl.when(blk == 0)
    def _():
        for ref in (bend_r, bend_i, cend_r, cend_i, kpair):
            ref[...] = jnp.zeros(ref.shape, ref.dtype)

    lam_r = lr_ref[0]
    lam_i = li_ref[0]
    dt = jnp.exp(ldt_ref[0])
    mag = jnp.exp(lam_r * dt)
    lb_r = mag * jnp.cos(lam_i * dt)
    lb_i = mag * jnp.sin(lam_i * dt)
    den = lam_r * lam_r + lam_i * lam_i
    nr = lb_r - 1.0
    co_r = (nr * lam_r + lb_i * lam_i) / den
    co_i = (lb_i * lam_r - nr * lam_i) / den
    bb_r, bb_i = _cmul(co_r, co_i, btr_ref[0], bti_ref[0])
    c_r = ctr_ref[0]
    c_i = cti_ref[0]
    pw = [(jnp.ones_like(lb_r), jnp.zeros_like(lb_r))]
    for _ in range(t_len):
        pw.append(_cmul(pw[-1][0], pw[-1][1], lb_r, lb_i))
    ar, ai = pw[t_len]

    lane = lax.broadcasted_iota(jnp.int32, (GROUP, LANES), 1)
    lo = lane < N_STATE

    def pair_tile(x, q):
        slab = x[:, q * LANES:(q + 1) * LANES]
        return jnp.concatenate([jnp.where(lo, slab, 0.0), jnp.where(lo, 0.0, slab)], axis=0).astype(BF16)

    grp_row = lax.broadcasted_iota(jnp.int32, (LANES, STATE_LANES), 0) // GROUP
    grp_lane = lax.broadcasted_iota(jnp.int32, (LANES, STATE_LANES), 1) // N_STATE
    same_group = grp_row == grp_lane

    def expand(x):
        return jnp.where(same_group, jnp.concatenate([x] * GROUPS_PER_BLOCK, axis=0), 0.0).astype(BF16)

    bbx_r = expand(bb_r)
    bbx_i = expand(bb_i)
    for s in range(t_len):
        ucat[:, s * LANES:(s + 1) * LANES] = u_ref[s]
        be_r, be_i = _cmul(pw[t_len - 1 - s][0], pw[t_len - 1 - s][1], bb_r, bb_i)
        ce_r, ce_i = _cmul(pw[s + 1][0], pw[s + 1][1], c_r, c_i)
        for q in range(4):
            r0 = s * LANES + q * 2 * GROUP
            rows = slice(r0, r0 + 2 * GROUP)
            cols = slice(q * LANES, (q + 1) * LANES)
            bend_r[rows, cols] = pair_tile(be_r, q)
            bend_i[rows, cols] = pair_tile(be_i, q)
            cend_r[rows, cols] = pair_tile(ce_r, q)
            cend_i[rows, cols] = pair_tile(-ce_i, q)
        cl_r, cl_i = _cmul(pw[s][0], pw[s][1], c_r, c_i)
        k_lag = (lax.dot_general(bbx_r, expand(cl_r), NT_DIMS, preferred_element_type=F32)
                 - lax.dot_general(bbx_i, expand(cl_i), NT_DIMS, preferred_element_type=F32))
        k_lag = k_lag.astype(BF16)
        kpair[(t_len - 1 - s) * LANES:(t_len - s) * LANES, LANES:] = k_lag
        if s <= t_len - 2:
            kpair[(t_len - 2 - s) * LANES:(t_len - 1 - s) * LANES, :LANES] = k_lag

    uc = ucat[...]
    hl_r[...] = jnp.dot(uc, bend_r[...], preferred_element_type=F32)
    hl_i[...] = jnp.dot(uc, bend_i[...], preferred_element_type=F32)

    cps = m // n_seq
    if cps == 1:
        h_r = h0r_ref[0]
        h_i = h0i_ref[0]
        hp_r[...] = h_r
        hp_i[...] = h_i
        hr_out[0] = ar * h_r - ai * h_i + hl_r[...]
        hi_out[0] = ar * h_i + ai * h_r + hl_i[...]
    else:
        def body(c, carry):
            new = []
            for n in range(n_seq):
                h_r, h_i = carry[n]
                row = pl.ds(n * cps + c, 1)
                hp_r[row, :] = h_r
                hp_i[row, :] = h_i
                new.append((ar * h_r - ai * h_i + hl_r[row, :], ar * h_i + ai * h_r + hl_i[row, :]))
            return tuple(new)

        init = tuple((h0r_ref[0, n:n + 1, :], h0i_ref[0, n:n + 1, :]) for n in range(n_seq))
        fin = lax.fori_loop(0, cps, body, init)
        for n in range(n_seq):
            hr_out[0, n:n + 1, :] = fin[n][0]
            hi_out[0, n:n + 1, :] = fin[n][1]

    ycar = (lax.dot_general(hp_r[...].astype(BF16), cend_r[...], NT_DIMS, preferred_element_type=F32)
            + lax.dot_general(hp_i[...].astype(BF16), cend_i[...], NT_DIMS, preferred_element_type=F32))
    dvec = d_ref[0]
    for t in range(0, t_len, 2):
        y2 = jnp.dot(ucat[:, :(t + 2) * LANES], kpair[(t_len - 2 - t) * LANES:, :],
                     preferred_element_type=F32) + ycar[:, t * LANES:(t + 2) * LANES]
        for k in range(2):
            y = y2[:, k * LANES:(k + 1) * LANES] + dvec * u_ref[t + k].astype(F32)
            y_ref[t + k] = jax.nn.gelu(y).astype(BF16)


def _s5_params(lam_re, lam_im, log_dt, b_re, b_im, c_re, c_im, d_skip):
    row = lambda x: x.astype(F32).reshape(N_BLOCKS, 1, STATE_LANES)
    ldt = jnp.broadcast_to(log_dt.astype(F32)[:, None], (N_GROUPS, N_STATE))
    bt = lambda x: (x.astype(F32).reshape(N_BLOCKS, GROUPS_PER_BLOCK, N_STATE, GROUP)
                    .transpose(0, 3, 1, 2).reshape(N_BLOCKS, GROUP, STATE_LANES))
    ct = lambda x: (x.astype(F32).reshape(N_BLOCKS, GROUPS_PER_BLOCK, GROUP, N_STATE)
                    .transpose(0, 2, 1, 3).reshape(N_BLOCKS, GROUP, STATE_LANES))
    return (row(lam_re), row(lam_im), row(ldt), bt(b_re), bt(b_im), ct(c_re), ct(c_im),
            d_skip.astype(F32).reshape(N_BLOCKS, 1, LANES))


def _s5(proj3, h0_re, h0_im, s5p, buf0, buf1, conv_w, t_len, m, n_seq):
    kern = functools.partial(_s5_kernel, t_len=t_len, m=m, n_seq=n_seq)
    blk3 = lambda b: (b, 0, 0)
    row_spec = pl.BlockSpec((1, 1, STATE_LANES), blk3)
    mat_spec = pl.BlockSpec((1, GROUP, STATE_LANES), blk3)
    state_spec = pl.BlockSpec((1, n_seq, STATE_LANES), blk3)
    end_mat = pltpu.VMEM((t_len * LANES, STATE_LANES), BF16)
    chunk_state = pltpu.VMEM((m, STATE_LANES), F32)
    nbw = WIDTH // LANES
    seg_spec = lambda seg: pl.BlockSpec((t_len, m, LANES), lambda b: (0, 0, seg * nbw + b))
    return pl.pallas_call(
        kern,
        grid=(N_BLOCKS,),
        in_specs=[seg_spec(0),
                  state_spec, state_spec,
                  row_spec, row_spec, row_spec,
                  mat_spec, mat_spec, mat_spec, mat_spec,
                  pl.BlockSpec((1, 1, LANES), blk3),
                  seg_spec(2), seg_spec(3), seg_spec(4), seg_spec(5),
                  pl.BlockSpec((n_seq, LANES), lambda b: (0, b)),
                  pl.BlockSpec((n_seq, LANES), lambda b: (0, b)),
                  pl.BlockSpec((3, LANES), lambda b: (0, b))],
        out_specs=[seg_spec(0), state_spec, state_spec, seg_spec(0),
                   pl.BlockSpec((2, m, LANES), lambda b: (0, 0, b))],
        out_shape=[jax.ShapeDtypeStruct((t_len, m, WIDTH), BF16),
                   jax.ShapeDtypeStruct((N_BLOCKS, n_seq, STATE_LANES), F32),
                   jax.ShapeDtypeStruct((N_BLOCKS, n_seq, STATE_LANES), F32),
                   jax.ShapeDtypeStruct((t_len, m, WIDTH), BF16),
                   jax.ShapeDtypeStruct((2, m, WIDTH), F32)],
        scratch_shapes=[pltpu.VMEM((m, t_len * LANES), BF16),
                        end_mat, end_mat, end_mat, end_mat,
                        pltpu.VMEM((t_len * LANES, 2 * LANES), BF16),
                        chunk_state, chunk_state, chunk_state, chunk_state],
        compiler_params=_cparams(1),
        name="s5_scan",
    )(proj3, h0_re, h0_im, *s5p, proj3, proj3, proj3, proj3, buf0, buf1, conv_w)


def _glu_kernel(ya_ref, sza_ref, w_ref, b_ref, o_ref):
    ya = ya_ref[...]
    glu = jnp.dot(ya, w_ref[...], preferred_element_type=F32) + b_ref[...]
    o_ref[...] = (ya.astype(F32) * _sigmoid(glu) * sza_ref[...].astype(F32)).astype(BF16)


def _glu_cast_kernel(ya_ref, sza_ref, w_ref, b_ref, o_ref, wbf_ref, *, tn):
    w = w_ref[...].astype(BF16)
    wbf_ref[...] = w
    glu = jnp.dot(ya_ref[...], w, preferred_element_type=F32) + b_ref[...]
    ya = ya_ref[:, pl.ds(pl.multiple_of(pl.program_id(0) * tn, tn), tn)].astype(F32)
    o_ref[...] = (ya * _sigmoid(glu) * sza_ref[...].astype(F32)).astype(BF16)


def _glu(ya, proj, w_glu, b_glu, tm=1024, tn=512):
    rows = ya.shape[0]
    out_shape = jax.ShapeDtypeStruct((rows, WIDTH), BF16)
    if w_glu.dtype == BF16:
        return pl.pallas_call(
            _glu_kernel,
            grid=(rows // tm,),
            in_specs=[pl.BlockSpec((tm, WIDTH), lambda i: (i, 0)),
                      pl.BlockSpec((tm, WIDTH), lambda i: (i, 1)),
                      pl.BlockSpec((WIDTH, WIDTH), lambda i: (0, 0), pipeline_mode=pl.Buffered(1)),
                      pl.BlockSpec((1, WIDTH), lambda i: (0, 0))],
            out_specs=pl.BlockSpec((tm, WIDTH), lambda i: (i, 0)),
            out_shape=out_shape,
            compiler_params=_cparams(1),
            name="glu_gate",
        )(ya, proj, w_glu, b_glu.reshape(1, WIDTH))
    nb = WIDTH // tn
    w_spec = pl.BlockSpec((WIDTH, tn), lambda j: (0, j))
    return pl.pallas_call(
        functools.partial(_glu_cast_kernel, tn=tn),
        grid=(nb,),
        in_specs=[pl.BlockSpec((rows, WIDTH), lambda j: (0, 0)),
                  pl.BlockSpec((rows, tn), lambda j: (0, nb + j)),
                  w_spec,
                  pl.BlockSpec((1, tn), lambda j: (0, j))],
        out_specs=[pl.BlockSpec((rows, tn), lambda j: (0, j)), w_spec],
        out_shape=[out_shape, jax.ShapeDtypeStruct((WIDTH, WIDTH), BF16)],
        compiler_params=_cparams(1),
        name="glu_gate_cast",
    )(ya, proj, w_glu, b_glu.reshape(1, WIDTH))


PERM_ROWS = 512


def _to_natural(pm, t_len):
    rows, n = pm.shape
    c = rows // t_len
    cg = PERM_ROWS // t_len
    r_nat = lax.broadcasted_iota(jnp.int32, (PERM_ROWS, PERM_ROWS), 0)
    r_in = lax.broadcasted_iota(jnp.int32, (PERM_ROWS, PERM_ROWS), 1)
    src = (r_nat & (t_len - 1)) * cg + lax.shift_right_logical(r_nat, t_len.bit_length() - 1)
    perm = jnp.where(r_in == src, 1.0, 0.0).astype(BF16)
    pm3 = pm.reshape(t_len, c, n)
    out = []
    for g in range(c // cg):
        grp = pm3[:, g * cg:(g + 1) * cg, :].reshape(PERM_ROWS, n).astype(BF16)
        out.append(jnp.dot(perm, grp, preferred_element_type=F32).astype(BF16))
    return out[0] if len(out) == 1 else jnp.concatenate(out, axis=0)


def _gated_merge(a_ref, b_ref, wa, wb, sga_ref, sgb_ref, o_ref, natural):
    t_len, c, _ = a_ref.shape
    rows = t_len * c
    pa = jnp.dot(a_ref[...].reshape(rows, WIDTH), wa, preferred_element_type=F32)
    pb = jnp.dot(b_ref[...].reshape(rows, WIDTH), wb, preferred_element_type=F32)
    tn = pa.shape[1]
    pm = sga_ref[...].reshape(rows, tn) * pa + sgb_ref[...].reshape(rows, tn) * pb
    o_ref[...] = _to_natural(pm, t_len) if natural else pm.astype(BF16)


def _merge_kernel(a_ref, b_ref, wa_ref, wb_ref, sga_ref, sgb_ref, o_ref, *, natural):
    _gated_merge(a_ref, b_ref, wa_ref[...], wb_ref[...], sga_ref, sgb_ref, o_ref, natural)


def _merge_cast_kernel(a_ref, b_ref, wa_ref, wb_ref, sga_ref, sgb_ref, o_ref, wabf_ref, wbbf_ref, *, natural):
    wa = wa_ref[...].astype(BF16)
    wb = wb_ref[...].astype(BF16)
    wabf_ref[...] = wa
    wbbf_ref[...] = wb
    _gated_merge(a_ref, b_ref, wa, wb, sga_ref, sgb_ref, o_ref, natural)


def _merge(a_in3, b_in3, proj3, w_pa, w_pb, c, natural):
    t_len, m, _ = a_in3.shape
    cast = w_pa.dtype != BF16
    tn = 512 if cast else 1024
    nb = WIDTH // tn
    assert not cast or c == m
    assert not natural or (PERM_ROWS % t_len == 0 and c % (PERM_ROWS // t_len) == 0)
    w_spec = pl.BlockSpec((WIDTH, tn), lambda i, j: (0, j))
    out_spec = pl.BlockSpec((t_len * c, tn), lambda i, j: (i, j))
    out_shape = jax.ShapeDtypeStruct((t_len * m, D_MODEL), BF16)
    w_shape = jax.ShapeDtypeStruct((WIDTH, D_MODEL), BF16)
    return pl.pallas_call(
        functools.partial(_merge_cast_kernel if cast else _merge_kernel, natural=natural),
        grid=(m // c, D_MODEL // tn),
        in_specs=[pl.BlockSpec((t_len, c, WIDTH), lambda i, j: (0, i, 0)),
                  pl.BlockSpec((t_len, c, WIDTH), lambda i, j: (0, i, 0)),
                  w_spec, w_spec,
                  pl.BlockSpec((t_len, c, tn), lambda i, j: (0, i, 6 * nb + j)),
                  pl.BlockSpec((t_len, c, tn), lambda i, j: (0, i, 8 * nb + j))],
        out_specs=[out_spec, w_spec, w_spec] if cast else out_spec,
        out_shape=[out_shape, w_shape, w_shape] if cast else out_shape,
        compiler_params=_cparams(2),
        name="merge_cast" if cast else "merge",
    )(a_in3, b_in3, w_pa, w_pb, proj3, proj3)


def _outproj_kernel(m_ref, w_ref, x_ref, gate_ref, fg_ref, o_ref, *rest, tn, tiles_per_seq):
    j = pl.program_id(1)
    w = w_ref[...]
    if len(rest) == 2:
        wbf_ref, hres = rest
        w = w.astype(BF16)
        wbf_ref[...] = w
    else:
        (hres,) = rest
    part = jnp.dot(m_ref[...], w, preferred_element_type=F32)
    col = pl.ds(pl.multiple_of(j * tn, tn), tn)
    if tiles_per_seq:
        gate = gate_ref[pl.ds(pl.program_id(0) // tiles_per_seq, 1), :]
        hres[:, col] = x_ref[...] + gate * part
    else:
        n_rows = gate_ref.shape[0]
        for r in range(0, hres.shape[0], n_rows):
            hres[r:r + n_rows, col] = x_ref[r:r + n_rows, :] + gate_ref[...] * part[r:r + n_rows, :]

    @pl.when(j == pl.num_programs(1) - 1)
    def _():
        for r in range(0, hres.shape[0], LANES):
            h = hres[r:r + LANES, :]
            o_ref[r:r + LANES, :] = h * lax.rsqrt(jnp.mean(h * h, axis=-1, keepdims=True) + EPS) * fg_ref[...]


def _outproj(merged, w_o, x2d, gate, final_g, tm, tiles_per_seq):
    rows = x2d.shape[0]
    cast = w_o.dtype != BF16
    tn = 256 if cast else 1024
    assert tiles_per_seq or tm % gate.shape[0] == 0
    assert not cast or tm == rows
    w_spec = pl.BlockSpec((D_MODEL, tn), lambda i, j: (0, j))
    out_spec = pl.BlockSpec((tm, D_MODEL), lambda i, j: (i, 0))
    out_shape = jax.ShapeDtypeStruct((rows, D_MODEL), F32)
    return pl.pallas_call(
        functools.partial(_outproj_kernel, tn=tn, tiles_per_seq=tiles_per_seq),
        grid=(rows // tm, D_MODEL // tn),
        in_specs=[pl.BlockSpec((tm, D_MODEL), lambda i, j: (i, 0)),
                  w_spec,
                  pl.BlockSpec((tm, tn), lambda i, j: (i, j)),
                  pl.BlockSpec((gate.shape[0], tn), lambda i, j: (0, j)),
                  pl.BlockSpec((1, D_MODEL), lambda i, j: (0, 0))],
        out_specs=[out_spec, w_spec] if cast else out_spec,
        out_shape=[out_shape, jax.ShapeDtypeStruct((D_MODEL, D_MODEL), BF16)] if cast else out_shape,
        scratch_shapes=[pltpu.VMEM((tm, D_MODEL), F32)],
        compiler_params=_cparams(2, V7X_VMEM_BYTES - 4 * MIB),
        name="outproj_cast" if cast else "outproj",
    )(merged, w_o, x2d, gate, final_g.reshape(1, D_MODEL))


def _state_to_blocks(state):
    n = state.shape[0]
    re = state[..., 0].reshape(n, N_BLOCKS, STATE_LANES).transpose(1, 0, 2)
    im = state[..., 1].reshape(n, N_BLOCKS, STATE_LANES).transpose(1, 0, 2)
    return re, im


def _blocks_to_state(re, im):
    n = re.shape[1]
    re = re.transpose(1, 0, 2).reshape(n, N_GROUPS, N_STATE)
    im = im.transpose(1, 0, 2).reshape(n, N_GROUPS, N_STATE)
    return jnp.stack([re, im], axis=-1)


def _group(x2d, t_len, n_seq, ct, tm, mod, ssm0, conv0, p):
    m = x2d.shape[0] // t_len
    cps = m // n_seq
    natural = cps > 1
    tiles_per_seq = cps // ct if natural else 0
    shift, scale, gate = mod
    xn = _norm_mod(x2d, t_len, p["norm_g"], scale, shift, ct, tiles_per_seq)
    xn = xn.reshape(t_len * m, D_MODEL)
    if "w_in_bf" in p:
        proj = _inproj(xn, p["w_in_bf"], tm)
    else:
        proj, p["w_in_bf"] = _inproj_cast(xn, p["w_in"])
    proj3 = proj.reshape(t_len, m, IN_COLS)
    h0_re, h0_im = _state_to_blocks(ssm0)
    ya3, hr, hi, b_in3, v_last = _s5(proj3, h0_re, h0_im, p["s5"], conv0[:, 0], conv0[:, 1], p["conv_w"],
                                     t_len, m, n_seq)
    ya = ya3.reshape(t_len * m, WIDTH)
    if "w_glu_bf" in p:
        a_in = _glu(ya, proj, p["w_glu_bf"], p["b_glu"])
    else:
        a_in, p["w_glu_bf"] = _glu(ya, proj, p["w_glu"], p["b_glu"])
    a_in3 = a_in.reshape(t_len, m, WIDTH)
    if "w_pa_bf" in p:
        merged = _merge(a_in3, b_in3, proj3, p["w_pa_bf"], p["w_pb_bf"], tm // t_len, natural)
    else:
        merged, p["w_pa_bf"], p["w_pb_bf"] = _merge(a_in3, b_in3, proj3, p["w_pa"], p["w_pb"], m, natural)
    if "w_o_bf" in p:
        out = _outproj(merged, p["w_o_bf"], x2d, gate, p["final_g"], ct * t_len, tiles_per_seq)
    else:
        out, p["w_o_bf"] = _outproj(merged, p["w_o"], x2d, gate, p["final_g"], ct * t_len, tiles_per_seq)
    conv_new = v_last[:, cps - 1::cps, :].transpose(1, 0, 2)
    return out, _blocks_to_state(hr, hi), conv_new


def kernel(x_prompt, x_sample, state_ssm, state_conv, c_prompt, c_sample, norm_g, w_ada, b_ada, w_in, lam_re, lam_im, log_dt, b_re, b_im, c_re, c_im, d_skip, w_glu, b_glu, w_pa, conv_w, w_pb, w_o, final_g):
    depth = norm_g.shape[0]
    assert depth == 1
    n_p, seq, _ = x_prompt.shape
    n_s, dec, _ = x_sample.shape
    t_p = 16
    assert seq % t_p == 0

    l = 0
    p = dict(norm_g=norm_g[l], w_in=w_in[l], conv_w=conv_w[l], w_glu=w_glu[l], b_glu=b_glu[l],
             w_pa=w_pa[l], w_pb=w_pb[l], w_o=w_o[l], final_g=final_g,
             s5=_s5_params(lam_re[l], lam_im[l], log_dt[l], b_re[l], b_im[l], c_re[l], c_im[l], d_skip[l]))

    n_c = n_p + n_s
    pad = (-n_c) % 16
    c_all = jnp.concatenate([c_prompt, c_sample, jnp.zeros((pad, D_MODEL), F32)], axis=0)
    mod = _ada(c_all, w_ada[l], b_ada[l])
    mod_p = tuple(mod[:n_p, k * D_MODEL:(k + 1) * D_MODEL] for k in range(3))
    mod_s = tuple(mod[n_p:n_c, k * D_MODEL:(k + 1) * D_MODEL] for k in range(3))

    x_s = x_sample.transpose(1, 0, 2).reshape(dec * n_s, D_MODEL)
    y_s, ssm_s, conv_s = _group(x_s, dec, n_s, n_s, dec * n_s, mod_s, state_ssm[l], state_conv[l], p)
    y_sample = y_s.reshape(dec, n_s, D_MODEL).transpose(1, 0, 2)

    ssm_p0 = jnp.zeros((n_p, N_GROUPS, N_STATE, 2), F32)
    conv_p0 = jnp.zeros((n_p, 2, WIDTH), F32)
    y_p, ssm_p, conv_p = _group(x_prompt.reshape(n_p * seq, D_MODEL), t_p, n_p, 32, 1024,
                                mod_p, ssm_p0, conv_p0, p)
    y_prompt = y_p.reshape(n_p, seq, D_MODEL)

    return (y_prompt, y_sample, ssm_p[None], conv_p[None], ssm_s[None], conv_s[None])
```

```python
import functools

import jax
import jax.numpy as jnp
from jax import lax
from jax.experimental import pallas as pl
from jax.experimental.pallas import tpu as pltpu

F32 = jnp.float32
BF16 = jnp.bfloat16

D_MODEL = 4096
WIDTH = 2048
N_GROUPS = 128
GROUP = 16
N_STATE = 64
GROUPS_PER_BLOCK = 8
N_BLOCKS = N_GROUPS // GROUPS_PER_BLOCK
STATE_LANES = GROUPS_PER_BLOCK * N_STATE
LANES = 128
IN_COLS = 10 * WIDTH
EPS = 1e-6
MIB = 1024 * 1024
V7X_VMEM_BYTES = 64 * MIB
VMEM_LIMIT = V7X_VMEM_BYTES - 8 * MIB
NT_DIMS = (((1,), (1,)), ((), ()))


def _cparams(n_axes, vmem_limit=VMEM_LIMIT):
    return pltpu.CompilerParams(dimension_semantics=("arbitrary",) * n_axes,
                                vmem_limit_bytes=vmem_limit)


def _sigmoid(x):
    return jax.nn.sigmoid(x)


def _ada_kernel(c_ref, w_ref, b_ref, o_ref):
    c = c_ref[...]
    a = (c * _sigmoid(c)).astype(BF16)
    o_ref[...] = jnp.dot(a, w_ref[...].astype(BF16), preferred_element_type=F32) + b_ref[...]


def _ada(c_all, w_ada, b_ada, tn=512):
    rows = c_all.shape[0]
    n_out = w_ada.shape[1]
    return pl.pallas_call(
        _ada_kernel,
        grid=(n_out // tn,),
        in_specs=[pl.BlockSpec((rows, D_MODEL), lambda j: (0, 0)),
                  pl.BlockSpec((D_MODEL, tn), lambda j: (0, j)),
                  pl.BlockSpec((1, tn), lambda j: (0, j))],
        out_specs=pl.BlockSpec((rows, tn), lambda j: (0, j)),
        out_shape=jax.ShapeDtypeStruct((rows, n_out), F32),
        compiler_params=_cparams(1),
        name="ada_mod",
    )(c_all, w_ada, b_ada.reshape(1, n_out))


def _norm_kernel(x_ref, g_ref, sc_ref, sh_ref, o_ref, *, t_len, ct, tiles_per_seq):
    def norm_mod(x, sc, sh):
        y = x * lax.rsqrt(jnp.mean(x * x, axis=-1, keepdims=True) + EPS) * g_ref[...]
        return (y * (1.0 + sc) + sh).astype(BF16)

    if not tiles_per_seq:
        for s in range(t_len):
            o_ref[s] = norm_mod(x_ref[s * ct:(s + 1) * ct, :], sc_ref[...], sh_ref[...])
        return

    n = pl.program_id(0) // tiles_per_seq
    y = norm_mod(x_ref[...], sc_ref[pl.ds(n, 1), :], sh_ref[pl.ds(n, 1), :])
    r = t_len * ct
    r_out = lax.broadcasted_iota(jnp.int32, (r, r), 0)
    r_in = lax.broadcasted_iota(jnp.int32, (r, r), 1)
    src_row = (r_out & (ct - 1)) * t_len + lax.shift_right_logical(r_out, ct.bit_length() - 1)
    perm = jnp.where(r_in == src_row, 1.0, 0.0).astype(BF16)
    yp = jnp.dot(perm, y, preferred_element_type=F32)
    o_ref[...] = yp.reshape(t_len, ct, D_MODEL).astype(BF16)


def _norm_mod(x2d, t_len, norm_g, scale, shift, ct, tiles_per_seq):
    assert ct & (ct - 1) == 0
    r = t_len * ct
    m = x2d.shape[0] // t_len
    assert tiles_per_seq or ct == m == scale.shape[0]
    mod_spec = pl.BlockSpec((scale.shape[0], D_MODEL), lambda i: (0, 0))
    return pl.pallas_call(
        functools.partial(_norm_kernel, t_len=t_len, ct=ct, tiles_per_seq=tiles_per_seq),
        grid=(m // ct,),
        in_specs=[pl.BlockSpec((r, D_MODEL), lambda i: (i, 0)),
                  pl.BlockSpec((1, D_MODEL), lambda i: (0, 0)),
                  mod_spec, mod_spec],
        out_specs=pl.BlockSpec((t_len, ct, D_MODEL), lambda i: (0, i, 0)),
        out_shape=jax.ShapeDtypeStruct((t_len, m, D_MODEL), BF16),
        compiler_params=_cparams(1),
        name="norm_mod",
    )(x2d, norm_g.reshape(1, D_MODEL), scale, shift)


def _activate(acc, seg):
    is_silu = jnp.logical_or(seg == 1, seg == 5)
    is_sig = seg >= 6
    sig = _sigmoid(acc)
    return jnp.where(is_sig, sig, jnp.where(is_silu, acc * sig, acc)).astype(BF16)


def _inproj_kernel(x_ref, w_ref, o_ref, *, tn):
    acc = jnp.dot(x_ref[...], w_ref[...], preferred_element_type=F32)
    o_ref[...] = _activate(acc, pl.program_id(1) // (WIDTH // tn))


def _inproj_cast_kernel(x_ref, w_ref, o_ref, wbf_ref, *, tn):
    w = w_ref[...].astype(BF16)
    wbf_ref[...] = w
    acc = jnp.dot(x_ref[...], w, preferred_element_type=F32)
    o_ref[...] = _activate(acc, pl.program_id(0) // (WIDTH // tn))


def _inproj_cast(xn, w_in, tn=1024):
    rows = xn.shape[0]
    return pl.pallas_call(
        functools.partial(_inproj_cast_kernel, tn=tn),
        grid=(IN_COLS // tn,),
        in_specs=[pl.BlockSpec((rows, D_MODEL), lambda j: (0, 0)),
                  pl.BlockSpec((D_MODEL, tn), lambda j: (0, j))],
        out_specs=[pl.BlockSpec((rows, tn), lambda j: (0, j)),
                   pl.BlockSpec((D_MODEL, tn), lambda j: (0, j))],
        out_shape=[jax.ShapeDtypeStruct((rows, IN_COLS), BF16),
                   jax.ShapeDtypeStruct((D_MODEL, IN_COLS), BF16)],
        compiler_params=_cparams(1, V7X_VMEM_BYTES - 4 * MIB),
        name="inproj_cast",
    )(xn, w_in)


def _inproj(xn, w_in_bf, tm, tn=1024):
    rows = xn.shape[0]
    return pl.pallas_call(
        functools.partial(_inproj_kernel, tn=tn),
        grid=(rows // tm, IN_COLS // tn),
        in_specs=[pl.BlockSpec((tm, D_MODEL), lambda i, j: (i, 0)),
                  pl.BlockSpec((D_MODEL, tn), lambda i, j: (0, j))],
        out_specs=pl.BlockSpec((tm, tn), lambda i, j: (i, j)),
        out_shape=jax.ShapeDtypeStruct((rows, IN_COLS), BF16),
        compiler_params=_cparams(2),
        name="inproj",
    )(xn, w_in_bf)


def _cmul(ar, ai, br, bi):
    return ar * br - ai * bi, ar * bi + ai * br


def _short_conv(hb_ref, cb_ref, bb_ref, szb_ref, buf0_ref, buf1_ref, w_ref, bin_ref, v_ref, *, t_len, m, n_seq):
    cps = m // n_seq

    def prev_chunk(v, first_rows):
        if cps == 1:
            return first_rows
        out = pltpu.roll(v, 1, 0)
        rid = lax.broadcasted_iota(jnp.int32, v.shape, 0)
        for n in range(n_seq):
            out = jnp.where(rid == n * cps, first_rows[n:n + 1, :], out)
        return out

    def v_of(s):
        return hb_ref[s].astype(F32) * cb_ref[s].astype(F32)

    vm2 = prev_chunk(v_of(t_len - 2), buf0_ref[...])
    vm1 = prev_chunk(v_of(t_len - 1), buf1_ref[...])
    for s in range(t_len):
        v0 = v_of(s)
        conv = vm2 * w_ref[0:1, :] + vm1 * w_ref[1:2, :] + v0 * w_ref[2:3, :]
        bin_ref[s] = (bb_ref[s].astype(F32) * conv * szb_ref[s].astype(F32)).astype(BF16)
        if s >= t_len - 2:
            v_ref[s - (t_len - 2)] = v0
        vm2, vm1 = vm1, v0


def _s5_kernel(u_ref, h0r_ref, h0i_ref, lr_ref, li_ref, ldt_ref, btr_ref, bti_ref, ctr_ref, cti_ref, d_ref,
               hb_ref, cb_ref, bb_ref, szb_ref, buf0_ref, buf1_ref, cw_ref,
               y_ref, hr_out, hi_out, bin_ref, v_ref,
               ucat, bend_r, bend_i, cend_r, cend_i, kpair, hl_r, hl_i, hp_r, hp_i, *, t_len, m, n_seq):
    blk = pl.program_id(0)
    _short_conv(hb_ref, cb_ref, bb_ref, szb_ref, buf0_ref, buf1_ref, cw_ref, bin_ref, v_ref,
                t_len=t_len, m=m, n_seq=n_seq)

    @pl.when(blk == 0)
    def _():
        for ref in (bend_r, bend_i, cend_r, cend_i, kpair):
            ref[...] = jnp.zeros(ref.shape, ref.dtype)

    lam_r = lr_ref[0]
    lam_i = li_ref[0]
    dt = jnp.exp(ldt_ref[0])
    mag = jnp.exp(lam_r * dt)
    lb_r = mag * jnp.cos(lam_i * dt)
    lb_i = mag * jnp.sin(lam_i * dt)
    den = lam_r * lam_r + lam_i * lam_i
    nr = lb_r - 1.0
    co_r = (nr * lam_r + lb_i * lam_i) / den
    co_i = (lb_i * lam_r - nr * lam_i) / den
    bb_r, bb_i = _cmul(co_r, co_i, btr_ref[0], bti_ref[0])
    c_r = ctr_ref[0]
    c_i = cti_ref[0]
    pw = [(jnp.ones_like(lb_r), jnp.zeros_like(lb_r))]
    for _ in range(t_len):
        pw.append(_cmul(pw[-1][0], pw[-1][1], lb_r, lb_i))
    ar, ai = pw[t_len]

    lane = lax.broadcasted_iota(jnp.int32, (GROUP, LANES), 1)
    lo = lane < N_STATE

    def pair_tile(x, q):
        slab = x[:, q * LANES:(q + 1) * LANES]
        return jnp.concatenate([jnp.where(lo, slab, 0.0), jnp.where(lo, 0.0, slab)], axis=0).astype(BF16)

    grp_row = lax.broadcasted_iota(jnp.int32, (LANES, STATE_LANES), 0) // GROUP
    grp_lane = lax.broadcasted_iota(jnp.int32, (LANES, STATE_LANES), 1) // N_STATE
    same_group = grp_row == grp_lane

    def expand(x):
        return jnp.where(same_group, jnp.concatenate([x] * GROUPS_PER_BLOCK, axis=0), 0.0).astype(BF16)

    bbx_r = expand(bb_r)
    bbx_i = expand(bb_i)
    for s in range(t_len):
        ucat[:, s * LANES:(s + 1) * LANES] = u_ref[s]
        be_r, be_i = _cmul(pw[t_len - 1 - s][0], pw[t_len - 1 - s][1], bb_r, bb_i)
        ce_r, ce_i = _cmul(pw[s + 1][0], pw[s + 1][1], c_r, c_i)
        for q in range(4):
            r0 = s * LANES + q * 2 * GROUP
            rows = slice(r0, r0 + 2 * GROUP)
            cols = slice(q * LANES, (q + 1) * LANES)
            bend_r[rows, cols] = pair_tile(be_r, q)
            bend_i[rows, cols] = pair_tile(be_i, q)
            cend_r[rows, cols] = pair_tile(ce_r, q)
            cend_i[rows, cols] = pair_tile(-ce_i, q)
        cl_r, cl_i = _cmul(pw[s][0], pw[s][1], c_r, c_i)
        k_lag = (lax.dot_general(bbx_r, expand(cl_r), NT_DIMS, preferred_element_type=F32)
                 - lax.dot_general(bbx_i, expand(cl_i), NT_DIMS, preferred_element_type=F32))
        k_lag = k_lag.astype(BF16)
        kpair[(t_len - 1 - s) * LANES:(t_len - s) * LANES, LANES:] = k_lag
        if s <= t_len - 2:
            kpair[(t_len - 2 - s) * LANES:(t_len - 1 - s) * LANES, :LANES] = k_lag

    uc = ucat[...]
    hl_r[...] = jnp.dot(uc, bend_r[...], preferred_element_type=F32)
    hl_i[...] = jnp.dot(uc, bend_i[...], preferred_element_type=F32)

    cps = m // n_seq
    if cps == 1:
        h_r = h0r_ref[0]
        h_i = h0i_ref[0]
        hp_r[...] = h_r
        hp_i[...] = h_i
        hr_out[0] = ar * h_r - ai * h_i + hl_r[...]
        hi_out[0] = ar * h_i + ai * h_r + hl_i[...]
    else:
        def body(c, carry):
            new = []
            for n in range(n_seq):
                h_r, h_i = carry[n]
                row = pl.ds(n * cps + c, 1)
                hp_r[row, :] = h_r
                hp_i[row, :] = h_i
                new.append((ar * h_r - ai * h_i + hl_r[row, :], ar * h_i + ai * h_r + hl_i[row, :]))
            return tuple(new)

        init = tuple((h0r_ref[0, n:n + 1, :], h0i_ref[0, n:n + 1, :]) for n in range(n_seq))
        fin = lax.fori_loop(0, cps, body, init)
        for n in range(n_seq):
            hr_out[0, n:n + 1, :] = fin[n][0]
            hi_out[0, n:n + 1, :] = fin[n][1]

    ycar = (lax.dot_general(hp_r[...].astype(BF16), cend_r[...], NT_DIMS, preferred_element_type=F32)
            + lax.dot_general(hp_i[...].astype(BF16), cend_i[...], NT_DIMS, preferred_element_type=F32))
    dvec = d_ref[0]
    for t in range(0, t_len, 2):
        y2 = jnp.dot(ucat[:, :(t + 2) * LANES], kpair[(t_len - 2 - t) * LANES:, :],
                     preferred_element_type=F32) + ycar[:, t * LANES:(t + 2) * LANES]
        for k in range(2):
            y = y2[:, k * LANES:(k + 1) * LANES] + dvec * u_ref[t + k].astype(F32)
            y_ref[t + k] = jax.nn.gelu(y).astype(BF16)


def _s5_params(lam_re, lam_im, log_dt, b_re, b_im, c_re, c_im, d_skip):
    row = lambda x: x.astype(F32).reshape(N_BLOCKS, 1, STATE_LANES)
    ldt = jnp.broadcast_to(log_dt.astype(F32)[:, None], (N_GROUPS, N_STATE))
    bt = lambda x: (x.astype(F32).reshape(N_BLOCKS, GROUPS_PER_BLOCK, N_STATE, GROUP)
                    .transpose(0, 3, 1, 2).reshape(N_BLOCKS, GROUP, STATE_LANES))
    ct = lambda x: (x.astype(F32).reshape(N_BLOCKS, GROUPS_PER_BLOCK, GROUP, N_STATE)
                    .transpose(0, 2, 1, 3).reshape(N_BLOCKS, GROUP, STATE_LANES))
    return (row(lam_re), row(lam_im), row(ldt), bt(b_re), bt(b_im), ct(c_re), ct(c_im),
            d_skip.astype(F32).reshape(N_BLOCKS, 1, LANES))


def _s5(proj3, h0_re, h0_im, s5p, buf0, buf1, conv_w, t_len, m, n_seq):
    kern = functools.partial(_s5_kernel, t_len=t_len, m=m, n_seq=n_seq)
    blk3 = lambda b: (b, 0, 0)
    row_spec = pl.BlockSpec((1, 1, STATE_LANES), blk3)
    mat_spec = pl.BlockSpec((1, GROUP, STATE_LANES), blk3)
    state_spec = pl.BlockSpec((1, n_seq, STATE_LANES), blk3)
    end_mat = pltpu.VMEM((t_len * LANES, STATE_LANES), BF16)
    chunk_state = pltpu.VMEM((m, STATE_LANES), F32)
    nbw = WIDTH // LANES
    seg_spec = lambda seg: pl.BlockSpec((t_len, m, LANES), lambda b: (0, 0, seg * nbw + b))
    return pl.pallas_call(
        kern,
        grid=(N_BLOCKS,),
        in_specs=[seg_spec(0),
                  state_spec, state_spec,
                  row_spec, row_spec, row_spec,
                  mat_spec, mat_spec, mat_spec, mat_spec,
                  pl.BlockSpec((1, 1, LANES), blk3),
                  seg_spec(2), seg_spec(3), seg_spec(4), seg_spec(5),
                  pl.BlockSpec((n_seq, LANES), lambda b: (0, b)),
                  pl.BlockSpec((n_seq, LANES), lambda b: (0, b)),
                  pl.BlockSpec((3, LANES), lambda b: (0, b))],
        out_specs=[seg_spec(0), state_spec, state_spec, seg_spec(0),
                   pl.BlockSpec((2, m, LANES), lambda b: (0, 0, b))],
        out_shape=[jax.ShapeDtypeStruct((t_len, m, WIDTH), BF16),
                   jax.ShapeDtypeStruct((N_BLOCKS, n_seq, STATE_LANES), F32),
                   jax.ShapeDtypeStruct((N_BLOCKS, n_seq, STATE_LANES), F32),
                   jax.ShapeDtypeStruct((t_len, m, WIDTH), BF16),
                   jax.ShapeDtypeStruct((2, m, WIDTH), F32)],
        scratch_shapes=[pltpu.VMEM((m, t_len * LANES), BF16),
                        end_mat, end_mat, end_mat, end_mat,
                        pltpu.VMEM((t_len * LANES, 2 * LANES), BF16),
                        chunk_state, chunk_state, chunk_state, chunk_state],
        compiler_params=_cparams(1),
        name="s5_scan",
    )(proj3, h0_re, h0_im, *s5p, proj3, proj3, proj3, proj3, buf0, buf1, conv_w)


def _glu_kernel(ya_ref, sza_ref, w_ref, b_ref, o_ref):
    ya = ya_ref[...]
    glu = jnp.dot(ya, w_ref[...], preferred_element_type=F32) + b_ref[...]
    o_ref[...] = (ya.astype(F32) * _sigmoid(glu) * sza_ref[...].astype(F32)).astype(BF16)


def _glu_cast_kernel(ya_ref, sza_ref, w_ref, b_ref, o_ref, wbf_ref, *, tn):
    w = w_ref[...].astype(BF16)
    wbf_ref[...] = w
    glu = jnp.dot(ya_ref[...], w, preferred_element_type=F32) + b_ref[...]
    ya = ya_ref[:, pl.ds(pl.multiple_of(pl.program_id(0) * tn, tn), tn)].astype(F32)
    o_ref[...] = (ya * _sigmoid(glu) * sza_ref[...].astype(F32)).astype(BF16)


def _glu(ya, proj, w_glu, b_glu, tm=1024, tn=512):
    rows = ya.shape[0]
    out_shape = jax.ShapeDtypeStruct((rows, WIDTH), BF16)
    if w_glu.dtype == BF16:
        return pl.pallas_call(
            _glu_kernel,
            grid=(rows // tm,),
            in_specs=[pl.BlockSpec((tm, WIDTH), lambda i: (i, 0)),
                      pl.BlockSpec((tm, WIDTH), lambda i: (i, 1)),
                      pl.BlockSpec((WIDTH, WIDTH), lambda i: (0, 0), pipeline_mode=pl.Buffered(1)),
                      pl.BlockSpec((1, WIDTH), lambda i: (0, 0))],
            out_specs=pl.BlockSpec((tm, WIDTH), lambda i: (i, 0)),
            out_shape=out_shape,
            compiler_params=_cparams(1),
            name="glu_gate",
        )(ya, proj, w_glu, b_glu.reshape(1, WIDTH))
    nb = WIDTH // tn
    w_spec = pl.BlockSpec((WIDTH, tn), lambda j: (0, j))
    return pl.pallas_call(
        functools.partial(_glu_cast_kernel, tn=tn),
        grid=(nb,),
        in_specs=[pl.BlockSpec((rows, WIDTH), lambda j: (0, 0)),
                  pl.BlockSpec((rows, tn), lambda j: (0, nb + j)),
                  w_spec,
                  pl.BlockSpec((1, tn), lambda j: (0, j))],
        out_specs=[pl.BlockSpec((rows, tn), lambda j: (0, j)), w_spec],
        out_shape=[out_shape, jax.ShapeDtypeStruct((WIDTH, WIDTH), BF16)],
        compiler_params=_cparams(1),
        name="glu_gate_cast",
    )(ya, proj, w_glu, b_glu.reshape(1, WIDTH))


PERM_ROWS = 512


def _to_natural(pm, t_len):
    rows, n = pm.shape
    c = rows // t_len
    cg = PERM_ROWS // t_len
    r_nat = lax.broadcasted_iota(jnp.int32, (PERM_ROWS, PERM_ROWS), 0)
    r_in = lax.broadcasted_iota(jnp.int32, (PERM_ROWS, PERM_ROWS), 1)
    src = (r_nat & (t_len - 1)) * cg + lax.shift_right_logical(r_nat, t_len.bit_length() - 1)
    perm = jnp.where(r_in == src, 1.0, 0.0).astype(BF16)
    pm3 = pm.reshape(t_len, c, n)
    out = []
    for g in range(c // cg):
        grp = pm3[:, g * cg:(g + 1) * cg, :].reshape(PERM_ROWS, n).astype(BF16)
        out.append(jnp.dot(perm, grp, preferred_element_type=F32).astype(BF16))
    return out[0] if len(out) == 1 else jnp.concatenate(out, axis=0)


def _gated_merge(a_ref, b_ref, wa, wb, sga_ref, sgb_ref, o_ref, natural):
    t_len, c, _ = a_ref.shape
    rows = t_len * c
    pa = jnp.dot(a_ref[...].reshape(rows, WIDTH), wa, preferred_element_type=F32)
    pb = jnp.dot(b_ref[...].reshape(rows, WIDTH), wb, preferred_element_type=F32)
    tn = pa.shape[1]
    pm = sga_ref[...].reshape(rows, tn) * pa + sgb_ref[...].reshape(rows, tn) * pb
    o_ref[...] = _to_natural(pm, t_len) if natural else pm.astype(BF16)


def _merge_kernel(a_ref, b_ref, wa_ref, wb_ref, sga_ref, sgb_ref, o_ref, *, natural):
    _gated_merge(a_ref, b_ref, wa_ref[...], wb_ref[...], sga_ref, sgb_ref, o_ref, natural)


def _merge_cast_kernel(a_ref, b_ref, wa_ref, wb_ref, sga_ref, sgb_ref, o_ref, wabf_ref, wbbf_ref, *, natural):
    wa = wa_ref[...].astype(BF16)
    wb = wb_ref[...].astype(BF16)
    wabf_ref[...] = wa
    wbbf_ref[...] = wb
    _gated_merge(a_ref, b_ref, wa, wb, sga_ref, sgb_ref, o_ref, natural)


def _merge(a_in3, b_in3, proj3, w_pa, w_pb, c, natural):
    t_len, m, _ = a_in3.shape
    cast = w_pa.dtype != BF16
    tn = 512 if cast else 1024
    nb = WIDTH // tn
    assert not cast or c == m
    assert not natural or (PERM_ROWS % t_len == 0 and c % (PERM_ROWS // t_len) == 0)
    w_spec = pl.BlockSpec((WIDTH, tn), lambda i, j: (0, j))
    out_spec = pl.BlockSpec((t_len * c, tn), lambda i, j: (i, j))
    out_shape = jax.ShapeDtypeStruct((t_len * m, D_MODEL), BF16)
    w_shape = jax.ShapeDtypeStruct((WIDTH, D_MODEL), BF16)
    return pl.pallas_call(
        functools.partial(_merge_cast_kernel if cast else _merge_kernel, natural=natural),
        grid=(m // c, D_MODEL // tn),
        in_specs=[pl.BlockSpec((t_len, c, WIDTH), lambda i, j: (0, i, 0)),
                  pl.BlockSpec((t_len, c, WIDTH), lambda i, j: (0, i, 0)),
                  w_spec, w_spec,
                  pl.BlockSpec((t_len, c, tn), lambda i, j: (0, i, 6 * nb + j)),
                  pl.BlockSpec((t_len, c, tn), lambda i, j: (0, i, 8 * nb + j))],
        out_specs=[out_spec, w_spec, w_spec] if cast else out_spec,
        out_shape=[out_shape, w_shape, w_shape] if cast else out_shape,
        compiler_params=_cparams(2),
        name="merge_cast" if cast else "merge",
    )(a_in3, b_in3, w_pa, w_pb, proj3, proj3)


def _outproj_kernel(m_ref, w_ref, x_ref, gate_ref, fg_ref, o_ref, *rest, tn, tiles_per_seq):
    j = pl.program_id(1)
    w = w_ref[...]
    if len(rest) == 2:
        wbf_ref, hres = rest
        w = w.astype(BF16)
        wbf_ref[...] = w
    else:
        (hres,) = rest
    part = jnp.dot(m_ref[...], w, preferred_element_type=F32)
    col = pl.ds(pl.multiple_of(j * tn, tn), tn)
    if tiles_per_seq:
        gate = gate_ref[pl.ds(pl.program_id(0) // tiles_per_seq, 1), :]
        hres[:, col] = x_ref[...] + gate * part
    else:
        n_rows = gate_ref.shape[0]
        for r in range(0, hres.shape[0], n_rows):
            hres[r:r + n_rows, col] = x_ref[r:r + n_rows, :] + gate_ref[...] * part[r:r + n_rows, :]

    @pl.when(j == pl.num_programs(1) - 1)
    def _():
        for r in range(0, hres.shape[0], LANES):
            h = hres[r:r + LANES, :]
            o_ref[r:r + LANES, :] = h * lax.rsqrt(jnp.mean(h * h, axis=-1, keepdims=True) + EPS) * fg_ref[...]


def _outproj(merged, w_o, x2d, gate, final_g, tm, tiles_per_seq):
    rows = x2d.shape[0]
    cast = w_o.dtype != BF16
    tn = 512 if cast else 1024
    assert tiles_per_seq or tm % gate.shape[0] == 0
    assert not cast or tm == rows
    w_spec = pl.BlockSpec((D_MODEL, tn), lambda i, j: (0, j))
    out_spec = pl.BlockSpec((tm, D_MODEL), lambda i, j: (i, 0))
    out_shape = jax.ShapeDtypeStruct((rows, D_MODEL), F32)
    return pl.pallas_call(
        functools.partial(_outproj_kernel, tn=tn, tiles_per_seq=tiles_per_seq),
        grid=(rows // tm, D_MODEL // tn),
        in_specs=[pl.BlockSpec((tm, D_MODEL), lambda i, j: (i, 0)),
                  w_spec,
                  pl.BlockSpec((tm, tn), lambda i, j: (i, j)),
                  pl.BlockSpec((gate.shape[0], tn), lambda i, j: (0, j)),
                  pl.BlockSpec((1, D_MODEL), lambda i, j: (0, 0))],
        out_specs=[out_spec, w_spec] if cast else out_spec,
        out_shape=[out_shape, jax.ShapeDtypeStruct((D_MODEL, D_MODEL), BF16)] if cast else out_shape,
        scratch_shapes=[pltpu.VMEM((tm, D_MODEL), F32)],
        compiler_params=_cparams(2, V7X_VMEM_BYTES - 4 * MIB),
        name="outproj_cast" if cast else "outproj",
    )(merged, w_o, x2d, gate, final_g.reshape(1, D_MODEL))


def _state_to_blocks(state):
    n = state.shape[0]
    re = state[..., 0].reshape(n, N_BLOCKS, STATE_LANES).transpose(1, 0, 2)
    im = state[..., 1].reshape(n, N_BLOCKS, STATE_LANES).transpose(1, 0, 2)
    return re, im


def _blocks_to_state(re, im):
    n = re.shape[1]
    re = re.transpose(1, 0, 2).reshape(n, N_GROUPS, N_STATE)
    im = im.transpose(1, 0, 2).reshape(n, N_GROUPS, N_STATE)
    return jnp.stack([re, im], axis=-1)


def _group(x2d, t_len, n_seq, ct, tm, mod, ssm0, conv0, p):
    m = x2d.shape[0] // t_len
    cps = m // n_seq
    natural = cps > 1
    tiles_per_seq = cps // ct if natural else 0
    shift, scale, gate = mod
    xn = _norm_mod(x2d, t_len, p["norm_g"], scale, shift, ct, tiles_per_seq)
    xn = xn.reshape(t_len * m, D_MODEL)
    if "w_in_bf" in p:
        proj = _inproj(xn, p["w_in_bf"], tm)
    else:
        proj, p["w_in_bf"] = _inproj_cast(xn, p["w_in"])
    proj3 = proj.reshape(t_len, m, IN_COLS)
    h0_re, h0_im = _state_to_blocks(ssm0)
    ya3, hr, hi, b_in3, v_last = _s5(proj3, h0_re, h0_im, p["s5"], conv0[:, 0], conv0[:, 1], p["conv_w"],
                                     t_len, m, n_seq)
    ya = ya3.reshape(t_len * m, WIDTH)
    if "w_glu_bf" in p:
        a_in = _glu(ya, proj, p["w_glu_bf"], p["b_glu"])
    else:
        a_in, p["w_glu_bf"] = _glu(ya, proj, p["w_glu"], p["b_glu"])
    a_in3 = a_in.reshape(t_len, m, WIDTH)
    if "w_pa_bf" in p:
        merged = _merge(a_in3, b_in3, proj3, p["w_pa_bf"], p["w_pb_bf"], tm // t_len, natural)
    else:
        merged, p["w_pa_bf"], p["w_pb_bf"] = _merge(a_in3, b_in3, proj3, p["w_pa"], p["w_pb"], m, natural)
    if "w_o_bf" in p:
        out = _outproj(merged, p["w_o_bf"], x2d, gate, p["final_g"], ct * t_len, tiles_per_seq)
    else:
        out, p["w_o_bf"] = _outproj(merged, p["w_o"], x2d, gate, p["final_g"], ct * t_len, tiles_per_seq)
    conv_new = v_last[:, cps - 1::cps, :].transpose(1, 0, 2)
    return out, _blocks_to_state(hr, hi), conv_new


def kernel(x_prompt, x_sample, state_ssm, state_conv, c_prompt, c_sample, norm_g, w_ada, b_ada, w_in, lam_re, lam_im, log_dt, b_re, b_im, c_re, c_im, d_skip, w_glu, b_glu, w_pa, conv_w, w_pb, w_o, final_g):
    depth = norm_g.shape[0]
    assert depth == 1
    n_p, seq, _ = x_prompt.shape
    n_s, dec, _ = x_sample.shape
    t_p = 16
    assert seq % t_p == 0

    l = 0
    p = dict(norm_g=norm_g[l], w_in=w_in[l], conv_w=conv_w[l], w_glu=w_glu[l], b_glu=b_glu[l],
             w_pa=w_pa[l], w_pb=w_pb[l], w_o=w_o[l], final_g=final_g,
             s5=_s5_params(lam_re[l], lam_im[l], log_dt[l], b_re[l], b_im[l], c_re[l], c_im[l], d_skip[l]))

    n_c = n_p + n_s
    pad = (-n_c) % 16
    c_all = jnp.concatenate([c_prompt, c_sample, jnp.zeros((pad, D_MODEL), F32)], axis=0)
    mod = _ada(c_all, w_ada[l], b_ada[l])
    mod_p = tuple(mod[:n_p, k * D_MODEL:(k + 1) * D_MODEL] for k in range(3))
    mod_s = tuple(mod[n_p:n_c, k * D_MODEL:(k + 1) * D_MODEL] for k in range(3))

    x_s = x_sample.transpose(1, 0, 2).reshape(dec * n_s, D_MODEL)
    y_s, ssm_s, conv_s = _group(x_s, dec, n_s, n_s, dec * n_s, mod_s, state_ssm[l], state_conv[l], p)
    y_sample = y_s.reshape(dec, n_s, D_MODEL).transpose(1, 0, 2)

    ssm_p0 = jnp.zeros((n_p, N_GROUPS, N_STATE, 2), F32)
    conv_p0 = jnp.zeros((n_p, 2, WIDTH), F32)
    y_p, ssm_p, conv_p = _group(x_prompt.reshape(n_p * seq, D_MODEL), t_p, n_p, 32, 1024,
                                mod_p, ssm_p0, conv_p0, p)
    y_prompt = y_p.reshape(n_p, seq, D_MODEL)

    return (y_prompt, y_sample, ssm_p[None], conv_p[None], ssm_s[None], conv_s[None])
```

```python
import functools

import jax
import jax.numpy as jnp
from jax import lax
from jax.experimental import pallas as pl
from jax.experimental.pallas import tpu as pltpu

F32 = jnp.float32
BF16 = jnp.bfloat16

D_MODEL = 4096
WIDTH = 2048
N_GROUPS = 128
GROUP = 16
N_STATE = 64
GROUPS_PER_BLOCK = 8
N_BLOCKS = N_GROUPS // GROUPS_PER_BLOCK
STATE_LANES = GROUPS_PER_BLOCK * N_STATE
LANES = 128
TAIL_COLS = 256
TAIL_SPLIT = 8
IN_COLS = 10 * WIDTH
EPS = 1e-6
MIB = 1024 * 1024
V7X_VMEM_BYTES = 64 * MIB
VMEM_LIMIT = V7X_VMEM_BYTES - 8 * MIB
NT_DIMS = (((1,), (1,)), ((), ()))


def _cparams(n_axes, vmem_limit=VMEM_LIMIT):
    return pltpu.CompilerParams(dimension_semantics=("arbitrary",) * n_axes,
                                vmem_limit_bytes=vmem_limit)


def _sigmoid(x):
    return jax.nn.sigmoid(x)


def _ada_kernel(c_ref, w_ref, b_ref, o_ref):
    c = c_ref[...]
    a = (c * _sigmoid(c)).astype(BF16)
    o_ref[...] = jnp.dot(a, w_ref[...].astype(BF16), preferred_element_type=F32) + b_ref[...]


def _ada(c_all, w_ada, b_ada, tn=512):
    rows = c_all.shape[0]
    n_out = w_ada.shape[1]
    return pl.pallas_call(
        _ada_kernel,
        grid=(n_out // tn,),
        in_specs=[pl.BlockSpec((rows, D_MODEL), lambda j: (0, 0)),
                  pl.BlockSpec((D_MODEL, tn), lambda j: (0, j)),
                  pl.BlockSpec((1, tn), lambda j: (0, j))],
        out_specs=pl.BlockSpec((rows, tn), lambda j: (0, j)),
        out_shape=jax.ShapeDtypeStruct((rows, n_out), F32),
        compiler_params=_cparams(1),
        name="ada_mod",
    )(c_all, w_ada, b_ada.reshape(1, n_out))


def _norm_kernel(x_ref, g_ref, sc_ref, sh_ref, o_ref, *, t_len, ct, tiles_per_seq):
    def norm_mod(x, sc, sh):
        y = x * lax.rsqrt(jnp.mean(x * x, axis=-1, keepdims=True) + EPS) * g_ref[...]
        return (y * (1.0 + sc) + sh).astype(BF16)

    if not tiles_per_seq:
        for s in range(t_len):
            o_ref[s] = norm_mod(x_ref[s * ct:(s + 1) * ct, :], sc_ref[...], sh_ref[...])
        return

    n = pl.program_id(0) // tiles_per_seq
    y = norm_mod(x_ref[...], sc_ref[pl.ds(n, 1), :], sh_ref[pl.ds(n, 1), :])
    r = t_len * ct
    r_out = lax.broadcasted_iota(jnp.int32, (r, r), 0)
    r_in = lax.broadcasted_iota(jnp.int32, (r, r), 1)
    src_row = (r_out & (ct - 1)) * t_len + lax.shift_right_logical(r_out, ct.bit_length() - 1)
    perm = jnp.where(r_in == src_row, 1.0, 0.0).astype(BF16)
    yp = jnp.dot(perm, y, preferred_element_type=F32)
    o_ref[...] = yp.reshape(t_len, ct, D_MODEL).astype(BF16)


def _norm_mod(x2d, t_len, norm_g, scale, shift, ct, tiles_per_seq):
    assert ct & (ct - 1) == 0
    r = t_len * ct
    m = x2d.shape[0] // t_len
    assert tiles_per_seq or ct == m == scale.shape[0]
    mod_spec = pl.BlockSpec((scale.shape[0], D_MODEL), lambda i: (0, 0))
    return pl.pallas_call(
        functools.partial(_norm_kernel, t_len=t_len, ct=ct, tiles_per_seq=tiles_per_seq),
        grid=(m // ct,),
        in_specs=[pl.BlockSpec((r, D_MODEL), lambda i: (i, 0)),
                  pl.BlockSpec((1, D_MODEL), lambda i: (0, 0)),
                  mod_spec, mod_spec],
        out_specs=pl.BlockSpec((t_len, ct, D_MODEL), lambda i: (0, i, 0)),
        out_shape=jax.ShapeDtypeStruct((t_len, m, D_MODEL), BF16),
        compiler_params=_cparams(1),
        name="norm_mod",
    )(x2d, norm_g.reshape(1, D_MODEL), scale, shift)


def _activate(acc, seg):
    is_silu = jnp.logical_or(seg == 1, seg == 5)
    is_sig = seg >= 6
    sig = _sigmoid(acc)
    return jnp.where(is_sig, sig, jnp.where(is_silu, acc * sig, acc)).astype(BF16)


def _inproj_kernel(x_ref, w_ref, o_ref, *, tn):
    seg = pl.program_id(1) // (WIDTH // tn)
    tm = x_ref.shape[0]
    head = tn - TAIL_COLS
    acc = jnp.dot(x_ref[...], w_ref[:, :head], preferred_element_type=F32)
    o_ref[:, :head] = _activate(acc, seg)
    for r in range(0, tm, tm // TAIL_SPLIT):
        rows = slice(r, r + tm // TAIL_SPLIT)
        acc = jnp.dot(x_ref[rows, :], w_ref[:, head:], preferred_element_type=F32)
        o_ref[rows, head:] = _activate(acc, seg)


def _inproj_cast_kernel(x_ref, w_ref, o_ref, wbf_ref, *, tn):
    w = w_ref[...].astype(BF16)
    wbf_ref[...] = w
    acc = jnp.dot(x_ref[...], w, preferred_element_type=F32)
    o_ref[...] = _activate(acc, pl.program_id(0) // (WIDTH // tn))


def _inproj_cast(xn, w_in, tn=1024):
    rows = xn.shape[0]
    return pl.pallas_call(
        functools.partial(_inproj_cast_kernel, tn=tn),
        grid=(IN_COLS // tn,),
        in_specs=[pl.BlockSpec((rows, D_MODEL), lambda j: (0, 0)),
                  pl.BlockSpec((D_MODEL, tn), lambda j: (0, j))],
        out_specs=[pl.BlockSpec((rows, tn), lambda j: (0, j)),
                   pl.BlockSpec((D_MODEL, tn), lambda j: (0, j))],
        out_shape=[jax.ShapeDtypeStruct((rows, IN_COLS), BF16),
                   jax.ShapeDtypeStruct((D_MODEL, IN_COLS), BF16)],
        compiler_params=_cparams(1, V7X_VMEM_BYTES - 4 * MIB),
        name="inproj_cast",
    )(xn, w_in)


def _inproj(xn, w_in_bf, tm, tn=1024):
    rows = xn.shape[0]
    return pl.pallas_call(
        functools.partial(_inproj_kernel, tn=tn),
        grid=(rows // tm, IN_COLS // tn),
        in_specs=[pl.BlockSpec((tm, D_MODEL), lambda i, j: (i, 0)),
                  pl.BlockSpec((D_MODEL, tn), lambda i, j: (0, j))],
        out_specs=pl.BlockSpec((tm, tn), lambda i, j: (i, j)),
        out_shape=jax.ShapeDtypeStruct((rows, IN_COLS), BF16),
        compiler_params=_cparams(2),
        name="inproj",
    )(xn, w_in_bf)


def _cmul(ar, ai, br, bi):
    return ar * br - ai * bi, ar * bi + ai * br


def _short_conv(hb_ref, cb_ref, bb_ref, szb_ref, buf0_ref, buf1_ref, w_ref, bin_ref, v_ref, *, t_len, m, n_seq):
    cps = m // n_seq

    def prev_chunk(v, first_rows):
        if cps == 1:
            return first_rows
        out = pltpu.roll(v, 1, 0)
        rid = lax.broadcasted_iota(jnp.int32, v.shape, 0)
        for n in range(n_seq):
            out = jnp.where(rid == n * cps, first_rows[n:n + 1, :], out)
        return out

    def v_of(s):
        return hb_ref[s].astype(F32) * cb_ref[s].astype(F32)

    vm2 = prev_chunk(v_of(t_len - 2), buf0_ref[...])
    vm1 = prev_chunk(v_of(t_len - 1), buf1_ref[...])
    for s in range(t_len):
        v0 = v_of(s)
        conv = vm2 * w_ref[0:1, :] + vm1 * w_ref[1:2, :] + v0 * w_ref[2:3, :]
        bin_ref[s] = (bb_ref[s].astype(F32) * conv * szb_ref[s].astype(F32)).astype(BF16)
        if s >= t_len - 2:
            v_ref[s - (t_len - 2)] = v0
        vm2, vm1 = vm1, v0


def _s5_kernel(u_ref, h0r_ref, h0i_ref, lr_ref, li_ref, ldt_ref, btr_ref, bti_ref, ctr_ref, cti_ref, d_ref,
               hb_ref, cb_ref, bb_ref, szb_ref, buf0_ref, buf1_ref, cw_ref,
               y_ref, hr_out, hi_out, bin_ref, v_ref,
               ucat, bend_r, bend_i, cend_r, cend_i, kpair, hl_r, hl_i, hp_r, hp_i, *, t_len, m, n_seq):
    blk = pl.program_id(0)
    _short_conv(hb_ref, cb_ref, bb_ref, szb_ref, buf0_ref, buf1_ref, cw_ref, bin_ref, v_ref,
                t_len=t_len, m=m, n_seq=n_seq)

    @pl.when(blk == 0)
    def _():
        for ref in (bend_r, bend_i, cend_r, cend_i, kpair):
            ref[...] = jnp.zeros(ref.shape, ref.dtype)

    lam_r = lr_ref[0]
    lam_i = li_ref[0]
    dt = jnp.exp(ldt_ref[0])
    mag = jnp.exp(lam_r * dt)
    lb_r = mag * jnp.cos(lam_i * dt)
    lb_i = mag * jnp.sin(lam_i * dt)
    den = lam_r * lam_r + lam_i * lam_i
    nr = lb_r - 1.0
    co_r = (nr * lam_r + lb_i * lam_i) / den
    co_i = (lb_i * lam_r - nr * lam_i) / den
    bb_r, bb_i = _cmul(co_r, co_i, btr_ref[0], bti_ref[0])
    c_r = ctr_ref[0]
    c_i = cti_ref[0]
    pw = [(jnp.ones_like(lb_r), jnp.zeros_like(lb_r))]
    for _ in range(t_len):
        pw.append(_cmul(pw[-1][0], pw[-1][1], lb_r, lb_i))
    ar, ai = pw[t_len]

    lane = lax.broadcasted_iota(jnp.int32, (GROUP, LANES), 1)
    lo = lane < N_STATE

    def pair_tile(x, q):
        slab = x[:, q * LANES:(q + 1) * LANES]
        return jnp.concatenate([jnp.where(lo, slab, 0.0), jnp.where(lo, 0.0, slab)], axis=0).astype(BF16)

    grp_row = lax.broadcasted_iota(jnp.int32, (LANES, STATE_LANES), 0) // GROUP
    grp_lane = lax.broadcasted_iota(jnp.int32, (LANES, STATE_LANES), 1) // N_STATE
    same_group = grp_row == grp_lane

    def expand(x):
        return jnp.where(same_group, jnp.concatenate([x] * GROUPS_PER_BLOCK, axis=0), 0.0).astype(BF16)

    bbx_r = expand(bb_r)
    bbx_i = expand(bb_i)
    for s in range(t_len):
        ucat[:, s * LANES:(s + 1) * LANES] = u_ref[s]
        be_r, be_i = _cmul(pw[t_len - 1 - s][0], pw[t_len - 1 - s][1], bb_r, bb_i)
        ce_r, ce_i = _cmul(pw[s + 1][0], pw[s + 1][1], c_r, c_i)
        for q in range(4):
            r0 = s * LANES + q * 2 * GROUP
            rows = slice(r0, r0 + 2 * GROUP)
            cols = slice(q * LANES, (q + 1) * LANES)
            bend_r[rows, cols] = pair_tile(be_r, q)
            bend_i[rows, cols] = pair_tile(be_i, q)
            cend_r[rows, cols] = pair_tile(ce_r, q)
            cend_i[rows, cols] = pair_tile(-ce_i, q)
        cl_r, cl_i = _cmul(pw[s][0], pw[s][1], c_r, c_i)
        k_lag = (lax.dot_general(bbx_r, expand(cl_r), NT_DIMS, preferred_element_type=F32)
                 - lax.dot_general(bbx_i, expand(cl_i), NT_DIMS, preferred_element_type=F32))
        k_lag = k_lag.astype(BF16)
        kpair[(t_len - 1 - s) * LANES:(t_len - s) * LANES, LANES:] = k_lag
        if s <= t_len - 2:
            kpair[(t_len - 2 - s) * LANES:(t_len - 1 - s) * LANES, :LANES] = k_lag

    uc = ucat[...]
    hl_r[...] = jnp.dot(uc, bend_r[...], preferred_element_type=F32)
    hl_i[...] = jnp.dot(uc, bend_i[...], preferred_element_type=F32)

    cps = m // n_seq
    if cps == 1:
        h_r = h0r_ref[0]
        h_i = h0i_ref[0]
        hp_r[...] = h_r
        hp_i[...] = h_i
        hr_out[0] = ar * h_r - ai * h_i + hl_r[...]
        hi_out[0] = ar * h_i + ai * h_r + hl_i[...]
    else:
        def body(c, carry):
            new = []
            for n in range(n_seq):
                h_r, h_i = carry[n]
                row = pl.ds(n * cps + c, 1)
                hp_r[row, :] = h_r
                hp_i[row, :] = h_i
                new.append((ar * h_r - ai * h_i + hl_r[row, :], ar * h_i + ai * h_r + hl_i[row, :]))
            return tuple(new)

        init = tuple((h0r_ref[0, n:n + 1, :], h0i_ref[0, n:n + 1, :]) for n in range(n_seq))
        fin = lax.fori_loop(0, cps, body, init)
        for n in range(n_seq):
            hr_out[0, n:n + 1, :] = fin[n][0]
            hi_out[0, n:n + 1, :] = fin[n][1]

    ycar = (lax.dot_general(hp_r[...].astype(BF16), cend_r[...], NT_DIMS, preferred_element_type=F32)
            + lax.dot_general(hp_i[...].astype(BF16), cend_i[...], NT_DIMS, preferred_element_type=F32))
    dvec = d_ref[0]
    for t in range(0, t_len, 2):
        y2 = jnp.dot(ucat[:, :(t + 2) * LANES], kpair[(t_len - 2 - t) * LANES:, :],
                     preferred_element_type=F32) + ycar[:, t * LANES:(t + 2) * LANES]
        for k in range(2):
            y = y2[:, k * LANES:(k + 1) * LANES] + dvec * u_ref[t + k].astype(F32)
            y_ref[t + k] = jax.nn.gelu(y).astype(BF16)


def _s5_params(lam_re, lam_im, log_dt, b_re, b_im, c_re, c_im, d_skip):
    row = lambda x: x.astype(F32).reshape(N_BLOCKS, 1, STATE_LANES)
    ldt = jnp.broadcast_to(log_dt.astype(F32)[:, None], (N_GROUPS, N_STATE))
    bt = lambda x: (x.astype(F32).reshape(N_BLOCKS, GROUPS_PER_BLOCK, N_STATE, GROUP)
                    .transpose(0, 3, 1, 2).reshape(N_BLOCKS, GROUP, STATE_LANES))
    ct = lambda x: (x.astype(F32).reshape(N_BLOCKS, GROUPS_PER_BLOCK, GROUP, N_STATE)
                    .transpose(0, 2, 1, 3).reshape(N_BLOCKS, GROUP, STATE_LANES))
    return (row(lam_re), row(lam_im), row(ldt), bt(b_re), bt(b_im), ct(c_re), ct(c_im),
            d_skip.astype(F32).reshape(N_BLOCKS, 1, LANES))


def _s5(proj3, h0_re, h0_im, s5p, buf0, buf1, conv_w, t_len, m, n_seq):
    kern = functools.partial(_s5_kernel, t_len=t_len, m=m, n_seq=n_seq)
    blk3 = lambda b: (b, 0, 0)
    row_spec = pl.BlockSpec((1, 1, STATE_LANES), blk3)
    mat_spec = pl.BlockSpec((1, GROUP, STATE_LANES), blk3)
    state_spec = pl.BlockSpec((1, n_seq, STATE_LANES), blk3)
    end_mat = pltpu.VMEM((t_len * LANES, STATE_LANES), BF16)
    chunk_state = pltpu.VMEM((m, STATE_LANES), F32)
    nbw = WIDTH // LANES
    seg_spec = lambda seg: pl.BlockSpec((t_len, m, LANES), lambda b: (0, 0, seg * nbw + b))
    return pl.pallas_call(
        kern,
        grid=(N_BLOCKS,),
        in_specs=[seg_spec(0),
                  state_spec, state_spec,
                  row_spec, row_spec, row_spec,
                  mat_spec, mat_spec, mat_spec, mat_spec,
                  pl.BlockSpec((1, 1, LANES), blk3),
                  seg_spec(2), seg_spec(3), seg_spec(4), seg_spec(5),
                  pl.BlockSpec((n_seq, LANES), lambda b: (0, b)),
                  pl.BlockSpec((n_seq, LANES), lambda b: (0, b)),
                  pl.BlockSpec((3, LANES), lambda b: (0, b))],
        out_specs=[seg_spec(0), state_spec, state_spec, seg_spec(0),
                   pl.BlockSpec((2, m, LANES), lambda b: (0, 0, b))],
        out_shape=[jax.ShapeDtypeStruct((t_len, m, WIDTH), BF16),
                   jax.ShapeDtypeStruct((N_BLOCKS, n_seq, STATE_LANES), F32),
                   jax.ShapeDtypeStruct((N_BLOCKS, n_seq, STATE_LANES), F32),
                   jax.ShapeDtypeStruct((t_len, m, WIDTH), BF16),
                   jax.ShapeDtypeStruct((2, m, WIDTH), F32)],
        scratch_shapes=[pltpu.VMEM((m, t_len * LANES), BF16),
                        end_mat, end_mat, end_mat, end_mat,
                        pltpu.VMEM((t_len * LANES, 2 * LANES), BF16),
                        chunk_state, chunk_state, chunk_state, chunk_state],
        compiler_params=_cparams(1),
        name="s5_scan",
    )(proj3, h0_re, h0_im, *s5p, proj3, proj3, proj3, proj3, buf0, buf1, conv_w)


def _glu_kernel(ya_ref, sza_ref, w_ref, b_ref, o_ref):
    ya = ya_ref[...]
    glu = jnp.dot(ya, w_ref[...], preferred_element_type=F32) + b_ref[...]
    o_ref[...] = (ya.astype(F32) * _sigmoid(glu) * sza_ref[...].astype(F32)).astype(BF16)


def _glu_cast_kernel(ya_ref, sza_ref, w_ref, b_ref, o_ref, wbf_ref, *, tn):
    w = w_ref[...].astype(BF16)
    wbf_ref[...] = w
    glu = jnp.dot(ya_ref[...], w, preferred_element_type=F32) + b_ref[...]
    ya = ya_ref[:, pl.ds(pl.multiple_of(pl.program_id(0) * tn, tn), tn)].astype(F32)
    o_ref[...] = (ya * _sigmoid(glu) * sza_ref[...].astype(F32)).astype(BF16)


def _glu(ya, proj, w_glu, b_glu, tm=1024, tn=512):
    rows = ya.shape[0]
    out_shape = jax.ShapeDtypeStruct((rows, WIDTH), BF16)
    if w_glu.dtype == BF16:
        return pl.pallas_call(
            _glu_kernel,
            grid=(rows // tm,),
            in_specs=[pl.BlockSpec((tm, WIDTH), lambda i: (i, 0)),
                      pl.BlockSpec((tm, WIDTH), lambda i: (i, 1)),
                      pl.BlockSpec((WIDTH, WIDTH), lambda i: (0, 0), pipeline_mode=pl.Buffered(1)),
                      pl.BlockSpec((1, WIDTH), lambda i: (0, 0))],
            out_specs=pl.BlockSpec((tm, WIDTH), lambda i: (i, 0)),
            out_shape=out_shape,
            compiler_params=_cparams(1),
            name="glu_gate",
        )(ya, proj, w_glu, b_glu.reshape(1, WIDTH))
    nb = WIDTH // tn
    w_spec = pl.BlockSpec((WIDTH, tn), lambda j: (0, j))
    return pl.pallas_call(
        functools.partial(_glu_cast_kernel, tn=tn),
        grid=(nb,),
        in_specs=[pl.BlockSpec((rows, WIDTH), lambda j: (0, 0)),
                  pl.BlockSpec((rows, tn), lambda j: (0, nb + j)),
                  w_spec,
                  pl.BlockSpec((1, tn), lambda j: (0, j))],
        out_specs=[pl.BlockSpec((rows, tn), lambda j: (0, j)), w_spec],
        out_shape=[out_shape, jax.ShapeDtypeStruct((WIDTH, WIDTH), BF16)],
        compiler_params=_cparams(1),
        name="glu_gate_cast",
    )(ya, proj, w_glu, b_glu.reshape(1, WIDTH))


PERM_ROWS = 512


def _to_natural(pm, t_len):
    rows, n = pm.shape
    c = rows // t_len
    cg = PERM_ROWS // t_len
    r_nat = lax.broadcasted_iota(jnp.int32, (PERM_ROWS, PERM_ROWS), 0)
    r_in = lax.broadcasted_iota(jnp.int32, (PERM_ROWS, PERM_ROWS), 1)
    src = (r_nat & (t_len - 1)) * cg + lax.shift_right_logical(r_nat, t_len.bit_length() - 1)
    perm = jnp.where(r_in == src, 1.0, 0.0).astype(BF16)
    pm3 = pm.reshape(t_len, c, n)
    out = []
    for g in range(c // cg):
        grp = pm3[:, g * cg:(g + 1) * cg, :].reshape(PERM_ROWS, n).astype(BF16)
        out.append(jnp.dot(perm, grp, preferred_element_type=F32).astype(BF16))
    return out[0] if len(out) == 1 else jnp.concatenate(out, axis=0)


def _gated_merge(a_ref, b_ref, wa, wb, sga_ref, sgb_ref, o_ref, natural):
    t_len, c, _ = a_ref.shape
    rows = t_len * c
    pa = jnp.dot(a_ref[...].reshape(rows, WIDTH), wa, preferred_element_type=F32)
    pb = jnp.dot(b_ref[...].reshape(rows, WIDTH), wb, preferred_element_type=F32)
    tn = pa.shape[1]
    pm = sga_ref[...].reshape(rows, tn) * pa + sgb_ref[...].reshape(rows, tn) * pb
    o_ref[...] = _to_natural(pm, t_len) if natural else pm.astype(BF16)


def _merge_kernel(a_ref, b_ref, wa_ref, wb_ref, sga_ref, sgb_ref, o_ref, *, natural):
    _gated_merge(a_ref, b_ref, wa_ref[...], wb_ref[...], sga_ref, sgb_ref, o_ref, natural)


def _merge_cast_kernel(a_ref, b_ref, wa_ref, wb_ref, sga_ref, sgb_ref, o_ref, wabf_ref, wbbf_ref, *, natural):
    wa = wa_ref[...].astype(BF16)
    wb = wb_ref[...].astype(BF16)
    wabf_ref[...] = wa
    wbbf_ref[...] = wb
    _gated_merge(a_ref, b_ref, wa, wb, sga_ref, sgb_ref, o_ref, natural)


def _merge(a_in3, b_in3, proj3, w_pa, w_pb, c, natural):
    t_len, m, _ = a_in3.shape
    cast = w_pa.dtype != BF16
    tn = 512 if cast else 1024
    nb = WIDTH // tn
    assert not cast or c == m
    assert not natural or (PERM_ROWS % t_len == 0 and c % (PERM_ROWS // t_len) == 0)
    w_spec = pl.BlockSpec((WIDTH, tn), lambda i, j: (0, j))
    out_spec = pl.BlockSpec((t_len * c, tn), lambda i, j: (i, j))
    out_shape = jax.ShapeDtypeStruct((t_len * m, D_MODEL), BF16)
    w_shape = jax.ShapeDtypeStruct((WIDTH, D_MODEL), BF16)
    return pl.pallas_call(
        functools.partial(_merge_cast_kernel if cast else _merge_kernel, natural=natural),
        grid=(m // c, D_MODEL // tn),
        in_specs=[pl.BlockSpec((t_len, c, WIDTH), lambda i, j: (0, i, 0)),
                  pl.BlockSpec((t_len, c, WIDTH), lambda i, j: (0, i, 0)),
                  w_spec, w_spec,
                  pl.BlockSpec((t_len, c, tn), lambda i, j: (0, i, 6 * nb + j)),
                  pl.BlockSpec((t_len, c, tn), lambda i, j: (0, i, 8 * nb + j))],
        out_specs=[out_spec, w_spec, w_spec] if cast else out_spec,
        out_shape=[out_shape, w_shape, w_shape] if cast else out_shape,
        compiler_params=_cparams(2),
        name="merge_cast" if cast else "merge",
    )(a_in3, b_in3, w_pa, w_pb, proj3, proj3)


def _outproj_kernel(m_ref, w_ref, x_ref, gate_ref, fg_ref, o_ref, *rest, tn, tiles_per_seq):
    j = pl.program_id(1)
    w = w_ref[...]
    if len(rest) == 2:
        wbf_ref, hres = rest
        w = w.astype(BF16)
        wbf_ref[...] = w
    else:
        (hres,) = rest
    part = jnp.dot(m_ref[...], w, preferred_element_type=F32)
    col = pl.ds(pl.multiple_of(j * tn, tn), tn)
    if tiles_per_seq:
        gate = gate_ref[pl.ds(pl.program_id(0) // tiles_per_seq, 1), :]
        hres[:, col] = x_ref[...] + gate * part
    else:
        n_rows = gate_ref.shape[0]
        for r in range(0, hres.shape[0], n_rows):
            hres[r:r + n_rows, col] = x_ref[r:r + n_rows, :] + gate_ref[...] * part[r:r + n_rows, :]

    @pl.when(j == pl.num_programs(1) - 1)
    def _():
        for r in range(0, hres.shape[0], LANES):
            h = hres[r:r + LANES, :]
            o_ref[r:r + LANES, :] = h * lax.rsqrt(jnp.mean(h * h, axis=-1, keepdims=True) + EPS) * fg_ref[...]


def _outproj(merged, w_o, x2d, gate, final_g, tm, tiles_per_seq):
    rows = x2d.shape[0]
    cast = w_o.dtype != BF16
    tn = 512 if cast else 1024
    assert tiles_per_seq or tm % gate.shape[0] == 0
    assert not cast or tm == rows
    w_spec = pl.BlockSpec((D_MODEL, tn), lambda i, j: (0, j))
    out_spec = pl.BlockSpec((tm, D_MODEL), lambda i, j: (i, 0))
    out_shape = jax.ShapeDtypeStruct((rows, D_MODEL), F32)
    return pl.pallas_call(
        functools.partial(_outproj_kernel, tn=tn, tiles_per_seq=tiles_per_seq),
        grid=(rows // tm, D_MODEL // tn),
        in_specs=[pl.BlockSpec((tm, D_MODEL), lambda i, j: (i, 0)),
                  w_spec,
                  pl.BlockSpec((tm, tn), lambda i, j: (i, j)),
                  pl.BlockSpec((gate.shape[0], tn), lambda i, j: (0, j)),
                  pl.BlockSpec((1, D_MODEL), lambda i, j: (0, 0))],
        out_specs=[out_spec, w_spec] if cast else out_spec,
        out_shape=[out_shape, jax.ShapeDtypeStruct((D_MODEL, D_MODEL), BF16)] if cast else out_shape,
        scratch_shapes=[pltpu.VMEM((tm, D_MODEL), F32)],
        compiler_params=_cparams(2, V7X_VMEM_BYTES - 4 * MIB),
        name="outproj_cast" if cast else "outproj",
    )(merged, w_o, x2d, gate, final_g.reshape(1, D_MODEL))


def _state_to_blocks(state):
    n = state.shape[0]
    re = state[..., 0].reshape(n, N_BLOCKS, STATE_LANES).transpose(1, 0, 2)
    im = state[..., 1].reshape(n, N_BLOCKS, STATE_LANES).transpose(1, 0, 2)
    return re, im


def _blocks_to_state(re, im):
    n = re.shape[1]
    re = re.transpose(1, 0, 2).reshape(n, N_GROUPS, N_STATE)
    im = im.transpose(1, 0, 2).reshape(n, N_GROUPS, N_STATE)
    return jnp.stack([re, im], axis=-1)


def _group(x2d, t_len, n_seq, ct, tm, mod, ssm0, conv0, p):
    m = x2d.shape[0] // t_len
    cps = m // n_seq
    natural = cps > 1
    tiles_per_seq = cps // ct if natural else 0
    shift, scale, gate = mod
    xn = _norm_mod(x2d, t_len, p["norm_g"], scale, shift, ct, tiles_per_seq)
    xn = xn.reshape(t_len * m, D_MODEL)
    if "w_in_bf" in p:
        proj = _inproj(xn, p["w_in_bf"], tm)
    else:
        proj, p["w_in_bf"] = _inproj_cast(xn, p["w_in"])
    proj3 = proj.reshape(t_len, m, IN_COLS)
    h0_re, h0_im = _state_to_blocks(ssm0)
    ya3, hr, hi, b_in3, v_last = _s5(proj3, h0_re, h0_im, p["s5"], conv0[:, 0], conv0[:, 1], p["conv_w"],
                                     t_len, m, n_seq)
    ya = ya3.reshape(t_len * m, WIDTH)
    if "w_glu_bf" in p:
        a_in = _glu(ya, proj, p["w_glu_bf"], p["b_glu"])
    else:
        a_in, p["w_glu_bf"] = _glu(ya, proj, p["w_glu"], p["b_glu"])
    a_in3 = a_in.reshape(t_len, m, WIDTH)
    if "w_pa_bf" in p:
        merged = _merge(a_in3, b_in3, proj3, p["w_pa_bf"], p["w_pb_bf"], tm // t_len, natural)
    else:
        merged, p["w_pa_bf"], p["w_pb_bf"] = _merge(a_in3, b_in3, proj3, p["w_pa"], p["w_pb"], m, natural)
    if "w_o_bf" in p:
        out = _outproj(merged, p["w_o_bf"], x2d, gate, p["final_g"], ct * t_len, tiles_per_seq)
    else:
        out, p["w_o_bf"] = _outproj(merged, p["w_o"], x2d, gate, p["final_g"], ct * t_len, tiles_per_seq)
    conv_new = v_last[:, cps - 1::cps, :].transpose(1, 0, 2)
    return out, _blocks_to_state(hr, hi), conv_new


def kernel(x_prompt, x_sample, state_ssm, state_conv, c_prompt, c_sample, norm_g, w_ada, b_ada, w_in, lam_re, lam_im, log_dt, b_re, b_im, c_re, c_im, d_skip, w_glu, b_glu, w_pa, conv_w, w_pb, w_o, final_g):
    depth = norm_g.shape[0]
    assert depth == 1
    n_p, seq, _ = x_prompt.shape
    n_s, dec, _ = x_sample.shape
    t_p = 16
    assert seq % t_p == 0

    l = 0
    p = dict(norm_g=norm_g[l], w_in=w_in[l], conv_w=conv_w[l], w_glu=w_glu[l], b_glu=b_glu[l],
             w_pa=w_pa[l], w_pb=w_pb[l], w_o=w_o[l], final_g=final_g,
             s5=_s5_params(lam_re[l], lam_im[l], log_dt[l], b_re[l], b_im[l], c_re[l], c_im[l], d_skip[l]))

    n_c = n_p + n_s
    pad = (-n_c) % 16
    c_all = jnp.concatenate([c_prompt, c_sample, jnp.zeros((pad, D_MODEL), F32)], axis=0)
    mod = _ada(c_all, w_ada[l], b_ada[l])
    mod_p = tuple(mod[:n_p, k * D_MODEL:(k + 1) * D_MODEL] for k in range(3))
    mod_s = tuple(mod[n_p:n_c, k * D_MODEL:(k + 1) * D_MODEL] for k in range(3))

    x_s = x_sample.transpose(1, 0, 2).reshape(dec * n_s, D_MODEL)
    y_s, ssm_s, conv_s = _group(x_s, dec, n_s, n_s, dec * n_s, mod_s, state_ssm[l], state_conv[l], p)
    y_sample = y_s.reshape(dec, n_s, D_MODEL).transpose(1, 0, 2)

    ssm_p0 = jnp.zeros((n_p, N_GROUPS, N_STATE, 2), F32)
    conv_p0 = jnp.zeros((n_p, 2, WIDTH), F32)
    y_p, ssm_p, conv_p = _group(x_prompt.reshape(n_p * seq, D_MODEL), t_p, n_p, 32, 1024,
                                mod_p, ssm_p0, conv_p0, p)
    y_prompt = y_p.reshape(n_p, seq, D_MODEL)

    return (y_prompt, y_sample, ssm_p[None], conv_p[None], ssm_s[None], conv_s[None])
```

```python
import functools

import jax
import jax.numpy as jnp
from jax import lax
from jax.experimental import pallas as pl
from jax.experimental.pallas import tpu as pltpu

F32 = jnp.float32
BF16 = jnp.bfloat16

D_MODEL = 4096
WIDTH = 2048
N_GROUPS = 128
GROUP = 16
N_STATE = 64
GROUPS_PER_BLOCK = 8
N_BLOCKS = N_GROUPS // GROUPS_PER_BLOCK
STATE_LANES = GROUPS_PER_BLOCK * N_STATE
LANES = 128
IN_COLS = 10 * WIDTH
EPS = 1e-6
MIB = 1024 * 1024
V7X_VMEM_BYTES = 64 * MIB
VMEM_LIMIT = V7X_VMEM_BYTES - 8 * MIB
NT_DIMS = (((1,), (1,)), ((), ()))


def _cparams(n_axes, vmem_limit=VMEM_LIMIT):
    return pltpu.CompilerParams(dimension_semantics=("arbitrary",) * n_axes,
                                vmem_limit_bytes=vmem_limit)


def _sigmoid(x):
    return jax.nn.sigmoid(x)


def _ada_kernel(c_ref, w_ref, b_ref, o_ref):
    c = c_ref[...]
    a = (c * _sigmoid(c)).astype(BF16)
    o_ref[...] = jnp.dot(a, w_ref[...].astype(BF16), preferred_element_type=F32) + b_ref[...]


def _ada(c_all, w_ada, b_ada, tn=512):
    rows = c_all.shape[0]
    n_out = w_ada.shape[1]
    return pl.pallas_call(
        _ada_kernel,
        grid=(n_out // tn,),
        in_specs=[pl.BlockSpec((rows, D_MODEL), lambda j: (0, 0)),
                  pl.BlockSpec((D_MODEL, tn), lambda j: (0, j)),
                  pl.BlockSpec((1, tn), lambda j: (0, j))],
        out_specs=pl.BlockSpec((rows, tn), lambda j: (0, j)),
        out_shape=jax.ShapeDtypeStruct((rows, n_out), F32),
        compiler_params=_cparams(1),
        name="ada_mod",
    )(c_all, w_ada, b_ada.reshape(1, n_out))


def _norm_kernel(x_ref, g_ref, sc_ref, sh_ref, o_ref, *, t_len, ct, tiles_per_seq):
    def norm_mod(x, sc, sh):
        y = x * lax.rsqrt(jnp.mean(x * x, axis=-1, keepdims=True) + EPS) * g_ref[...]
        return (y * (1.0 + sc) + sh).astype(BF16)

    if not tiles_per_seq:
        for s in range(t_len):
            o_ref[s] = norm_mod(x_ref[s * ct:(s + 1) * ct, :], sc_ref[...], sh_ref[...])
        return

    n = pl.program_id(0) // tiles_per_seq
    y = norm_mod(x_ref[...], sc_ref[pl.ds(n, 1), :], sh_ref[pl.ds(n, 1), :])
    r = t_len * ct
    r_out = lax.broadcasted_iota(jnp.int32, (r, r), 0)
    r_in = lax.broadcasted_iota(jnp.int32, (r, r), 1)
    src_row = (r_out & (ct - 1)) * t_len + lax.shift_right_logical(r_out, ct.bit_length() - 1)
    perm = jnp.where(r_in == src_row, 1.0, 0.0).astype(BF16)
    yp = jnp.dot(perm, y, preferred_element_type=F32)
    o_ref[...] = yp.reshape(t_len, ct, D_MODEL).astype(BF16)


def _norm_mod(x2d, t_len, norm_g, scale, shift, ct, tiles_per_seq):
    assert ct & (ct - 1) == 0
    r = t_len * ct
    m = x2d.shape[0] // t_len
    assert tiles_per_seq or ct == m == scale.shape[0]
    mod_spec = pl.BlockSpec((scale.shape[0], D_MODEL), lambda i: (0, 0))
    return pl.pallas_call(
        functools.partial(_norm_kernel, t_len=t_len, ct=ct, tiles_per_seq=tiles_per_seq),
        grid=(m // ct,),
        in_specs=[pl.BlockSpec((r, D_MODEL), lambda i: (i, 0)),
                  pl.BlockSpec((1, D_MODEL), lambda i: (0, 0)),
                  mod_spec, mod_spec],
        out_specs=pl.BlockSpec((t_len, ct, D_MODEL), lambda i: (0, i, 0)),
        out_shape=jax.ShapeDtypeStruct((t_len, m, D_MODEL), BF16),
        compiler_params=_cparams(1),
        name="norm_mod",
    )(x2d, norm_g.reshape(1, D_MODEL), scale, shift)


def _activate(acc, seg):
    is_silu = jnp.logical_or(seg == 1, seg == 5)
    is_sig = seg >= 6
    sig = _sigmoid(acc)
    return jnp.where(is_sig, sig, jnp.where(is_silu, acc * sig, acc)).astype(BF16)


def _inproj_kernel(x_ref, w_ref, o_ref, *, tn):
    acc = jnp.dot(x_ref[...], w_ref[...], preferred_element_type=F32)
    o_ref[...] = _activate(acc, pl.program_id(1) // (WIDTH // tn))


def _inproj_cast_kernel(x_ref, w_ref, o_ref, wbf_ref, *, tn):
    w = w_ref[...].astype(BF16)
    wbf_ref[...] = w
    acc = jnp.dot(x_ref[...], w, preferred_element_type=F32)
    o_ref[...] = _activate(acc, pl.program_id(0) // (WIDTH // tn))


def _inproj_cast(xn, w_in, tn=1024):
    rows = xn.shape[0]
    return pl.pallas_call(
        functools.partial(_inproj_cast_kernel, tn=tn),
        grid=(IN_COLS // tn,),
        in_specs=[pl.BlockSpec((rows, D_MODEL), lambda j: (0, 0)),
                  pl.BlockSpec((D_MODEL, tn), lambda j: (0, j))],
        out_specs=[pl.BlockSpec((rows, tn), lambda j: (0, j)),
                   pl.BlockSpec((D_MODEL, tn), lambda j: (0, j))],
        out_shape=[jax.ShapeDtypeStruct((rows, IN_COLS), BF16),
                   jax.ShapeDtypeStruct((D_MODEL, IN_COLS), BF16)],
        compiler_params=_cparams(1, V7X_VMEM_BYTES - 4 * MIB),
        name="inproj_cast",
    )(xn, w_in)


def _inproj(xn, w_in_bf, tm, tn=1024):
    rows = xn.shape[0]
    return pl.pallas_call(
        functools.partial(_inproj_kernel, tn=tn),
        grid=(rows // tm, IN_COLS // tn),
        in_specs=[pl.BlockSpec((tm, D_MODEL), lambda i, j: (i, 0)),
                  pl.BlockSpec((D_MODEL, tn), lambda i, j: (0, j))],
        out_specs=pl.BlockSpec((tm, tn), lambda i, j: (i, j)),
        out_shape=jax.ShapeDtypeStruct((rows, IN_COLS), BF16),
        compiler_params=_cparams(2),
        name="inproj",
    )(xn, w_in_bf)


def _cmul(ar, ai, br, bi):
    return ar * br - ai * bi, ar * bi + ai * br


def _short_conv(hb_ref, cb_ref, bb_ref, szb_ref, buf0_ref, buf1_ref, w_ref, bin_ref, v_ref, *, t_len, m, n_seq):
    cps = m // n_seq

    def prev_chunk(v, first_rows):
        if cps == 1:
            return first_rows
        out = pltpu.roll(v, 1, 0)
        rid = lax.broadcasted_iota(jnp.int32, v.shape, 0)
        for n in range(n_seq):
            out = jnp.where(rid == n * cps, first_rows[n:n + 1, :], out)
        return out

    def v_of(s):
        return hb_ref[s].astype(F32) * cb_ref[s].astype(F32)

    vm2 = prev_chunk(v_of(t_len - 2), buf0_ref[...])
    vm1 = prev_chunk(v_of(t_len - 1), buf1_ref[...])
    for s in range(t_len):
        v0 = v_of(s)
        conv = vm2 * w_ref[0:1, :] + vm1 * w_ref[1:2, :] + v0 * w_ref[2:3, :]
        bin_ref[s] = (bb_ref[s].astype(F32) * conv * szb_ref[s].astype(F32)).astype(BF16)
        if s >= t_len - 2:
            v_ref[s - (t_len - 2)] = v0
        vm2, vm1 = vm1, v0


def _s5_kernel(u_ref, h0r_ref, h0i_ref, lr_ref, li_ref, ldt_ref, btr_ref, bti_ref, ctr_ref, cti_ref, d_ref,
               hb_ref, cb_ref, bb_ref, szb_ref, buf0_ref, buf1_ref, cw_ref,
               y_ref, hr_out, hi_out, bin_ref, v_ref,
               ucat, bend_r, bend_i, cend_r, cend_i, kpair, hl_r, hl_i, hp_r, hp_i, *, t_len, m, n_seq):
    blk = pl.program_id(0)
    _short_conv(hb_ref, cb_ref, bb_ref, szb_ref, buf0_ref, buf1_ref, cw_ref, bin_ref, v_ref,
                t_len=t_len, m=m, n_seq=n_seq)

    @pl.when(blk == 0)
    def _():
        for ref in (bend_r, bend_i, cend_r, cend_i, kpair):
            ref[...] = jnp.zeros(ref.shape, ref.dtype)

    lam_r = lr_ref[0]
    lam_i = li_ref[0]
    dt = jnp.exp(ldt_ref[0])
    mag = jnp.exp(lam_r * dt)
    lb_r = mag * jnp.cos(lam_i * dt)
    lb_i = mag * jnp.sin(lam_i * dt)
    den = lam_r * lam_r + lam_i * lam_i
    nr = lb_r - 1.0
    co_r = (nr * lam_r + lb_i * lam_i) / den
    co_i = (lb_i * lam_r - nr * lam_i) / den
    bb_r, bb_i = _cmul(co_r, co_i, btr_ref[0], bti_ref[0])
    c_r = ctr_ref[0]
    c_i = cti_ref[0]
    pw = [(jnp.ones_like(lb_r), jnp.zeros_like(lb_r))]
    for _ in range(t_len):
        pw.append(_cmul(pw[-1][0], pw[-1][1], lb_r, lb_i))
    ar, ai = pw[t_len]

    lane = lax.broadcasted_iota(jnp.int32, (GROUP, LANES), 1)
    lo = lane < N_STATE

    def pair_tile(x, q):
        slab = x[:, q * LANES:(q + 1) * LANES]
        return jnp.concatenate([jnp.where(lo, slab, 0.0), jnp.where(lo, 0.0, slab)], axis=0).astype(BF16)

    grp_row = lax.broadcasted_iota(jnp.int32, (LANES, STATE_LANES), 0) // GROUP
    grp_lane = lax.broadcasted_iota(jnp.int32, (LANES, STATE_LANES), 1) // N_STATE
    same_group = grp_row == grp_lane

    def expand(x):
        return jnp.where(same_group, jnp.concatenate([x] * GROUPS_PER_BLOCK, axis=0), 0.0).astype(BF16)

    bbx_r = expand(bb_r)
    bbx_i = expand(bb_i)
    for s in range(t_len):
        ucat[:, s * LANES:(s + 1) * LANES] = u_ref[s]
        be_r, be_i = _cmul(pw[t_len - 1 - s][0], pw[t_len - 1 - s][1], bb_r, bb_i)
        ce_r, ce_i = _cmul(pw[s + 1][0], pw[s + 1][1], c_r, c_i)
        for q in range(4):
            r0 = s * LANES + q * 2 * GROUP
            rows = slice(r0, r0 + 2 * GROUP)
            cols = slice(q * LANES, (q + 1) * LANES)
            bend_r[rows, cols] = pair_tile(be_r, q)
            bend_i[rows, cols] = pair_tile(be_i, q)
            cend_r[rows, cols] = pair_tile(ce_r, q)
            cend_i[rows, cols] = pair_tile(-ce_i, q)
        cl_r, cl_i = _cmul(pw[s][0], pw[s][1], c_r, c_i)
        k_lag = (lax.dot_general(bbx_r, expand(cl_r), NT_DIMS, preferred_element_type=F32)
                 - lax.dot_general(bbx_i, expand(cl_i), NT_DIMS, preferred_element_type=F32))
        k_lag = k_lag.astype(BF16)
        kpair[(t_len - 1 - s) * LANES:(t_len - s) * LANES, LANES:] = k_lag
        if s <= t_len - 2:
            kpair[(t_len - 2 - s) * LANES:(t_len - 1 - s) * LANES, :LANES] = k_lag

    uc = ucat[...]
    hl_r[...] = jnp.dot(uc, bend_r[...], preferred_element_type=F32)
    hl_i[...] = jnp.dot(uc, bend_i[...], preferred_element_type=F32)

    cps = m // n_seq
    if cps == 1:
        h_r = h0r_ref[0]
        h_i = h0i_ref[0]
        hp_r[...] = h_r
        hp_i[...] = h_i
        hr_out[0] = ar * h_r - ai * h_i + hl_r[...]
        hi_out[0] = ar * h_i + ai * h_r + hl_i[...]
    else:
        def body(c, carry):
            new = []
            for n in range(n_seq):
                h_r, h_i = carry[n]
                row = pl.ds(n * cps + c, 1)
                hp_r[row, :] = h_r
                hp_i[row, :] = h_i
                new.append((ar * h_r - ai * h_i + hl_r[row, :], ar * h_i + ai * h_r + hl_i[row, :]))
            return tuple(new)

        init = tuple((h0r_ref[0, n:n + 1, :], h0i_ref[0, n:n + 1, :]) for n in range(n_seq))
        fin = lax.fori_loop(0, cps, body, init)
        for n in range(n_seq):
            hr_out[0, n:n + 1, :] = fin[n][0]
            hi_out[0, n:n + 1, :] = fin[n][1]

    ycar = (lax.dot_general(hp_r[...].astype(BF16), cend_r[...], NT_DIMS, preferred_element_type=F32)
            + lax.dot_general(hp_i[...].astype(BF16), cend_i[...], NT_DIMS, preferred_element_type=F32))
    dvec = d_ref[0]
    for t in range(0, t_len, 2):
        y2 = jnp.dot(ucat[:, :(t + 2) * LANES], kpair[(t_len - 2 - t) * LANES:, :],
                     preferred_element_type=F32) + ycar[:, t * LANES:(t + 2) * LANES]
        for k in range(2):
            y = y2[:, k * LANES:(k + 1) * LANES] + dvec * u_ref[t + k].astype(F32)
            y_ref[t + k] = jax.nn.gelu(y).astype(BF16)


def _s5_params(lam_re, lam_im, log_dt, b_re, b_im, c_re, c_im, d_skip):
    row = lambda x: x.astype(F32).reshape(N_BLOCKS, 1, STATE_LANES)
    ldt = jnp.broadcast_to(log_dt.astype(F32)[:, None], (N_GROUPS, N_STATE))
    bt = lambda x: (x.astype(F32).reshape(N_BLOCKS, GROUPS_PER_BLOCK, N_STATE, GROUP)
                    .transpose(0, 3, 1, 2).reshape(N_BLOCKS, GROUP, STATE_LANES))
    ct = lambda x: (x.astype(F32).reshape(N_BLOCKS, GROUPS_PER_BLOCK, GROUP, N_STATE)
                    .transpose(0, 2, 1, 3).reshape(N_BLOCKS, GROUP, STATE_LANES))
    return (row(lam_re), row(lam_im), row(ldt), bt(b_re), bt(b_im), ct(c_re), ct(c_im),
            d_skip.astype(F32).reshape(N_BLOCKS, 1, LANES))


def _s5(proj3, h0_re, h0_im, s5p, buf0, buf1, conv_w, t_len, m, n_seq):
    kern = functools.partial(_s5_kernel, t_len=t_len, m=m, n_seq=n_seq)
    blk3 = lambda b: (b, 0, 0)
    row_spec = pl.BlockSpec((1, 1, STATE_LANES), blk3)
    mat_spec = pl.BlockSpec((1, GROUP, STATE_LANES), blk3)
    state_spec = pl.BlockSpec((1, n_seq, STATE_LANES), blk3)
    end_mat = pltpu.VMEM((t_len * LANES, STATE_LANES), BF16)
    chunk_state = pltpu.VMEM((m, STATE_LANES), F32)
    nbw = WIDTH // LANES
    seg_spec = lambda seg: pl.BlockSpec((t_len, m, LANES), lambda b: (0, 0, seg * nbw + b))
    return pl.pallas_call(
        kern,
        grid=(N_BLOCKS,),
        in_specs=[seg_spec(0),
                  state_spec, state_spec,
                  row_spec, row_spec, row_spec,
                  mat_spec, mat_spec, mat_spec, mat_spec,
                  pl.BlockSpec((1, 1, LANES), blk3),
                  seg_spec(2), seg_spec(3), seg_spec(4), seg_spec(5),
                  pl.BlockSpec((n_seq, LANES), lambda b: (0, b)),
                  pl.BlockSpec((n_seq, LANES), lambda b: (0, b)),
                  pl.BlockSpec((3, LANES), lambda b: (0, b))],
        out_specs=[seg_spec(0), state_spec, state_spec, seg_spec(0),
                   pl.BlockSpec((2, m, LANES), lambda b: (0, 0, b))],
        out_shape=[jax.ShapeDtypeStruct((t_len, m, WIDTH), BF16),
                   jax.ShapeDtypeStruct((N_BLOCKS, n_seq, STATE_LANES), F32),
                   jax.ShapeDtypeStruct((N_BLOCKS, n_seq, STATE_LANES), F32),
                   jax.ShapeDtypeStruct((t_len, m, WIDTH), BF16),
                   jax.ShapeDtypeStruct((2, m, WIDTH), F32)],
        scratch_shapes=[pltpu.VMEM((m, t_len * LANES), BF16),
                        end_mat, end_mat, end_mat, end_mat,
                        pltpu.VMEM((t_len * LANES, 2 * LANES), BF16),
                        chunk_state, chunk_state, chunk_state, chunk_state],
        compiler_params=_cparams(1),
        name="s5_scan",
    )(proj3, h0_re, h0_im, *s5p, proj3, proj3, proj3, proj3, buf0, buf1, conv_w)


def _glu_kernel(ya_ref, sza_ref, w_ref, b_ref, o_ref):
    ya = ya_ref[...]
    glu = jnp.dot(ya, w_ref[...], preferred_element_type=F32) + b_ref[...]
    o_ref[...] = (ya.astype(F32) * _sigmoid(glu) * sza_ref[...].astype(F32)).astype(BF16)


def _glu_cast_kernel(ya_ref, sza_ref, w_ref, b_ref, o_ref, wbf_ref, *, tn):
    w = w_ref[...].astype(BF16)
    wbf_ref[...] = w
    glu = jnp.dot(ya_ref[...], w, preferred_element_type=F32) + b_ref[...]
    ya = ya_ref[:, pl.ds(pl.multiple_of(pl.program_id(0) * tn, tn), tn)].astype(F32)
    o_ref[...] = (ya * _sigmoid(glu) * sza_ref[...].astype(F32)).astype(BF16)


def _glu(ya, proj, w_glu, b_glu, tm=1024, tn=512):
    rows = ya.shape[0]
    out_shape = jax.ShapeDtypeStruct((rows, WIDTH), BF16)
    if w_glu.dtype == BF16:
        return pl.pallas_call(
            _glu_kernel,
            grid=(rows // tm,),
            in_specs=[pl.BlockSpec((tm, WIDTH), lambda i: (i, 0)),
                      pl.BlockSpec((tm, WIDTH), lambda i: (i, 1)),
                      pl.BlockSpec((WIDTH, WIDTH), lambda i: (0, 0), pipeline_mode=pl.Buffered(1)),
                      pl.BlockSpec((1, WIDTH), lambda i: (0, 0))],
            out_specs=pl.BlockSpec((tm, WIDTH), lambda i: (i, 0)),
            out_shape=out_shape,
            compiler_params=_cparams(1),
            name="glu_gate",
        )(ya, proj, w_glu, b_glu.reshape(1, WIDTH))
    nb = WIDTH // tn
    w_spec = pl.BlockSpec((WIDTH, tn), lambda j: (0, j))
    return pl.pallas_call(
        functools.partial(_glu_cast_kernel, tn=tn),
        grid=(nb,),
        in_specs=[pl.BlockSpec((rows, WIDTH), lambda j: (0, 0)),
                  pl.BlockSpec((rows, tn), lambda j: (0, nb + j)),
                  w_spec,
                  pl.BlockSpec((1, tn), lambda j: (0, j))],
        out_specs=[pl.BlockSpec((rows, tn), lambda j: (0, j)), w_spec],
        out_shape=[out_shape, jax.ShapeDtypeStruct((WIDTH, WIDTH), BF16)],
        compiler_params=_cparams(1),
        name="glu_gate_cast",
    )(ya, proj, w_glu, b_glu.reshape(1, WIDTH))


PERM_ROWS = 512


def _to_natural(pm, t_len):
    rows, n = pm.shape
    c = rows // t_len
    cg = PERM_ROWS // t_len
    r_nat = lax.broadcasted_iota(jnp.int32, (PERM_ROWS, PERM_ROWS), 0)
    r_in = lax.broadcasted_iota(jnp.int32, (PERM_ROWS, PERM_ROWS), 1)
    src = (r_nat & (t_len - 1)) * cg + lax.shift_right_logical(r_nat, t_len.bit_length() - 1)
    perm = jnp.where(r_in == src, 1.0, 0.0).astype(BF16)
    pm3 = pm.reshape(t_len, c, n)
    out = []
    for g in range(c // cg):
        grp = pm3[:, g * cg:(g + 1) * cg, :].reshape(PERM_ROWS, n).astype(BF16)
        out.append(jnp.dot(perm, grp, preferred_element_type=F32).astype(BF16))
    return out[0] if len(out) == 1 else jnp.concatenate(out, axis=0)


def _gated_merge(a_ref, b_ref, wa, wb, sga_ref, sgb_ref, o_ref, natural):
    t_len, c, _ = a_ref.shape
    rows = t_len * c
    pa = jnp.dot(a_ref[...].reshape(rows, WIDTH), wa, preferred_element_type=F32)
    pb = jnp.dot(b_ref[...].reshape(rows, WIDTH), wb, preferred_element_type=F32)
    tn = pa.shape[1]
    pm = sga_ref[...].reshape(rows, tn) * pa + sgb_ref[...].reshape(rows, tn) * pb
    o_ref[...] = _to_natural(pm, t_len) if natural else pm.astype(BF16)


def _merge_kernel(a_ref, b_ref, wa_ref, wb_ref, sga_ref, sgb_ref, o_ref, *, natural):
    _gated_merge(a_ref, b_ref, wa_ref[...], wb_ref[...], sga_ref, sgb_ref, o_ref, natural)


def _merge_cast_kernel(a_ref, b_ref, wa_ref, wb_ref, sga_ref, sgb_ref, o_ref, wabf_ref, wbbf_ref, *, natural):
    wa = wa_ref[...].astype(BF16)
    wb = wb_ref[...].astype(BF16)
    wabf_ref[...] = wa
    wbbf_ref[...] = wb
    _gated_merge(a_ref, b_ref, wa, wb, sga_ref, sgb_ref, o_ref, natural)


def _merge(a_in3, b_in3, proj3, w_pa, w_pb, c, natural):
    t_len, m, _ = a_in3.shape
    cast = w_pa.dtype != BF16
    tn = 512 if cast else 1024
    nb = WIDTH // tn
    assert not cast or c == m
    assert not natural or (PERM_ROWS % t_len == 0 and c % (PERM_ROWS // t_len) == 0)
    w_spec = pl.BlockSpec((WIDTH, tn), lambda i, j: (0, j))
    out_spec = pl.BlockSpec((t_len * c, tn), lambda i, j: (i, j))
    out_shape = jax.ShapeDtypeStruct((t_len * m, D_MODEL), BF16)
    w_shape = jax.ShapeDtypeStruct((WIDTH, D_MODEL), BF16)
    return pl.pallas_call(
        functools.partial(_merge_cast_kernel if cast else _merge_kernel, natural=natural),
        grid=(m // c, D_MODEL // tn),
        in_specs=[pl.BlockSpec((t_len, c, WIDTH), lambda i, j: (0, i, 0)),
                  pl.BlockSpec((t_len, c, WIDTH), lambda i, j: (0, i, 0)),
                  w_spec, w_spec,
                  pl.BlockSpec((t_len, c, tn), lambda i, j: (0, i, 6 * nb + j)),
                  pl.BlockSpec((t_len, c, tn), lambda i, j: (0, i, 8 * nb + j))],
        out_specs=[out_spec, w_spec, w_spec] if cast else out_spec,
        out_shape=[out_shape, w_shape, w_shape] if cast else out_shape,
        compiler_params=_cparams(2),
        name="merge_cast" if cast else "merge",
    )(a_in3, b_in3, w_pa, w_pb, proj3, proj3)


def _outproj_kernel(m_ref, w_ref, x_ref, gate_ref, fg_ref, o_ref, *rest, tn, tiles_per_seq):
    j = pl.program_id(1)
    w = w_ref[...]
    if rest:
        (wbf_ref,) = rest
        w = w.astype(BF16)
        wbf_ref[...] = w
    part = jnp.dot(m_ref[...], w, preferred_element_type=F32)
    col = pl.ds(pl.multiple_of(j * tn, tn), tn)
    tm = o_ref.shape[0]
    if tiles_per_seq:
        gate = gate_ref[pl.ds(pl.program_id(0) // tiles_per_seq, 1), :]
        o_ref[:, col] = x_ref[...] + gate * part
    else:
        n_rows = gate_ref.shape[0]
        for r in range(0, tm, n_rows):
            o_ref[r:r + n_rows, col] = x_ref[r:r + n_rows, :] + gate_ref[...] * part[r:r + n_rows, :]

    @pl.when(j == pl.num_programs(1) - 1)
    def _():
        for r in range(0, tm, LANES):
            h = o_ref[r:r + LANES, :]
            o_ref[r:r + LANES, :] = h * lax.rsqrt(jnp.mean(h * h, axis=-1, keepdims=True) + EPS) * fg_ref[...]


def _outproj(merged, w_o, x2d, gate, final_g, tm, tiles_per_seq):
    rows = x2d.shape[0]
    cast = w_o.dtype != BF16
    tn = 512 if cast else 1024
    assert tiles_per_seq or tm % gate.shape[0] == 0
    assert not cast or tm == rows
    w_spec = pl.BlockSpec((D_MODEL, tn), lambda i, j: (0, j))
    out_spec = pl.BlockSpec((tm, D_MODEL), lambda i, j: (i, 0))
    out_shape = jax.ShapeDtypeStruct((rows, D_MODEL), F32)
    return pl.pallas_call(
        functools.partial(_outproj_kernel, tn=tn, tiles_per_seq=tiles_per_seq),
        grid=(rows // tm, D_MODEL // tn),
        in_specs=[pl.BlockSpec((tm, D_MODEL), lambda i, j: (i, 0)),
                  w_spec,
                  pl.BlockSpec((tm, tn), lambda i, j: (i, j)),
                  pl.BlockSpec((gate.shape[0], tn), lambda i, j: (0, j)),
                  pl.BlockSpec((1, D_MODEL), lambda i, j: (0, 0))],
        out_specs=[out_spec, w_spec] if cast else out_spec,
        out_shape=[out_shape, jax.ShapeDtypeStruct((D_MODEL, D_MODEL), BF16)] if cast else out_shape,
        compiler_params=_cparams(2),
        name="outproj_cast" if cast else "outproj",
    )(merged, w_o, x2d, gate, final_g.reshape(1, D_MODEL))


def _state_to_blocks(state):
    n = state.shape[0]
    re = state[..., 0].reshape(n, N_BLOCKS, STATE_LANES).transpose(1, 0, 2)
    im = state[..., 1].reshape(n, N_BLOCKS, STATE_LANES).transpose(1, 0, 2)
    return re, im


def _blocks_to_state(re, im):
    n = re.shape[1]
    re = re.transpose(1, 0, 2).reshape(n, N_GROUPS, N_STATE)
    im = im.transpose(1, 0, 2).reshape(n, N_GROUPS, N_STATE)
    return jnp.stack([re, im], axis=-1)


def _group(x2d, t_len, n_seq, ct, tm, mod, ssm0, conv0, p):
    m = x2d.shape[0] // t_len
    cps = m // n_seq
    natural = cps > 1
    tiles_per_seq = cps // ct if natural else 0
    shift, scale, gate = mod
    xn = _norm_mod(x2d, t_len, p["norm_g"], scale, shift, ct, tiles_per_seq)
    xn = xn.reshape(t_len * m, D_MODEL)
    if "w_in_bf" in p:
        proj = _inproj(xn, p["w_in_bf"], tm)
    else:
        proj, p["w_in_bf"] = _inproj_cast(xn, p["w_in"])
    proj3 = proj.reshape(t_len, m, IN_COLS)
    h0_re, h0_im = _state_to_blocks(ssm0)
    ya3, hr, hi, b_in3, v_last = _s5(proj3, h0_re, h0_im, p["s5"], conv0[:, 0], conv0[:, 1], p["conv_w"],
                                     t_len, m, n_seq)
    ya = ya3.reshape(t_len * m, WIDTH)
    if "w_glu_bf" in p:
        a_in = _glu(ya, proj, p["w_glu_bf"], p["b_glu"])
    else:
        a_in, p["w_glu_bf"] = _glu(ya, proj, p["w_glu"], p["b_glu"])
    a_in3 = a_in.reshape(t_len, m, WIDTH)
    if "w_pa_bf" in p:
        merged = _merge(a_in3, b_in3, proj3, p["w_pa_bf"], p["w_pb_bf"], tm // t_len, natural)
    else:
        merged, p["w_pa_bf"], p["w_pb_bf"] = _merge(a_in3, b_in3, proj3, p["w_pa"], p["w_pb"], m, natural)
    if "w_o_bf" in p:
        out = _outproj(merged, p["w_o_bf"], x2d, gate, p["final_g"], ct * t_len, tiles_per_seq)
    else:
        out, p["w_o_bf"] = _outproj(merged, p["w_o"], x2d, gate, p["final_g"], ct * t_len, tiles_per_seq)
    conv_new = v_last[:, cps - 1::cps, :].transpose(1, 0, 2)
    return out, _blocks_to_state(hr, hi), conv_new


def kernel(x_prompt, x_sample, state_ssm, state_conv, c_prompt, c_sample, norm_g, w_ada, b_ada, w_in, lam_re, lam_im, log_dt, b_re, b_im, c_re, c_im, d_skip, w_glu, b_glu, w_pa, conv_w, w_pb, w_o, final_g):
    depth = norm_g.shape[0]
    assert depth == 1
    n_p, seq, _ = x_prompt.shape
    n_s, dec, _ = x_sample.shape
    t_p = 16
    assert seq % t_p == 0

    l = 0
    p = dict(norm_g=norm_g[l], w_in=w_in[l], conv_w=conv_w[l], w_glu=w_glu[l], b_glu=b_glu[l],
             w_pa=w_pa[l], w_pb=w_pb[l], w_o=w_o[l], final_g=final_g,
             s5=_s5_params(lam_re[l], lam_im[l], log_dt[l], b_re[l], b_im[l], c_re[l], c_im[l], d_skip[l]))

    n_c = n_p + n_s
    pad = (-n_c) % 16
    c_all = jnp.concatenate([c_prompt, c_sample, jnp.zeros((pad, D_MODEL), F32)], axis=0)
    mod = _ada(c_all, w_ada[l], b_ada[l])
    mod_p = tuple(mod[:n_p, k * D_MODEL:(k + 1) * D_MODEL] for k in range(3))
    mod_s = tuple(mod[n_p:n_c, k * D_MODEL:(k + 1) * D_MODEL] for k in range(3))

    x_s = x_sample.transpose(1, 0, 2).reshape(dec * n_s, D_MODEL)
    y_s, ssm_s, conv_s = _group(x_s, dec, n_s, n_s, dec * n_s, mod_s, state_ssm[l], state_conv[l], p)
    y_sample = y_s.reshape(dec, n_s, D_MODEL).transpose(1, 0, 2)

    ssm_p0 = jnp.zeros((n_p, N_GROUPS, N_STATE, 2), F32)
    conv_p0 = jnp.zeros((n_p, 2, WIDTH), F32)
    y_p, ssm_p, conv_p = _group(x_prompt.reshape(n_p * seq, D_MODEL), t_p, n_p, 32, 1024,
                                mod_p, ssm_p0, conv_p0, p)
    y_prompt = y_p.reshape(n_p, seq, D_MODEL)

    return (y_prompt, y_sample, ssm_p[None], conv_p[None], ssm_s[None], conv_s[None])
```

```python
import functools

import jax
import jax.numpy as jnp
from jax import lax
from jax.experimental import pallas as pl
from jax.experimental.pallas import tpu as pltpu

F32 = jnp.float32
BF16 = jnp.bfloat16

D_MODEL = 4096
WIDTH = 2048
N_GROUPS = 128
GROUP = 16
N_STATE = 64
GROUPS_PER_BLOCK = 8
N_BLOCKS = N_GROUPS // GROUPS_PER_BLOCK
STATE_LANES = GROUPS_PER_BLOCK * N_STATE
LANES = 128
IN_COLS = 10 * WIDTH
EPS = 1e-6
MIB = 1024 * 1024
V7X_VMEM_BYTES = 64 * MIB
VMEM_LIMIT = V7X_VMEM_BYTES - 8 * MIB
NT_DIMS = (((1,), (1,)), ((), ()))


def _cparams(n_axes, vmem_limit=VMEM_LIMIT):
    return pltpu.CompilerParams(dimension_semantics=("arbitrary",) * n_axes,
                                vmem_limit_bytes=vmem_limit)


def _sigmoid(x):
    return jax.nn.sigmoid(x)


def _ada_kernel(c_ref, w_ref, b_ref, o_ref):
    c = c_ref[...]
    a = (c * _sigmoid(c)).astype(BF16)
    o_ref[...] = jnp.dot(a, w_ref[...].astype(BF16), preferred_element_type=F32) + b_ref[...]


def _ada(c_all, w_ada, b_ada, tn=512):
    rows = c_all.shape[0]
    n_out = w_ada.shape[1]
    return pl.pallas_call(
        _ada_kernel,
        grid=(n_out // tn,),
        in_specs=[pl.BlockSpec((rows, D_MODEL), lambda j: (0, 0)),
                  pl.BlockSpec((D_MODEL, tn), lambda j: (0, j)),
                  pl.BlockSpec((1, tn), lambda j: (0, j))],
        out_specs=pl.BlockSpec((rows, tn), lambda j: (0, j)),
        out_shape=jax.ShapeDtypeStruct((rows, n_out), F32),
        compiler_params=_cparams(1),
        name="ada_mod",
    )(c_all, w_ada, b_ada.reshape(1, n_out))


X_RING = 3


def _norm_kernel(x_ref, g_ref, sc_ref, sh_ref, o_ref, *ring, t_len, ct, tiles_per_seq):
    def norm_mod(x, sc, sh):
        y = x * lax.rsqrt(jnp.mean(x * x, axis=-1, keepdims=True) + EPS) * g_ref[...]
        return (y * (1.0 + sc) + sh).astype(BF16)

    if not tiles_per_seq:
        for s in range(t_len):
            o_ref[s] = norm_mod(x_ref[s * ct:(s + 1) * ct, :], sc_ref[...], sh_ref[...])
        return

    xbuf, sem = ring
    i = pl.program_id(0)
    n_tiles = pl.num_programs(0)
    r = t_len * ct

    def x_copy(tile, slot):
        rows = pl.ds(pl.multiple_of(tile * r, r), r)
        return pltpu.make_async_copy(x_ref.at[rows, :], xbuf.at[slot], sem.at[slot])

    @pl.when(i == 0)
    def _():
        for k in range(X_RING - 1):
            x_copy(k, k).start()

    @pl.when(i + (X_RING - 1) < n_tiles)
    def _():
        x_copy(i + (X_RING - 1), lax.rem(i + (X_RING - 1), X_RING)).start()

    slot = lax.rem(i, X_RING)
    x_copy(i, slot).wait()
    n = i // tiles_per_seq
    y = norm_mod(xbuf[slot], sc_ref[pl.ds(n, 1), :], sh_ref[pl.ds(n, 1), :])
    r_out = lax.broadcasted_iota(jnp.int32, (r, r), 0)
    r_in = lax.broadcasted_iota(jnp.int32, (r, r), 1)
    src_row = (r_out & (ct - 1)) * t_len + lax.shift_right_logical(r_out, ct.bit_length() - 1)
    perm = jnp.where(r_in == src_row, 1.0, 0.0).astype(BF16)
    yp = jnp.dot(perm, y, preferred_element_type=F32)
    o_ref[...] = yp.reshape(t_len, ct, D_MODEL).astype(BF16)


def _norm_mod(x2d, t_len, norm_g, scale, shift, ct, tiles_per_seq):
    assert ct & (ct - 1) == 0
    r = t_len * ct
    m = x2d.shape[0] // t_len
    assert tiles_per_seq or ct == m == scale.shape[0]
    mod_spec = pl.BlockSpec((scale.shape[0], D_MODEL), lambda i: (0, 0))
    if tiles_per_seq:
        assert m // ct >= X_RING - 1
        x_spec = pl.BlockSpec(memory_space=pl.ANY)
        scratch = [pltpu.VMEM((X_RING, r, D_MODEL), F32), pltpu.SemaphoreType.DMA((X_RING,))]
    else:
        x_spec = pl.BlockSpec((r, D_MODEL), lambda i: (i, 0))
        scratch = []
    return pl.pallas_call(
        functools.partial(_norm_kernel, t_len=t_len, ct=ct, tiles_per_seq=tiles_per_seq),
        grid=(m // ct,),
        in_specs=[x_spec,
                  pl.BlockSpec((1, D_MODEL), lambda i: (0, 0)),
                  mod_spec, mod_spec],
        out_specs=pl.BlockSpec((t_len, ct, D_MODEL), lambda i: (0, i, 0)),
        out_shape=jax.ShapeDtypeStruct((t_len, m, D_MODEL), BF16),
        scratch_shapes=scratch,
        compiler_params=_cparams(1),
        name="norm_mod",
    )(x2d, norm_g.reshape(1, D_MODEL), scale, shift)


def _activate(acc, seg):
    is_silu = jnp.logical_or(seg == 1, seg == 5)
    is_sig = seg >= 6
    sig = _sigmoid(acc)
    return jnp.where(is_sig, sig, jnp.where(is_silu, acc * sig, acc)).astype(BF16)


def _inproj_kernel(x_ref, w_ref, o_ref, *, tn):
    acc = jnp.dot(x_ref[...], w_ref[...], preferred_element_type=F32)
    o_ref[...] = _activate(acc, pl.program_id(1) // (WIDTH // tn))


def _inproj_cast_kernel(x_ref, w_ref, o_ref, wbf_ref, *, tn):
    w = w_ref[...].astype(BF16)
    wbf_ref[...] = w
    acc = jnp.dot(x_ref[...], w, preferred_element_type=F32)
    o_ref[...] = _activate(acc, pl.program_id(0) // (WIDTH // tn))


def _inproj_cast(xn, w_in, tn=1024):
    rows = xn.shape[0]
    return pl.pallas_call(
        functools.partial(_inproj_cast_kernel, tn=tn),
        grid=(IN_COLS // tn,),
        in_specs=[pl.BlockSpec((rows, D_MODEL), lambda j: (0, 0)),
                  pl.BlockSpec((D_MODEL, tn), lambda j: (0, j))],
        out_specs=[pl.BlockSpec((rows, tn), lambda j: (0, j)),
                   pl.BlockSpec((D_MODEL, tn), lambda j: (0, j))],
        out_shape=[jax.ShapeDtypeStruct((rows, IN_COLS), BF16),
                   jax.ShapeDtypeStruct((D_MODEL, IN_COLS), BF16)],
        compiler_params=_cparams(1, V7X_VMEM_BYTES - 4 * MIB),
        name="inproj_cast",
    )(xn, w_in)


def _inproj(xn, w_in_bf, tm, tn=1024):
    rows = xn.shape[0]
    return pl.pallas_call(
        functools.partial(_inproj_kernel, tn=tn),
        grid=(rows // tm, IN_COLS // tn),
        in_specs=[pl.BlockSpec((tm, D_MODEL), lambda i, j: (i, 0)),
                  pl.BlockSpec((D_MODEL, tn), lambda i, j: (0, j))],
        out_specs=pl.BlockSpec((tm, tn), lambda i, j: (i, j)),
        out_shape=jax.ShapeDtypeStruct((rows, IN_COLS), BF16),
        compiler_params=_cparams(2),
        name="inproj",
    )(xn, w_in_bf)


def _cmul(ar, ai, br, bi):
    return ar * br - ai * bi, ar * bi + ai * br


def _short_conv(hb_ref, cb_ref, bb_ref, szb_ref, buf0_ref, buf1_ref, w_ref, bin_ref, v_ref, *, t_len, m, n_seq):
    cps = m // n_seq

    def prev_chunk(v, first_rows):
        if cps == 1:
            return first_rows
        out = pltpu.roll(v, 1, 0)
        rid = lax.broadcasted_iota(jnp.int32, v.shape, 0)
        for n in range(n_seq):
            out = jnp.where(rid == n * cps, first_rows[n:n + 1, :], out)
        return out

    def v_of(s):
        return hb_ref[s].astype(F32) * cb_ref[s].astype(F32)

    vm2 = prev_chunk(v_of(t_len - 2), buf0_ref[...])
    vm1 = prev_chunk(v_of(t_len - 1), buf1_ref[...])
    for s in range(t_len):
        v0 = v_of(s)
        conv = vm2 * w_ref[0:1, :] + vm1 * w_ref[1:2, :] + v0 * w_ref[2:3, :]
        bin_ref[s] = (bb_ref[s].astype(F32) * conv * szb_ref[s].astype(F32)).astype(BF16)
        if s >= t_len - 2:
            v_ref[s - (t_len - 2)] = v0
        vm2, vm1 = vm1, v0


def _s5_kernel(u_ref, h0r_ref, h0i_ref, lr_ref, li_ref, ldt_ref, btr_ref, bti_ref, ctr_ref, cti_ref, d_ref,
               hb_ref, cb_ref, bb_ref, szb_ref, buf0_ref, buf1_ref, cw_ref,
               y_ref, hr_out, hi_out, bin_ref, v_ref,
               ucat, bend_r, bend_i, cend_r, cend_i, kpair, hl_r, hl_i, hp_r, hp_i, *, t_len, m, n_seq):
    blk = pl.program_id(0)
    _short_conv(hb_ref, cb_ref, bb_ref, szb_ref, buf0_ref, buf1_ref, cw_ref, bin_ref, v_ref,
                t_len=t_len, m=m, n_seq=n_seq)

    @pl.when(blk == 0)
    def _():
        for ref in (bend_r, bend_i, cend_r, cend_i, kpair):
            ref[...] = jnp.zeros(ref.shape, ref.dtype)

    lam_r = lr_ref[0]
    lam_i = li_ref[0]
    dt = jnp.exp(ldt_ref[0])
    mag = jnp.exp(lam_r * dt)
    lb_r = mag * jnp.cos(lam_i * dt)
    lb_i = mag * jnp.sin(lam_i * dt)
    den = lam_r * lam_r + lam_i * lam_i
    nr = lb_r - 1.0
    co_r = (nr * lam_r + lb_i * lam_i) / den
    co_i = (lb_i * lam_r - nr * lam_i) / den
    bb_r, bb_i = _cmul(co_r, co_i, btr_ref[0], bti_ref[0])
    c_r = ctr_ref[0]
    c_i = cti_ref[0]
    pw = [(jnp.ones_like(lb_r), jnp.zeros_like(lb_r))]
    for _ in range(t_len):
        pw.append(_cmul(pw[-1][0], pw[-1][1], lb_r, lb_i))
    ar, ai = pw[t_len]

    lane = lax.broadcasted_iota(jnp.int32, (GROUP, LANES), 1)
    lo = lane < N_STATE

    def pair_tile(x, q):
        slab = x[:, q * LANES:(q + 1) * LANES]
        return jnp.concatenate([jnp.where(lo, slab, 0.0), jnp.where(lo, 0.0, slab)], axis=0).astype(BF16)

    grp_row = lax.broadcasted_iota(jnp.int32, (LANES, STATE_LANES), 0) // GROUP
    grp_lane = lax.broadcasted_iota(jnp.int32, (LANES, STATE_LANES), 1) // N_STATE
    same_group = grp_row == grp_lane

    def expand(x):
        return jnp.where(same_group, jnp.concatenate([x] * GROUPS_PER_BLOCK, axis=0), 0.0).astype(BF16)

    bbx_r = expand(bb_r)
    bbx_i = expand(bb_i)
    for s in range(t_len):
        ucat[:, s * LANES:(s + 1) * LANES] = u_ref[s]
        be_r, be_i = _cmul(pw[t_len - 1 - s][0], pw[t_len - 1 - s][1], bb_r, bb_i)
        ce_r, ce_i = _cmul(pw[s + 1][0], pw[s + 1][1], c_r, c_i)
        for q in range(4):
            r0 = s * LANES + q * 2 * GROUP
            rows = slice(r0, r0 + 2 * GROUP)
            cols = slice(q * LANES, (q + 1) * LANES)
            bend_r[rows, cols] = pair_tile(be_r, q)
            bend_i[rows, cols] = pair_tile(be_i, q)
            cend_r[rows, cols] = pair_tile(ce_r, q)
            cend_i[rows, cols] = pair_tile(-ce_i, q)
        cl_r, cl_i = _cmul(pw[s][0], pw[s][1], c_r, c_i)
        k_lag = (lax.dot_general(bbx_r, expand(cl_r), NT_DIMS, preferred_element_type=F32)
                 - lax.dot_general(bbx_i, expand(cl_i), NT_DIMS, preferred_element_type=F32))
        k_lag = k_lag.astype(BF16)
        kpair[(t_len - 1 - s) * LANES:(t_len - s) * LANES, LANES:] = k_lag
        if s <= t_len - 2:
            kpair[(t_len - 2 - s) * LANES:(t_len - 1 - s) * LANES, :LANES] = k_lag

    uc = ucat[...]
    hl_r[...] = jnp.dot(uc, bend_r[...], preferred_element_type=F32)
    hl_i[...] = jnp.dot(uc, bend_i[...], preferred_element_type=F32)

    cps = m // n_seq
    if cps == 1:
        h_r = h0r_ref[0]
        h_i = h0i_ref[0]
        hp_r[...] = h_r
        hp_i[...] = h_i
        hr_out[0] = ar * h_r - ai * h_i + hl_r[...]
        hi_out[0] = ar * h_i + ai * h_r + hl_i[...]
    else:
        def body(c, carry):
            new = []
            for n in range(n_seq):
                h_r, h_i = carry[n]
                row = pl.ds(n * cps + c, 1)
                hp_r[row, :] = h_r
                hp_i[row, :] = h_i
                new.append((ar * h_r - ai * h_i + hl_r[row, :], ar * h_i + ai * h_r + hl_i[row, :]))
            return tuple(new)

        init = tuple((h0r_ref[0, n:n + 1, :], h0i_ref[0, n:n + 1, :]) for n in range(n_seq))
        fin = lax.fori_loop(0, cps, body, init)
        for n in range(n_seq):
            hr_out[0, n:n + 1, :] = fin[n][0]
            hi_out[0, n:n + 1, :] = fin[n][1]

    ycar = (lax.dot_general(hp_r[...].astype(BF16), cend_r[...], NT_DIMS, preferred_element_type=F32)
            + lax.dot_general(hp_i[...].astype(BF16), cend_i[...], NT_DIMS, preferred_element_type=F32))
    dvec = d_ref[0]
    for t in range(0, t_len, 2):
        y2 = jnp.dot(ucat[:, :(t + 2) * LANES], kpair[(t_len - 2 - t) * LANES:, :],
                     preferred_element_type=F32) + ycar[:, t * LANES:(t + 2) * LANES]
        for k in range(2):
            y = y2[:, k * LANES:(k + 1) * LANES] + dvec * u_ref[t + k].astype(F32)
            y_ref[t + k] = jax.nn.gelu(y).astype(BF16)


def _s5_params(lam_re, lam_im, log_dt, b_re, b_im, c_re, c_im, d_skip):
    row = lambda x: x.astype(F32).reshape(N_BLOCKS, 1, STATE_LANES)
    ldt = jnp.broadcast_to(log_dt.astype(F32)[:, None], (N_GROUPS, N_STATE))
    bt = lambda x: (x.astype(F32).reshape(N_BLOCKS, GROUPS_PER_BLOCK, N_STATE, GROUP)
                    .transpose(0, 3, 1, 2).reshape(N_BLOCKS, GROUP, STATE_LANES))
    ct = lambda x: (x.astype(F32).reshape(N_BLOCKS, GROUPS_PER_BLOCK, GROUP, N_STATE)
                    .transpose(0, 2, 1, 3).reshape(N_BLOCKS, GROUP, STATE_LANES))
    return (row(lam_re), row(lam_im), row(ldt), bt(b_re), bt(b_im), ct(c_re), ct(c_im),
            d_skip.astype(F32).reshape(N_BLOCKS, 1, LANES))


def _s5(proj3, h0_re, h0_im, s5p, buf0, buf1, conv_w, t_len, m, n_seq):
    kern = functools.partial(_s5_kernel, t_len=t_len, m=m, n_seq=n_seq)
    blk3 = lambda b: (b, 0, 0)
    row_spec = pl.BlockSpec((1, 1, STATE_LANES), blk3)
    mat_spec = pl.BlockSpec((1, GROUP, STATE_LANES), blk3)
    state_spec = pl.BlockSpec((1, n_seq, STATE_LANES), blk3)
    end_mat = pltpu.VMEM((t_len * LANES, STATE_LANES), BF16)
    chunk_state = pltpu.VMEM((m, STATE_LANES), F32)
    nbw = WIDTH // LANES
    seg_spec = lambda seg: pl.BlockSpec((t_len, m, LANES), lambda b: (0, 0, seg * nbw + b))
    return pl.pallas_call(
        kern,
        grid=(N_BLOCKS,),
        in_specs=[seg_spec(0),
                  state_spec, state_spec,
                  row_spec, row_spec, row_spec,
                  mat_spec, mat_spec, mat_spec, mat_spec,
                  pl.BlockSpec((1, 1, LANES), blk3),
                  seg_spec(2), seg_spec(3), seg_spec(4), seg_spec(5),
                  pl.BlockSpec((n_seq, LANES), lambda b: (0, b)),
                  pl.BlockSpec((n_seq, LANES), lambda b: (0, b)),
                  pl.BlockSpec((3, LANES), lambda b: (0, b))],
        out_specs=[seg_spec(0), state_spec, state_spec, seg_spec(0),
                   pl.BlockSpec((2, m, LANES), lambda b: (0, 0, b))],
        out_shape=[jax.ShapeDtypeStruct((t_len, m, WIDTH), BF16),
                   jax.ShapeDtypeStruct((N_BLOCKS, n_seq, STATE_LANES), F32),
                   jax.ShapeDtypeStruct((N_BLOCKS, n_seq, STATE_LANES), F32),
                   jax.ShapeDtypeStruct((t_len, m, WIDTH), BF16),
                   jax.ShapeDtypeStruct((2, m, WIDTH), F32)],
        scratch_shapes=[pltpu.VMEM((m, t_len * LANES), BF16),
                        end_mat, end_mat, end_mat, end_mat,
                        pltpu.VMEM((t_len * LANES, 2 * LANES), BF16),
                        chunk_state, chunk_state, chunk_state, chunk_state],
        compiler_params=_cparams(1),
        name="s5_scan",
    )(proj3, h0_re, h0_im, *s5p, proj3, proj3, proj3, proj3, buf0, buf1, conv_w)


def _glu_kernel(ya_ref, sza_ref, w_ref, b_ref, o_ref):
    ya = ya_ref[...]
    glu = jnp.dot(ya, w_ref[...], preferred_element_type=F32) + b_ref[...]
    o_ref[...] = (ya.astype(F32) * _sigmoid(glu) * sza_ref[...].astype(F32)).astype(BF16)


def _glu_cast_kernel(ya_ref, sza_ref, w_ref, b_ref, o_ref, wbf_ref, *, tn):
    w = w_ref[...].astype(BF16)
    wbf_ref[...] = w
    glu = jnp.dot(ya_ref[...], w, preferred_element_type=F32) + b_ref[...]
    ya = ya_ref[:, pl.ds(pl.multiple_of(pl.program_id(0) * tn, tn), tn)].astype(F32)
    o_ref[...] = (ya * _sigmoid(glu) * sza_ref[...].astype(F32)).astype(BF16)


def _glu(ya, proj, w_glu, b_glu, tm=1024, tn=512):
    rows = ya.shape[0]
    out_shape = jax.ShapeDtypeStruct((rows, WIDTH), BF16)
    if w_glu.dtype == BF16:
        return pl.pallas_call(
            _glu_kernel,
            grid=(rows // tm,),
            in_specs=[pl.BlockSpec((tm, WIDTH), lambda i: (i, 0)),
                      pl.BlockSpec((tm, WIDTH), lambda i: (i, 1)),
                      pl.BlockSpec((WIDTH, WIDTH), lambda i: (0, 0), pipeline_mode=pl.Buffered(1)),
                      pl.BlockSpec((1, WIDTH), lambda i: (0, 0))],
            out_specs=pl.BlockSpec((tm, WIDTH), lambda i: (i, 0)),
            out_shape=out_shape,
            compiler_params=_cparams(1),
            name="glu_gate",
        )(ya, proj, w_glu, b_glu.reshape(1, WIDTH))
    nb = WIDTH // tn
    w_spec = pl.BlockSpec((WIDTH, tn), lambda j: (0, j))
    return pl.pallas_call(
        functools.partial(_glu_cast_kernel, tn=tn),
        grid=(nb,),
        in_specs=[pl.BlockSpec((rows, WIDTH), lambda j: (0, 0)),
                  pl.BlockSpec((rows, tn), lambda j: (0, nb + j)),
                  w_spec,
                  pl.BlockSpec((1, tn), lambda j: (0, j))],
        out_specs=[pl.BlockSpec((rows, tn), lambda j: (0, j)), w_spec],
        out_shape=[out_shape, jax.ShapeDtypeStruct((WIDTH, WIDTH), BF16)],
        compiler_params=_cparams(1),
        name="glu_gate_cast",
    )(ya, proj, w_glu, b_glu.reshape(1, WIDTH))


PERM_ROWS = 512


def _to_natural(pm, t_len):
    rows, n = pm.shape
    c = rows // t_len
    cg = PERM_ROWS // t_len
    r_nat = lax.broadcasted_iota(jnp.int32, (PERM_ROWS, PERM_ROWS), 0)
    r_in = lax.broadcasted_iota(jnp.int32, (PERM_ROWS, PERM_ROWS), 1)
    src = (r_nat & (t_len - 1)) * cg + lax.shift_right_logical(r_nat, t_len.bit_length() - 1)
    perm = jnp.where(r_in == src, 1.0, 0.0).astype(BF16)
    pm3 = pm.reshape(t_len, c, n)
    out = []
    for g in range(c // cg):
        grp = pm3[:, g * cg:(g + 1) * cg, :].reshape(PERM_ROWS, n).astype(BF16)
        out.append(jnp.dot(perm, grp, preferred_element_type=F32).astype(BF16))
    return out[0] if len(out) == 1 else jnp.concatenate(out, axis=0)


def _gated_merge(a_ref, b_ref, wa, wb, sga_ref, sgb_ref, o_ref, natural):
    t_len, c, _ = a_ref.shape
    rows = t_len * c
    pa = jnp.dot(a_ref[...].reshape(rows, WIDTH), wa, preferred_element_type=F32)
    pb = jnp.dot(b_ref[...].reshape(rows, WIDTH), wb, preferred_element_type=F32)
    tn = pa.shape[1]
    pm = sga_ref[...].reshape(rows, tn) * pa + sgb_ref[...].reshape(rows, tn) * pb
    o_ref[...] = _to_natural(pm, t_len) if natural else pm.astype(BF16)


def _merge_kernel(a_ref, b_ref, wa_ref, wb_ref, sga_ref, sgb_ref, o_ref, *, natural):
    _gated_merge(a_ref, b_ref, wa_ref[...], wb_ref[...], sga_ref, sgb_ref, o_ref, natural)


def _merge_cast_kernel(a_ref, b_ref, wa_ref, wb_ref, sga_ref, sgb_ref, o_ref, wabf_ref, wbbf_ref, *, natural):
    wa = wa_ref[...].astype(BF16)
    wb = wb_ref[...].astype(BF16)
    wabf_ref[...] = wa
    wbbf_ref[...] = wb
    _gated_merge(a_ref, b_ref, wa, wb, sga_ref, sgb_ref, o_ref, natural)


def _merge(a_in3, b_in3, proj3, w_pa, w_pb, c, natural):
    t_len, m, _ = a_in3.shape
    cast = w_pa.dtype != BF16
    tn = 512 if cast else 1024
    nb = WIDTH // tn
    assert not cast or c == m
    assert not natural or (PERM_ROWS % t_len == 0 and c % (PERM_ROWS // t_len) == 0)
    w_spec = pl.BlockSpec((WIDTH, tn), lambda i, j: (0, j))
    out_spec = pl.BlockSpec((t_len * c, tn), lambda i, j: (i, j))
    out_shape = jax.ShapeDtypeStruct((t_len * m, D_MODEL), BF16)
    w_shape = jax.ShapeDtypeStruct((WIDTH, D_MODEL), BF16)
    return pl.pallas_call(
        functools.partial(_merge_cast_kernel if cast else _merge_kernel, natural=natural),
        grid=(m // c, D_MODEL // tn),
        in_specs=[pl.BlockSpec((t_len, c, WIDTH), lambda i, j: (0, i, 0)),
                  pl.BlockSpec((t_len, c, WIDTH), lambda i, j: (0, i, 0)),
                  w_spec, w_spec,
                  pl.BlockSpec((t_len, c, tn), lambda i, j: (0, i, 6 * nb + j)),
                  pl.BlockSpec((t_len, c, tn), lambda i, j: (0, i, 8 * nb + j))],
        out_specs=[out_spec, w_spec, w_spec] if cast else out_spec,
        out_shape=[out_shape, w_shape, w_shape] if cast else out_shape,
        compiler_params=_cparams(2),
        name="merge_cast" if cast else "merge",
    )(a_in3, b_in3, w_pa, w_pb, proj3, proj3)


def _outproj_kernel(m_ref, w_ref, x_ref, gate_ref, fg_ref, o_ref, *rest, tn, tiles_per_seq):
    j = pl.program_id(1)
    w = w_ref[...]
    if rest:
        (wbf_ref,) = rest
        w = w.astype(BF16)
        wbf_ref[...] = w
    part = jnp.dot(m_ref[...], w, preferred_element_type=F32)
    col = pl.ds(pl.multiple_of(j * tn, tn), tn)
    tm = o_ref.shape[0]
    if tiles_per_seq:
        gate = gate_ref[pl.ds(pl.program_id(0) // tiles_per_seq, 1), :]
        o_ref[:, col] = x_ref[...] + gate * part
    else:
        n_rows = gate_ref.shape[0]
        for r in range(0, tm, n_rows):
            o_ref[r:r + n_rows, col] = x_ref[r:r + n_rows, :] + gate_ref[...] * part[r:r + n_rows, :]

    @pl.when(j == pl.num_programs(1) - 1)
    def _():
        for r in range(0, tm, LANES):
            h = o_ref[r:r + LANES, :]
            o_ref[r:r + LANES, :] = h * lax.rsqrt(jnp.mean(h * h, axis=-1, keepdims=True) + EPS) * fg_ref[...]


def _outproj(merged, w_o, x2d, gate, final_g, tm, tiles_per_seq):
    rows = x2d.shape[0]
    cast = w_o.dtype != BF16
    tn = 512 if cast else 1024
    assert tiles_per_seq or tm % gate.shape[0] == 0
    assert not cast or tm == rows
    w_spec = pl.BlockSpec((D_MODEL, tn), lambda i, j: (0, j))
    out_spec = pl.BlockSpec((tm, D_MODEL), lambda i, j: (i, 0))
    out_shape = jax.ShapeDtypeStruct((rows, D_MODEL), F32)
    return pl.pallas_call(
        functools.partial(_outproj_kernel, tn=tn, tiles_per_seq=tiles_per_seq),
        grid=(rows // tm, D_MODEL // tn),
        in_specs=[pl.BlockSpec((tm, D_MODEL), lambda i, j: (i, 0)),
                  w_spec,
                  pl.BlockSpec((tm, tn), lambda i, j: (i, j)),
                  pl.BlockSpec((gate.shape[0], tn), lambda i, j: (0, j)),
                  pl.BlockSpec((1, D_MODEL), lambda i, j: (0, 0))],
        out_specs=[out_spec, w_spec] if cast else out_spec,
        out_shape=[out_shape, jax.ShapeDtypeStruct((D_MODEL, D_MODEL), BF16)] if cast else out_shape,
        compiler_params=_cparams(2),
        name="outproj_cast" if cast else "outproj",
    )(merged, w_o, x2d, gate, final_g.reshape(1, D_MODEL))


def _state_to_blocks(state):
    n = state.shape[0]
    re = state[..., 0].reshape(n, N_BLOCKS, STATE_LANES).transpose(1, 0, 2)
    im = state[..., 1].reshape(n, N_BLOCKS, STATE_LANES).transpose(1, 0, 2)
    return re, im


def _blocks_to_state(re, im):
    n = re.shape[1]
    re = re.transpose(1, 0, 2).reshape(n, N_GROUPS, N_STATE)
    im = im.transpose(1, 0, 2).reshape(n, N_GROUPS, N_STATE)
    return jnp.stack([re, im], axis=-1)


def _group(x2d, t_len, n_seq, ct, tm, mod, ssm0, conv0, p):
    m = x2d.shape[0] // t_len
    cps = m // n_seq
    natural = cps > 1
    tiles_per_seq = cps // ct if natural else 0
    shift, scale, gate = mod
    xn = _norm_mod(x2d, t_len, p["norm_g"], scale, shift, ct, tiles_per_seq)
    xn = xn.reshape(t_len * m, D_MODEL)
    if "w_in_bf" in p:
        proj = _inproj(xn, p["w_in_bf"], tm)
    else:
        proj, p["w_in_bf"] = _inproj_cast(xn, p["w_in"])
    proj3 = proj.reshape(t_len, m, IN_COLS)
    h0_re, h0_im = _state_to_blocks(ssm0)
    ya3, hr, hi, b_in3, v_last = _s5(proj3, h0_re, h0_im, p["s5"], conv0[:, 0], conv0[:, 1], p["conv_w"],
                                     t_len, m, n_seq)
    ya = ya3.reshape(t_len * m, WIDTH)
    if "w_glu_bf" in p:
        a_in = _glu(ya, proj, p["w_glu_bf"], p["b_glu"])
    else:
        a_in, p["w_glu_bf"] = _glu(ya, proj, p["w_glu"], p["b_glu"])
    a_in3 = a_in.reshape(t_len, m, WIDTH)
    if "w_pa_bf" in p:
        merged = _merge(a_in3, b_in3, proj3, p["w_pa_bf"], p["w_pb_bf"], tm // t_len, natural)
    else:
        merged, p["w_pa_bf"], p["w_pb_bf"] = _merge(a_in3, b_in3, proj3, p["w_pa"], p["w_pb"], m, natural)
    if "w_o_bf" in p:
        out = _outproj(merged, p["w_o_bf"], x2d, gate, p["final_g"], ct * t_len, tiles_per_seq)
    else:
        out, p["w_o_bf"] = _outproj(merged, p["w_o"], x2d, gate, p["final_g"], ct * t_len, tiles_per_seq)
    conv_new = v_last[:, cps - 1::cps, :].transpose(1, 0, 2)
    return out, _blocks_to_state(hr, hi), conv_new


def kernel(x_prompt, x_sample, state_ssm, state_conv, c_prompt, c_sample, norm_g, w_ada, b_ada, w_in, lam_re, lam_im, log_dt, b_re, b_im, c_re, c_im, d_skip, w_glu, b_glu, w_pa, conv_w, w_pb, w_o, final_g):
    depth = norm_g.shape[0]
    assert depth == 1
    n_p, seq, _ = x_prompt.shape
    n_s, dec, _ = x_sample.shape
    t_p = 16
    assert seq % t_p == 0

    l = 0
    p = dict(norm_g=norm_g[l], w_in=w_in[l], conv_w=conv_w[l], w_glu=w_glu[l], b_glu=b_glu[l],
             w_pa=w_pa[l], w_pb=w_pb[l], w_o=w_o[l], final_g=final_g,
             s5=_s5_params(lam_re[l], lam_im[l], log_dt[l], b_re[l], b_im[l], c_re[l], c_im[l], d_skip[l]))

    n_c = n_p + n_s
    pad = (-n_c) % 16
    c_all = jnp.concatenate([c_prompt, c_sample, jnp.zeros((pad, D_MODEL), F32)], axis=0)
    mod = _ada(c_all, w_ada[l], b_ada[l])
    mod_p = tuple(mod[:n_p, k * D_MODEL:(k + 1) * D_MODEL] for k in range(3))
    mod_s = tuple(mod[n_p:n_c, k * D_MODEL:(k + 1) * D_MODEL] for k in range(3))

    x_s = x_sample.transpose(1, 0, 2).reshape(dec * n_s, D_MODEL)
    y_s, ssm_s, conv_s = _group(x_s, dec, n_s, n_s, dec * n_s, mod_s, state_ssm[l], state_conv[l], p)
    y_sample = y_s.reshape(dec, n_s, D_MODEL).transpose(1, 0, 2)

    ssm_p0 = jnp.zeros((n_p, N_GROUPS, N_STATE, 2), F32)
    conv_p0 = jnp.zeros((n_p, 2, WIDTH), F32)
    y_p, ssm_p, conv_p = _group(x_prompt.reshape(n_p * seq, D_MODEL), t_p, n_p, 32, 1024,
                                mod_p, ssm_p0, conv_p0, p)
    y_prompt = y_p.reshape(n_p, seq, D_MODEL)

    return (y_prompt, y_sample, ssm_p[None], conv_p[None], ssm_s[None], conv_s[None])
```

```python
import functools

import jax
import jax.numpy as jnp
from jax import lax
from jax.experimental import pallas as pl
from jax.experimental.pallas import tpu as pltpu

F32 = jnp.float32
BF16 = jnp.bfloat16

D_MODEL = 4096
WIDTH = 2048
N_GROUPS = 128
GROUP = 16
N_STATE = 64
GROUPS_PER_BLOCK = 8
N_BLOCKS = N_GROUPS // GROUPS_PER_BLOCK
STATE_LANES = GROUPS_PER_BLOCK * N_STATE
LANES = 128
IN_COLS = 10 * WIDTH
EPS = 1e-6
MIB = 1024 * 1024
V7X_VMEM_BYTES = 64 * MIB
VMEM_LIMIT = V7X_VMEM_BYTES - 8 * MIB
NT_DIMS = (((1,), (1,)), ((), ()))


def _cparams(n_axes, vmem_limit=VMEM_LIMIT):
    return pltpu.CompilerParams(dimension_semantics=("arbitrary",) * n_axes,
                                vmem_limit_bytes=vmem_limit)


def _sigmoid(x):
    return jax.nn.sigmoid(x)


def _ada_kernel(c_ref, w_hbm, b_ref, o_ref, wbuf, sem, *, tn):
    j = pl.program_id(0)
    n_blocks = pl.num_programs(0)

    def w_copy(blk, slot):
        cols = pl.ds(pl.multiple_of(blk * tn, tn), tn)
        return pltpu.make_async_copy(w_hbm.at[:, cols], wbuf.at[slot], sem.at[slot])

    @pl.when(j == 0)
    def _():
        for k in range(X_RING - 1):
            w_copy(k, k).start()

    @pl.when(j + (X_RING - 1) < n_blocks)
    def _():
        w_copy(j + (X_RING - 1), lax.rem(j + (X_RING - 1), X_RING)).start()

    slot = lax.rem(j, X_RING)
    w_copy(j, slot).wait()
    c = c_ref[...]
    a = (c * _sigmoid(c)).astype(BF16)
    o_ref[...] = jnp.dot(a, wbuf[slot].astype(BF16), preferred_element_type=F32) + b_ref[...]


def _ada(c_all, w_ada, b_ada, tn=512):
    rows = c_all.shape[0]
    n_out = w_ada.shape[1]
    assert n_out // tn >= X_RING - 1
    return pl.pallas_call(
        functools.partial(_ada_kernel, tn=tn),
        grid=(n_out // tn,),
        in_specs=[pl.BlockSpec((rows, D_MODEL), lambda j: (0, 0)),
                  pl.BlockSpec(memory_space=pl.ANY),
                  pl.BlockSpec((1, tn), lambda j: (0, j))],
        out_specs=pl.BlockSpec((rows, tn), lambda j: (0, j)),
        out_shape=jax.ShapeDtypeStruct((rows, n_out), F32),
        scratch_shapes=[pltpu.VMEM((X_RING, D_MODEL, tn), F32), pltpu.SemaphoreType.DMA((X_RING,))],
        compiler_params=_cparams(1),
        name="ada_mod",
    )(c_all, w_ada, b_ada.reshape(1, n_out))


X_RING = 3


def _norm_kernel(x_ref, g_ref, sc_ref, sh_ref, o_ref, *ring, t_len, ct, tiles_per_seq):
    def norm_mod(x, sc, sh):
        y = x * lax.rsqrt(jnp.mean(x * x, axis=-1, keepdims=True) + EPS) * g_ref[...]
        return (y * (1.0 + sc) + sh).astype(BF16)

    if not tiles_per_seq:
        for s in range(t_len):
            o_ref[s] = norm_mod(x_ref[s * ct:(s + 1) * ct, :], sc_ref[...], sh_ref[...])
        return

    xbuf, sem = ring
    i = pl.program_id(0)
    n_tiles = pl.num_programs(0)
    r = t_len * ct

    def x_copy(tile, slot):
        rows = pl.ds(pl.multiple_of(tile * r, r), r)
        return pltpu.make_async_copy(x_ref.at[rows, :], xbuf.at[slot], sem.at[slot])

    @pl.when(i == 0)
    def _():
        for k in range(X_RING - 1):
            x_copy(k, k).start()

    @pl.when(i + (X_RING - 1) < n_tiles)
    def _():
        x_copy(i + (X_RING - 1), lax.rem(i + (X_RING - 1), X_RING)).start()

    slot = lax.rem(i, X_RING)
    x_copy(i, slot).wait()
    n = i // tiles_per_seq
    y = norm_mod(xbuf[slot], sc_ref[pl.ds(n, 1), :], sh_ref[pl.ds(n, 1), :])
    r_out = lax.broadcasted_iota(jnp.int32, (r, r), 0)
    r_in = lax.broadcasted_iota(jnp.int32, (r, r), 1)
    src_row = (r_out & (ct - 1)) * t_len + lax.shift_right_logical(r_out, ct.bit_length() - 1)
    perm = jnp.where(r_in == src_row, 1.0, 0.0).astype(BF16)
    yp = jnp.dot(perm, y, preferred_element_type=F32)
    o_ref[...] = yp.reshape(t_len, ct, D_MODEL).astype(BF16)


def _norm_mod(x2d, t_len, norm_g, scale, shift, ct, tiles_per_seq):
    assert ct & (ct - 1) == 0
    r = t_len * ct
    m = x2d.shape[0] // t_len
    assert tiles_per_seq or ct == m == scale.shape[0]
    mod_spec = pl.BlockSpec((scale.shape[0], D_MODEL), lambda i: (0, 0))
    if tiles_per_seq:
        assert m // ct >= X_RING - 1
        x_spec = pl.BlockSpec(memory_space=pl.ANY)
        scratch = [pltpu.VMEM((X_RING, r, D_MODEL), F32), pltpu.SemaphoreType.DMA((X_RING,))]
    else:
        x_spec = pl.BlockSpec((r, D_MODEL), lambda i: (i, 0))
        scratch = []
    return pl.pallas_call(
        functools.partial(_norm_kernel, t_len=t_len, ct=ct, tiles_per_seq=tiles_per_seq),
        grid=(m // ct,),
        in_specs=[x_spec,
                  pl.BlockSpec((1, D_MODEL), lambda i: (0, 0)),
                  mod_spec, mod_spec],
        out_specs=pl.BlockSpec((t_len, ct, D_MODEL), lambda i: (0, i, 0)),
        out_shape=jax.ShapeDtypeStruct((t_len, m, D_MODEL), BF16),
        scratch_shapes=scratch,
        compiler_params=_cparams(1),
        name="norm_mod",
    )(x2d, norm_g.reshape(1, D_MODEL), scale, shift)


def _activate(acc, seg):
    is_silu = jnp.logical_or(seg == 1, seg == 5)
    is_sig = seg >= 6
    sig = _sigmoid(acc)
    return jnp.where(is_sig, sig, jnp.where(is_silu, acc * sig, acc)).astype(BF16)


def _inproj_kernel(x_ref, w_ref, o_ref, *, tn):
    acc = jnp.dot(x_ref[...], w_ref[...], preferred_element_type=F32)
    o_ref[...] = _activate(acc, pl.program_id(1) // (WIDTH // tn))


def _inproj_cast_kernel(x_ref, w_ref, o_ref, wbf_ref, *, tn):
    w = w_ref[...].astype(BF16)
    wbf_ref[...] = w
    acc = jnp.dot(x_ref[...], w, preferred_element_type=F32)
    o_ref[...] = _activate(acc, pl.program_id(0) // (WIDTH // tn))


def _inproj_cast(xn, w_in, tn=1024):
    rows = xn.shape[0]
    return pl.pallas_call(
        functools.partial(_inproj_cast_kernel, tn=tn),
        grid=(IN_COLS // tn,),
        in_specs=[pl.BlockSpec((rows, D_MODEL), lambda j: (0, 0)),
                  pl.BlockSpec((D_MODEL, tn), lambda j: (0, j))],
        out_specs=[pl.BlockSpec((rows, tn), lambda j: (0, j)),
                   pl.BlockSpec((D_MODEL, tn), lambda j: (0, j))],
        out_shape=[jax.ShapeDtypeStruct((rows, IN_COLS), BF16),
                   jax.ShapeDtypeStruct((D_MODEL, IN_COLS), BF16)],
        compiler_params=_cparams(1, V7X_VMEM_BYTES - 4 * MIB),
        name="inproj_cast",
    )(xn, w_in)


def _inproj(xn, w_in_bf, tm, tn=1024):
    rows = xn.shape[0]
    return pl.pallas_call(
        functools.partial(_inproj_kernel, tn=tn),
        grid=(rows // tm, IN_COLS // tn),
        in_specs=[pl.BlockSpec((tm, D_MODEL), lambda i, j: (i, 0)),
                  pl.BlockSpec((D_MODEL, tn), lambda i, j: (0, j))],
        out_specs=pl.BlockSpec((tm, tn), lambda i, j: (i, j)),
        out_shape=jax.ShapeDtypeStruct((rows, IN_COLS), BF16),
        compiler_params=_cparams(2),
        name="inproj",
    )(xn, w_in_bf)


def _cmul(ar, ai, br, bi):
    return ar * br - ai * bi, ar * bi + ai * br


def _short_conv(hb_ref, cb_ref, bb_ref, szb_ref, buf0_ref, buf1_ref, w_ref, bin_ref, v_ref, *, t_len, m, n_seq):
    cps = m // n_seq

    def prev_chunk(v, first_rows):
        if cps == 1:
            return first_rows
        out = pltpu.roll(v, 1, 0)
        rid = lax.broadcasted_iota(jnp.int32, v.shape, 0)
        for n in range(n_seq):
            out = jnp.where(rid == n * cps, first_rows[n:n + 1, :], out)
        return out

    def v_of(s):
        return hb_ref[s].astype(F32) * cb_ref[s].astype(F32)

    vm2 = prev_chunk(v_of(t_len - 2), buf0_ref[...])
    vm1 = prev_chunk(v_of(t_len - 1), buf1_ref[...])
    for s in range(t_len):
        v0 = v_of(s)
        conv = vm2 * w_ref[0:1, :] + vm1 * w_ref[1:2, :] + v0 * w_ref[2:3, :]
        bin_ref[s] = (bb_ref[s].astype(F32) * conv * szb_ref[s].astype(F32)).astype(BF16)
        if s >= t_len - 2:
            v_ref[s - (t_len - 2)] = v0
        vm2, vm1 = vm1, v0


def _s5_kernel(u_ref, h0r_ref, h0i_ref, lr_ref, li_ref, ldt_ref, btr_ref, bti_ref, ctr_ref, cti_ref, d_ref,
               hb_ref, cb_ref, bb_ref, szb_ref, buf0_ref, buf1_ref, cw_ref,
               y_ref, hr_out, hi_out, bin_ref, v_ref,
               ucat, bend_r, bend_i, cend_r, cend_i, kpair, hl_r, hl_i, hp_r, hp_i, *, t_len, m, n_seq):
    blk = pl.program_id(0)
    _short_conv(hb_ref, cb_ref, bb_ref, szb_ref, buf0_ref, buf1_ref, cw_ref, bin_ref, v_ref,
                t_len=t_len, m=m, n_seq=n_seq)

    @pl.when(blk == 0)
    def _():
        for ref in (bend_r, bend_i, cend_r, cend_i, kpair):
            ref[...] = jnp.zeros(ref.shape, ref.dtype)

    lam_r = lr_ref[0]
    lam_i = li_ref[0]
    dt = jnp.exp(ldt_ref[0])
    mag = jnp.exp(lam_r * dt)
    lb_r = mag * jnp.cos(lam_i * dt)
    lb_i = mag * jnp.sin(lam_i * dt)
    den = lam_r * lam_r + lam_i * lam_i
    nr = lb_r - 1.0
    co_r = (nr * lam_r + lb_i * lam_i) / den
    co_i = (lb_i * lam_r - nr * lam_i) / den
    bb_r, bb_i = _cmul(co_r, co_i, btr_ref[0], bti_ref[0])
    c_r = ctr_ref[0]
    c_i = cti_ref[0]
    pw = [(jnp.ones_like(lb_r), jnp.zeros_like(lb_r))]
    for _ in range(t_len):
        pw.append(_cmul(pw[-1][0], pw[-1][1], lb_r, lb_i))
    ar, ai = pw[t_len]

    lane = lax.broadcasted_iota(jnp.int32, (GROUP, LANES), 1)
    lo = lane < N_STATE

    def pair_tile(x, q):
        slab = x[:, q * LANES:(q + 1) * LANES]
        return jnp.concatenate([jnp.where(lo, slab, 0.0), jnp.where(lo, 0.0, slab)], axis=0).astype(BF16)

    grp_row = lax.broadcasted_iota(jnp.int32, (LANES, STATE_LANES), 0) // GROUP
    grp_lane = lax.broadcasted_iota(jnp.int32, (LANES, STATE_LANES), 1) // N_STATE
    same_group = grp_row == grp_lane

    def expand(x):
        return jnp.where(same_group, jnp.concatenate([x] * GROUPS_PER_BLOCK, axis=0), 0.0).astype(BF16)

    bbx_r = expand(bb_r)
    bbx_i = expand(bb_i)
    for s in range(t_len):
        ucat[:, s * LANES:(s + 1) * LANES] = u_ref[s]
        be_r, be_i = _cmul(pw[t_len - 1 - s][0], pw[t_len - 1 - s][1], bb_r, bb_i)
        ce_r, ce_i = _cmul(pw[s + 1][0], pw[s + 1][1], c_r, c_i)
        for q in range(4):
            r0 = s * LANES + q * 2 * GROUP
            rows = slice(r0, r0 + 2 * GROUP)
            cols = slice(q * LANES, (q + 1) * LANES)
            bend_r[rows, cols] = pair_tile(be_r, q)
            bend_i[rows, cols] = pair_tile(be_i, q)
            cend_r[rows, cols] = pair_tile(ce_r, q)
            cend_i[rows, cols] = pair_tile(-ce_i, q)
        cl_r, cl_i = _cmul(pw[s][0], pw[s][1], c_r, c_i)
        k_lag = (lax.dot_general(bbx_r, expand(cl_r), NT_DIMS, preferred_element_type=F32)
                 - lax.dot_general(bbx_i, expand(cl_i), NT_DIMS, preferred_element_type=F32))
        k_lag = k_lag.astype(BF16)
        kpair[(t_len - 1 - s) * LANES:(t_len - s) * LANES, LANES:] = k_lag
        if s <= t_len - 2:
            kpair[(t_len - 2 - s) * LANES:(t_len - 1 - s) * LANES, :LANES] = k_lag

    uc = ucat[...]
    hl_r[...] = jnp.dot(uc, bend_r[...], preferred_element_type=F32)
    hl_i[...] = jnp.dot(uc, bend_i[...], preferred_element_type=F32)

    cps = m // n_seq
    if cps == 1:
        h_r = h0r_ref[0]
        h_i = h0i_ref[0]
        hp_r[...] = h_r
        hp_i[...] = h_i
        hr_out[0] = ar * h_r - ai * h_i + hl_r[...]
        hi_out[0] = ar * h_i + ai * h_r + hl_i[...]
    else:
        def body(c, carry):
            new = []
            for n in range(n_seq):
                h_r, h_i = carry[n]
                row = pl.ds(n * cps + c, 1)
                hp_r[row, :] = h_r
                hp_i[row, :] = h_i
                new.append((ar * h_r - ai * h_i + hl_r[row, :], ar * h_i + ai * h_r + hl_i[row, :]))
            return tuple(new)

        init = tuple((h0r_ref[0, n:n + 1, :], h0i_ref[0, n:n + 1, :]) for n in range(n_seq))
        fin = lax.fori_loop(0, cps, body, init)
        for n in range(n_seq):
            hr_out[0, n:n + 1, :] = fin[n][0]
            hi_out[0, n:n + 1, :] = fin[n][1]

    ycar = (lax.dot_general(hp_r[...].astype(BF16), cend_r[...], NT_DIMS, preferred_element_type=F32)
            + lax.dot_general(hp_i[...].astype(BF16), cend_i[...], NT_DIMS, preferred_element_type=F32))
    dvec = d_ref[0]
    for t in range(0, t_len, 2):
        y2 = jnp.dot(ucat[:, :(t + 2) * LANES], kpair[(t_len - 2 - t) * LANES:, :],
                     preferred_element_type=F32) + ycar[:, t * LANES:(t + 2) * LANES]
        for k in range(2):
            y = y2[:, k * LANES:(k + 1) * LANES] + dvec * u_ref[t + k].astype(F32)
            y_ref[t + k] = jax.nn.gelu(y).astype(BF16)


def _s5_params(lam_re, lam_im, log_dt, b_re, b_im, c_re, c_im, d_skip):
    row = lambda x: x.astype(F32).reshape(N_BLOCKS, 1, STATE_LANES)
    ldt = jnp.broadcast_to(log_dt.astype(F32)[:, None], (N_GROUPS, N_STATE))
    bt = lambda x: (x.astype(F32).reshape(N_BLOCKS, GROUPS_PER_BLOCK, N_STATE, GROUP)
                    .transpose(0, 3, 1, 2).reshape(N_BLOCKS, GROUP, STATE_LANES))
    ct = lambda x: (x.astype(F32).reshape(N_BLOCKS, GROUPS_PER_BLOCK, GROUP, N_STATE)
                    .transpose(0, 2, 1, 3).reshape(N_BLOCKS, GROUP, STATE_LANES))
    return (row(lam_re), row(lam_im), row(ldt), bt(b_re), bt(b_im), ct(c_re), ct(c_im),
            d_skip.astype(F32).reshape(N_BLOCKS, 1, LANES))


def _s5(proj3, h0_re, h0_im, s5p, buf0, buf1, conv_w, t_len, m, n_seq):
    kern = functools.partial(_s5_kernel, t_len=t_len, m=m, n_seq=n_seq)
    blk3 = lambda b: (b, 0, 0)
    row_spec = pl.BlockSpec((1, 1, STATE_LANES), blk3)
    mat_spec = pl.BlockSpec((1, GROUP, STATE_LANES), blk3)
    state_spec = pl.BlockSpec((1, n_seq, STATE_LANES), blk3)
    end_mat = pltpu.VMEM((t_len * LANES, STATE_LANES), BF16)
    chunk_state = pltpu.VMEM((m, STATE_LANES), F32)
    nbw = WIDTH // LANES
    seg_spec = lambda seg: pl.BlockSpec((t_len, m, LANES), lambda b: (0, 0, seg * nbw + b))
    return pl.pallas_call(
        kern,
        grid=(N_BLOCKS,),
        in_specs=[seg_spec(0),
                  state_spec, state_spec,
                  row_spec, row_spec, row_spec,
                  mat_spec, mat_spec, mat_spec, mat_spec,
                  pl.BlockSpec((1, 1, LANES), blk3),
                  seg_spec(2), seg_spec(3), seg_spec(4), seg_spec(5),
                  pl.BlockSpec((n_seq, LANES), lambda b: (0, b)),
                  pl.BlockSpec((n_seq, LANES), lambda b: (0, b)),
                  pl.BlockSpec((3, LANES), lambda b: (0, b))],
        out_specs=[seg_spec(0), state_spec, state_spec, seg_spec(0),
                   pl.BlockSpec((2, m, LANES), lambda b: (0, 0, b))],
        out_shape=[jax.ShapeDtypeStruct((t_len, m, WIDTH), BF16),
                   jax.ShapeDtypeStruct((N_BLOCKS, n_seq, STATE_LANES), F32),
                   jax.ShapeDtypeStruct((N_BLOCKS, n_seq, STATE_LANES), F32),
                   jax.ShapeDtypeStruct((t_len, m, WIDTH), BF16),
                   jax.ShapeDtypeStruct((2, m, WIDTH), F32)],
        scratch_shapes=[pltpu.VMEM((m, t_len * LANES), BF16),
                        end_mat, end_mat, end_mat, end_mat,
                        pltpu.VMEM((t_len * LANES, 2 * LANES), BF16),
                        chunk_state, chunk_state, chunk_state, chunk_state],
        compiler_params=_cparams(1),
        name="s5_scan",
    )(proj3, h0_re, h0_im, *s5p, proj3, proj3, proj3, proj3, buf0, buf1, conv_w)


def _glu_kernel(ya_ref, sza_ref, w_ref, b_ref, o_ref):
    ya = ya_ref[...]
    glu = jnp.dot(ya, w_ref[...], preferred_element_type=F32) + b_ref[...]
    o_ref[...] = (ya.astype(F32) * _sigmoid(glu) * sza_ref[...].astype(F32)).astype(BF16)


def _glu_cast_kernel(ya_ref, sza_ref, w_ref, b_ref, o_ref, wbf_ref, *, tn):
    w = w_ref[...].astype(BF16)
    wbf_ref[...] = w
    glu = jnp.dot(ya_ref[...], w, preferred_element_type=F32) + b_ref[...]
    ya = ya_ref[:, pl.ds(pl.multiple_of(pl.program_id(0) * tn, tn), tn)].astype(F32)
    o_ref[...] = (ya * _sigmoid(glu) * sza_ref[...].astype(F32)).astype(BF16)


def _glu(ya, proj, w_glu, b_glu, tm=1024, tn=512):
    rows = ya.shape[0]
    out_shape = jax.ShapeDtypeStruct((rows, WIDTH), BF16)
    if w_glu.dtype == BF16:
        return pl.pallas_call(
            _glu_kernel,
            grid=(rows // tm,),
            in_specs=[pl.BlockSpec((tm, WIDTH), lambda i: (i, 0)),
                      pl.BlockSpec((tm, WIDTH), lambda i: (i, 1)),
                      pl.BlockSpec((WIDTH, WIDTH), lambda i: (0, 0), pipeline_mode=pl.Buffered(1)),
                      pl.BlockSpec((1, WIDTH), lambda i: (0, 0))],
            out_specs=pl.BlockSpec((tm, WIDTH), lambda i: (i, 0)),
            out_shape=out_shape,
            compiler_params=_cparams(1),
            name="glu_gate",
        )(ya, proj, w_glu, b_glu.reshape(1, WIDTH))
    nb = WIDTH // tn
    w_spec = pl.BlockSpec((WIDTH, tn), lambda j: (0, j))
    return pl.pallas_call(
        functools.partial(_glu_cast_kernel, tn=tn),
        grid=(nb,),
        in_specs=[pl.BlockSpec((rows, WIDTH), lambda j: (0, 0)),
                  pl.BlockSpec((rows, tn), lambda j: (0, nb + j)),
                  w_spec,
                  pl.BlockSpec((1, tn), lambda j: (0, j))],
        out_specs=[pl.BlockSpec((rows, tn), lambda j: (0, j)), w_spec],
        out_shape=[out_shape, jax.ShapeDtypeStruct((WIDTH, WIDTH), BF16)],
        compiler_params=_cparams(1),
        name="glu_gate_cast",
    )(ya, proj, w_glu, b_glu.reshape(1, WIDTH))


PERM_ROWS = 512


def _to_natural(pm, t_len):
    rows, n = pm.shape
    c = rows // t_len
    cg = PERM_ROWS // t_len
    r_nat = lax.broadcasted_iota(jnp.int32, (PERM_ROWS, PERM_ROWS), 0)
    r_in = lax.broadcasted_iota(jnp.int32, (PERM_ROWS, PERM_ROWS), 1)
    src = (r_nat & (t_len - 1)) * cg + lax.shift_right_logical(r_nat, t_len.bit_length() - 1)
    perm = jnp.where(r_in == src, 1.0, 0.0).astype(BF16)
    pm3 = pm.reshape(t_len, c, n)
    out = []
    for g in range(c // cg):
        grp = pm3[:, g * cg:(g + 1) * cg, :].reshape(PERM_ROWS, n).astype(BF16)
        out.append(jnp.dot(perm, grp, preferred_element_type=F32).astype(BF16))
    return out[0] if len(out) == 1 else jnp.concatenate(out, axis=0)


def _gated_merge(a_ref, b_ref, wa, wb, sga_ref, sgb_ref, o_ref, natural):
    t_len, c, _ = a_ref.shape
    rows = t_len * c
    pa = jnp.dot(a_ref[...].reshape(rows, WIDTH), wa, preferred_element_type=F32)
    pb = jnp.dot(b_ref[...].reshape(rows, WIDTH), wb, preferred_element_type=F32)
    tn = pa.shape[1]
    pm = sga_ref[...].reshape(rows, tn) * pa + sgb_ref[...].reshape(rows, tn) * pb
    o_ref[...] = _to_natural(pm, t_len) if natural else pm.astype(BF16)


def _merge_kernel(a_ref, b_ref, wa_ref, wb_ref, sga_ref, sgb_ref, o_ref, *, natural):
    _gated_merge(a_ref, b_ref, wa_ref[...], wb_ref[...], sga_ref, sgb_ref, o_ref, natural)


def _merge_cast_kernel(a_ref, b_ref, wa_ref, wb_ref, sga_ref, sgb_ref, o_ref, wabf_ref, wbbf_ref, *, natural):
    wa = wa_ref[...].astype(BF16)
    wb = wb_ref[...].astype(BF16)
    wabf_ref[...] = wa
    wbbf_ref[...] = wb
    _gated_merge(a_ref, b_ref, wa, wb, sga_ref, sgb_ref, o_ref, natural)


def _merge(a_in3, b_in3, proj3, w_pa, w_pb, c, natural):
    t_len, m, _ = a_in3.shape
    cast = w_pa.dtype != BF16
    tn = 512 if cast else 1024
    nb = WIDTH // tn
    assert not cast or c == m
    assert not natural or (PERM_ROWS % t_len == 0 and c % (PERM_ROWS // t_len) == 0)
    w_spec = pl.BlockSpec((WIDTH, tn), lambda i, j: (0, j))
    out_spec = pl.BlockSpec((t_len * c, tn), lambda i, j: (i, j))
    out_shape = jax.ShapeDtypeStruct((t_len * m, D_MODEL), BF16)
    w_shape = jax.ShapeDtypeStruct((WIDTH, D_MODEL), BF16)
    return pl.pallas_call(
        functools.partial(_merge_cast_kernel if cast else _merge_kernel, natural=natural),
        grid=(m // c, D_MODEL // tn),
        in_specs=[pl.BlockSpec((t_len, c, WIDTH), lambda i, j: (0, i, 0)),
                  pl.BlockSpec((t_len, c, WIDTH), lambda i, j: (0, i, 0)),
                  w_spec, w_spec,
                  pl.BlockSpec((t_len, c, tn), lambda i, j: (0, i, 6 * nb + j)),
                  pl.BlockSpec((t_len, c, tn), lambda i, j: (0, i, 8 * nb + j))],
        out_specs=[out_spec, w_spec, w_spec] if cast else out_spec,
        out_shape=[out_shape, w_shape, w_shape] if cast else out_shape,
        compiler_params=_cparams(2),
        name="merge_cast" if cast else "merge",
    )(a_in3, b_in3, w_pa, w_pb, proj3, proj3)


def _outproj_kernel(m_ref, w_ref, x_ref, gate_ref, fg_ref, o_ref, *rest, tn, tiles_per_seq):
    j = pl.program_id(1)
    w = w_ref[...]
    if rest:
        (wbf_ref,) = rest
        w = w.astype(BF16)
        wbf_ref[...] = w
    part = jnp.dot(m_ref[...], w, preferred_element_type=F32)
    col = pl.ds(pl.multiple_of(j * tn, tn), tn)
    tm = o_ref.shape[0]
    if tiles_per_seq:
        gate = gate_ref[pl.ds(pl.program_id(0) // tiles_per_seq, 1), :]
        o_ref[:, col] = x_ref[...] + gate * part
    else:
        n_rows = gate_ref.shape[0]
        for r in range(0, tm, n_rows):
            o_ref[r:r + n_rows, col] = x_ref[r:r + n_rows, :] + gate_ref[...] * part[r:r + n_rows, :]

    @pl.when(j == pl.num_programs(1) - 1)
    def _():
        for r in range(0, tm, LANES):
            h = o_ref[r:r + LANES, :]
            o_ref[r:r + LANES, :] = h * lax.rsqrt(jnp.mean(h * h, axis=-1, keepdims=True) + EPS) * fg_ref[...]


def _outproj(merged, w_o, x2d, gate, final_g, tm, tiles_per_seq):
    rows = x2d.shape[0]
    cast = w_o.dtype != BF16
    tn = 512 if cast else 1024
    assert tiles_per_seq or tm % gate.shape[0] == 0
    assert not cast or tm == rows
    w_spec = pl.BlockSpec((D_MODEL, tn), lambda i, j: (0, j))
    out_spec = pl.BlockSpec((tm, D_MODEL), lambda i, j: (i, 0))
    out_shape = jax.ShapeDtypeStruct((rows, D_MODEL), F32)
    return pl.pallas_call(
        functools.partial(_outproj_kernel, tn=tn, tiles_per_seq=tiles_per_seq),
        grid=(rows // tm, D_MODEL // tn),
        in_specs=[pl.BlockSpec((tm, D_MODEL), lambda i, j: (i, 0)),
                  w_spec,
                  pl.BlockSpec((tm, tn), lambda i, j: (i, j)),
                  pl.BlockSpec((gate.shape[0], tn), lambda i, j: (0, j)),
                  pl.BlockSpec((1, D_MODEL), lambda i, j: (0, 0))],
        out_specs=[out_spec, w_spec] if cast else out_spec,
        out_shape=[out_shape, jax.ShapeDtypeStruct((D_MODEL, D_MODEL), BF16)] if cast else out_shape,
        compiler_params=_cparams(2),
        name="outproj_cast" if cast else "outproj",
    )(merged, w_o, x2d, gate, final_g.reshape(1, D_MODEL))


def _state_to_blocks(state):
    n = state.shape[0]
    re = state[..., 0].reshape(n, N_BLOCKS, STATE_LANES).transpose(1, 0, 2)
    im = state[..., 1].reshape(n, N_BLOCKS, STATE_LANES).transpose(1, 0, 2)
    return re, im


def _blocks_to_state(re, im):
    n = re.shape[1]
    re = re.transpose(1, 0, 2).reshape(n, N_GROUPS, N_STATE)
    im = im.transpose(1, 0, 2).reshape(n, N_GROUPS, N_STATE)
    return jnp.stack([re, im], axis=-1)


def _group(x2d, t_len, n_seq, ct, tm, mod, ssm0, conv0, p):
    m = x2d.shape[0] // t_len
    cps = m // n_seq
    natural = cps > 1
    tiles_per_seq = cps // ct if natural else 0
    shift, scale, gate = mod
    xn = _norm_mod(x2d, t_len, p["norm_g"], scale, shift, ct, tiles_per_seq)
    xn = xn.reshape(t_len * m, D_MODEL)
    if "w_in_bf" in p:
        proj = _inproj(xn, p["w_in_bf"], tm)
    else:
        proj, p["w_in_bf"] = _inproj_cast(xn, p["w_in"])
    proj3 = proj.reshape(t_len, m, IN_COLS)
    h0_re, h0_im = _state_to_blocks(ssm0)
    ya3, hr, hi, b_in3, v_last = _s5(proj3, h0_re, h0_im, p["s5"], conv0[:, 0], conv0[:, 1], p["conv_w"],
                                     t_len, m, n_seq)
    ya = ya3.reshape(t_len * m, WIDTH)
    if "w_glu_bf" in p:
        a_in = _glu(ya, proj, p["w_glu_bf"], p["b_glu"])
    else:
        a_in, p["w_glu_bf"] = _glu(ya, proj, p["w_glu"], p["b_glu"])
    a_in3 = a_in.reshape(t_len, m, WIDTH)
    if "w_pa_bf" in p:
        merged = _merge(a_in3, b_in3, proj3, p["w_pa_bf"], p["w_pb_bf"], tm // t_len, natural)
    else:
        merged, p["w_pa_bf"], p["w_pb_bf"] = _merge(a_in3, b_in3, proj3, p["w_pa"], p["w_pb"], m, natural)
    if "w_o_bf" in p:
        out = _outproj(merged, p["w_o_bf"], x2d, gate, p["final_g"], ct * t_len, tiles_per_seq)
    else:
        out, p["w_o_bf"] = _outproj(merged, p["w_o"], x2d, gate, p["final_g"], ct * t_len, tiles_per_seq)
    conv_new = v_last[:, cps - 1::cps, :].transpose(1, 0, 2)
    return out, _blocks_to_state(hr, hi), conv_new


def kernel(x_prompt, x_sample, state_ssm, state_conv, c_prompt, c_sample, norm_g, w_ada, b_ada, w_in, lam_re, lam_im, log_dt, b_re, b_im, c_re, c_im, d_skip, w_glu, b_glu, w_pa, conv_w, w_pb, w_o, final_g):
    depth = norm_g.shape[0]
    assert depth == 1
    n_p, seq, _ = x_prompt.shape
    n_s, dec, _ = x_sample.shape
    t_p = 16
    assert seq % t_p == 0

    l = 0
    p = dict(norm_g=norm_g[l], w_in=w_in[l], conv_w=conv_w[l], w_glu=w_glu[l], b_glu=b_glu[l],
             w_pa=w_pa[l], w_pb=w_pb[l], w_o=w_o[l], final_g=final_g,
             s5=_s5_params(lam_re[l], lam_im[l], log_dt[l], b_re[l], b_im[l], c_re[l], c_im[l], d_skip[l]))

    n_c = n_p + n_s
    pad = (-n_c) % 16
    c_all = jnp.concatenate([c_prompt, c_sample, jnp.zeros((pad, D_MODEL), F32)], axis=0)
    mod = _ada(c_all, w_ada[l], b_ada[l])
    mod_p = tuple(mod[:n_p, k * D_MODEL:(k + 1) * D_MODEL] for k in range(3))
    mod_s = tuple(mod[n_p:n_c, k * D_MODEL:(k + 1) * D_MODEL] for k in range(3))

    x_s = x_sample.transpose(1, 0, 2).reshape(dec * n_s, D_MODEL)
    y_s, ssm_s, conv_s = _group(x_s, dec, n_s, n_s, dec * n_s, mod_s, state_ssm[l], state_conv[l], p)
    y_sample = y_s.reshape(dec, n_s, D_MODEL).transpose(1, 0, 2)

    ssm_p0 = jnp.zeros((n_p, N_GROUPS, N_STATE, 2), F32)
    conv_p0 = jnp.zeros((n_p, 2, WIDTH), F32)
    y_p, ssm_p, conv_p = _group(x_prompt.reshape(n_p * seq, D_MODEL), t_p, n_p, 32, 1024,
                                mod_p, ssm_p0, conv_p0, p)
    y_prompt = y_p.reshape(n_p, seq, D_MODEL)

    return (y_prompt, y_sample, ssm_p[None], conv_p[None], ssm_s[None], conv_s[None])
```
